```python
import jax, jax.numpy as jnp
from jax import lax
import numpy as np

D_MODEL = 4096
BATCH = 2
SEQ = 8192
DEPTH = 2
DEC_BATCH = 16
DEC_SEQ = 64
PAST_LEN = 2048

CHUNK = 64
D_MIX = D_MODEL
A_VDIM = 128
A_HEADS = D_MIX // 2 // A_VDIM
A_NOPE = 128
A_ROPE = 64
Q_LORA = 1024
KV_LORA = 512
B_DIM = 128
B_HEADS = D_MIX // 2 // B_DIM
BAND_PREV = 8
MAX_REL = 128
MEM_TOKENS = 256
MEM_HEADS = 4
MEM_DIM = 256
MEM_W = MEM_HEADS * MEM_DIM
D_FF = 11008
CONV_W = 3

ROPE_THETA = 10000.0
EPS = 1e-6
Q_BLOCK = 128
NEG_INF = -1e30
MLA_SCALE = (A_NOPE + A_ROPE) ** -0.5
B_SCALE = B_DIM ** -0.5
MEM_SCALE = MEM_DIM ** -0.5
IN_COLS = Q_LORA + KV_LORA + A_ROPE + 3 * B_HEADS * B_DIM

kernel_name = "hybrid_mla_band_stream_step"


def _rmsnorm(x, g):
    x32 = x.astype(jnp.float32)
    y = x32 * lax.rsqrt(jnp.mean(x32 * x32, axis=-1, keepdims=True) + EPS)
    return (y * g.astype(jnp.float32)).astype(x.dtype)


def _rope(x, pos):
    half = A_ROPE // 2
    inv = ROPE_THETA ** (-jnp.arange(half, dtype=jnp.float32) / half)
    ang = pos.astype(jnp.float32)[:, None] * inv[None, :]
    cos = jnp.cos(ang)[None, :, None, :]
    sin = jnp.sin(ang)[None, :, None, :]
    x32 = x.astype(jnp.float32)
    x1, x2 = x32[..., :half], x32[..., half:]
    return jnp.concatenate([x1 * cos - x2 * sin, x1 * sin + x2 * cos], axis=-1).astype(x.dtype)


def _mix_project(h, pos, w_in, norm_cq, norm_ckv, w_uq):
    b, s, _ = h.shape
    proj = h @ w_in
    c_q, c_kv, k_rope, qkv_b = jnp.split(
        proj, [Q_LORA, Q_LORA + KV_LORA, Q_LORA + KV_LORA + A_ROPE], axis=-1)
    q = (_rmsnorm(c_q, norm_cq) @ w_uq).reshape(b, s, A_HEADS, A_NOPE + A_ROPE)
    q_nope = q[..., :A_NOPE]
    q_rope = _rope(q[..., A_NOPE:], pos)
    c_kv = _rmsnorm(c_kv, norm_ckv)
    k_rope = _rope(k_rope[:, :, None, :], pos)[:, :, 0, :]
    qkv_b = qkv_b.reshape(b, s, 3, B_HEADS, B_DIM)
    return q_nope, q_rope, c_kv, k_rope, qkv_b[:, :, 0], qkv_b[:, :, 1], qkv_b[:, :, 2]


def _mla_decompress(c_kv, w_uk, w_uv):
    k_nope = jnp.einsum('bkl,lhn->bkhn', c_kv, w_uk)
    v = jnp.einsum('bkl,lhv->bkhv', c_kv, w_uv)
    return k_nope, v


def _mla_core(q_nope, q_rope, k_nope, k_rope, v, q_pos, k_pos):
    s = (jnp.einsum('bqhn,bkhn->bhqk', q_nope, k_nope)
         + jnp.einsum('bqhr,bkr->bhqk', q_rope, k_rope)).astype(jnp.float32) * MLA_SCALE
    mask = (k_pos[None, :] // CHUNK) <= (q_pos[:, None] // CHUNK)
    s = jnp.where(mask[None, None], s, NEG_INF)
    p = jax.nn.softmax(s, axis=-1).astype(v.dtype)
    return jnp.einsum('bhqk,bkhv->bqhv', p, v)


def _mla_prompt(q_nope, q_rope, k_nope, k_rope, v):
    b, s = q_nope.shape[:2]
    nb = s // Q_BLOCK
    k_pos = jnp.arange(s, dtype=jnp.int32)

    def to_blocks(t):
        return jnp.moveaxis(t.reshape(b, nb, Q_BLOCK, *t.shape[2:]), 1, 0)

    def blk(args):
        qn, qr, qp = args
        return _mla_core(qn, qr, k_nope, k_rope, v, qp, k_pos)

    o = lax.map(blk, (to_blocks(q_nope), to_blocks(q_rope), k_pos.reshape(nb, Q_BLOCK)))
    return jnp.moveaxis(o, 0, 1).reshape(b, s, A_HEADS * A_VDIM)


def _band_core(q, k, v, q_pos, k_pos, rel_bias):
    s = jnp.einsum('bqhd,bkhd->bhqk', q, k).astype(jnp.float32) * B_SCALE
    rel = jnp.clip(q_pos[:, None] - k_pos[None, :], -MAX_REL, MAX_REL) + MAX_REL
    s = s + rel_bias.astype(jnp.float32)[:, rel][None]
    qc = q_pos[:, None] // CHUNK
    kc = k_pos[None, :] // CHUNK
    mask = (kc <= qc) & (kc >= qc - BAND_PREV) & (k_pos[None, :] >= 0)
    s = jnp.where(mask[None, None], s, NEG_INF)
    p = jax.nn.softmax(s, axis=-1).astype(v.dtype)
    return jnp.einsum('bhqk,bkhd->bqhd', p, v)


def _band_prompt(q, k, v, rel_bias):
    b, s = q.shape[:2]
    nc = s // CHUNK
    pad = BAND_PREV * CHUNK
    band = pad + CHUNK
    kp = jnp.pad(k, ((0, 0), (pad, 0), (0, 0), (0, 0)))
    vp = jnp.pad(v, ((0, 0), (pad, 0), (0, 0), (0, 0)))
    q_chunks = jnp.moveaxis(q.reshape(b, nc, CHUNK, B_HEADS, B_DIM), 1, 0)

    def blk(args):
        qc, c = args
        start = c * CHUNK
        kc = lax.dynamic_slice_in_dim(kp, start, band, axis=1)
        vc = lax.dynamic_slice_in_dim(vp, start, band, axis=1)
        q_pos = start + jnp.arange(CHUNK, dtype=jnp.int32)
        k_pos = start - pad + jnp.arange(band, dtype=jnp.int32)
        return _band_core(qc, kc, vc, q_pos, k_pos, rel_bias)

    o = lax.map(blk, (q_chunks, jnp.arange(nc, dtype=jnp.int32)))
    return jnp.moveaxis(o, 0, 1).reshape(b, s, B_HEADS * B_DIM)


def _merge(oa, ob, g_out_a, g_out_b, w_o):
    return jnp.concatenate([_rmsnorm(oa, g_out_a), _rmsnorm(ob, g_out_b)], axis=-1) @ w_o


def _mem_kv(mem, norm_memtok, w_mkv):
    b, m, _ = mem.shape
    kv = (_rmsnorm(mem, norm_memtok) @ w_mkv).reshape(b, m, 2, MEM_HEADS, MEM_DIM)
    return kv[:, :, 0], kv[:, :, 1]


def _mem_attend(h, mk, mv, w_mq, w_mo):
    b, s, _ = h.shape
    q = (h @ w_mq).reshape(b, s, MEM_HEADS, MEM_DIM)
    sc = jnp.einsum('bqhd,bmhd->bhqm', q, mk).astype(jnp.float32) * MEM_SCALE
    p = jax.nn.softmax(sc, axis=-1).astype(mv.dtype)
    o = jnp.einsum('bhqm,bmhd->bqhd', p, mv).reshape(b, s, MEM_W)
    return o @ w_mo


def _conv_ffn(h, past_g, w_up, w_dw, b_dw, w_down):
    s = h.shape[1]
    a, g = jnp.split(h @ w_up, 2, axis=-1)
    g_all = jnp.concatenate([past_g, g], axis=1)
    gc = b_dw
    for j in range(CONV_W):
        gc = gc + w_dw[j] * g_all[:, j:j + s]
    y = (a * jax.nn.silu(gc)) @ w_down
    return y, g_all[:, s:]


def setup_inputs(seed: int = 0) -> dict:
    key = jax.random.key(seed)
    ks = iter(jax.random.split(key, 48))
    f32 = jnp.float32

    def nrm(shape, scale):
        return jax.random.normal(next(ks), shape, f32) * scale

    def gain(shape):
        return 1.0 + 0.01 * jax.random.normal(next(ks), shape, f32)

    band_keep = min(BAND_PREV * CHUNK, PAST_LEN)
    return {
        "x_prompt": nrm((BATCH, SEQ, D_MODEL), 1.0),
        "x_sample": nrm((DEC_BATCH, DEC_SEQ, D_MODEL), 1.0),
        "mem_prompt": nrm((BATCH, MEM_TOKENS, D_MODEL), 1.0),
        "cache_mla_ckv": nrm((DEPTH, DEC_BATCH, PAST_LEN, KV_LORA), 1.0),
        "cache_mla_krope": nrm((DEPTH, DEC_BATCH, PAST_LEN, A_ROPE), 1.0),
        "cache_band_k": nrm((DEPTH, DEC_BATCH, band_keep, B_HEADS, B_DIM), 1.0),
        "cache_band_v": nrm((DEPTH, DEC_BATCH, band_keep, B_HEADS, B_DIM), 1.0),
        "cache_mem_k": nrm((DEPTH, DEC_BATCH, MEM_TOKENS, MEM_HEADS, MEM_DIM), 1.0),
        "cache_mem_v": nrm((DEPTH, DEC_BATCH, MEM_TOKENS, MEM_HEADS, MEM_DIM), 1.0),
        "state_conv": nrm((DEPTH, DEC_BATCH, CONV_W - 1, D_FF), 1.0),
        "norm_mix": gain((DEPTH, D_MODEL)),
        "w_in": nrm((DEPTH, D_MODEL, IN_COLS), D_MODEL ** -0.5),
        "norm_cq": gain((DEPTH, Q_LORA)),
        "norm_ckv": gain((DEPTH, KV_LORA)),
        "w_uq": nrm((DEPTH, Q_LORA, A_HEADS * (A_NOPE + A_ROPE)), Q_LORA ** -0.5),
        "w_uk": nrm((DEPTH, KV_LORA, A_HEADS, A_NOPE), KV_LORA ** -0.5),
        "w_uv": nrm((DEPTH, KV_LORA, A_HEADS, A_VDIM), KV_LORA ** -0.5),
        "rel_bias": nrm((DEPTH, B_HEADS, 2 * MAX_REL + 1), 0.1),
        "g_out_a": gain((DEPTH, A_HEADS * A_VDIM)),
        "g_out_b": gain((DEPTH, B_HEADS * B_DIM)),
        "w_o": nrm((DEPTH, D_MIX, D_MODEL), D_MIX ** -0.5),
        "norm_mem": gain((DEPTH, D_MODEL)),
        "norm_memtok": gain((DEPTH, D_MODEL)),
        "w_mq": nrm((DEPTH, D_MODEL, MEM_W), D_MODEL ** -0.5),
        "w_mkv": nrm((DEPTH, D_MODEL, 2 * MEM_W), D_MODEL ** -0.5),
        "w_mo": nrm((DEPTH, MEM_W, D_MODEL), MEM_W ** -0.5),
        "norm_ffn": gain((DEPTH, D_MODEL)),
        "w_up": nrm((DEPTH, D_MODEL, 2 * D_FF), D_MODEL ** -0.5),
        "w_dw": nrm((DEPTH, CONV_W, D_FF), CONV_W ** -0.5),
        "b_dw": nrm((DEPTH, D_FF), 0.01),
        "w_down": nrm((DEPTH, D_FF, D_MODEL), D_FF ** -0.5),
        "norm_final": gain((D_MODEL,)),
    }


def reference(x_prompt, x_sample, mem_prompt, cache_mla_ckv, cache_mla_krope, cache_band_k,
              cache_band_v, cache_mem_k, cache_mem_v, state_conv, norm_mix, w_in, norm_cq,
              norm_ckv, w_uq, w_uk, w_uv, rel_bias, g_out_a, g_out_b, w_o, norm_mem,
              norm_memtok, w_mq, w_mkv, w_mo, norm_ffn, w_up, w_dw, b_dw, w_down, norm_final):
    b, s, _ = x_prompt.shape
    db, t, _ = x_sample.shape
    n_past = cache_mla_ckv.shape[2]
    n_band = cache_band_k.shape[2]
    pos_p = jnp.arange(s, dtype=jnp.int32)
    pos_s = n_past + jnp.arange(t, dtype=jnp.int32)
    mla_kpos_s = jnp.arange(n_past + t, dtype=jnp.int32)
    band_kpos_s = n_past - n_band + jnp.arange(n_band + t, dtype=jnp.int32)
    band_keep = min(BAND_PREV * CHUNK, s)

    xp, xs = x_prompt, x_sample
    p_ckv, p_krope, p_bk, p_bv, p_mk, p_mv, p_conv = [], [], [], [], [], [], []
    s_ckv, s_krope, s_bk, s_bv, s_conv = [], [], [], [], []

    for l in range(DEPTH):
        h = _rmsnorm(xp, norm_mix[l])
        qn, qr, ckv, kr, qb, kb, vb = _mix_project(h, pos_p, w_in[l], norm_cq[l], norm_ckv[l], w_uq[l])
        k_nope, v_a = _mla_decompress(ckv, w_uk[l], w_uv[l])
        oa = _mla_prompt(qn, qr, k_nope, kr, v_a)
        ob = _band_prompt(qb, kb, vb, rel_bias[l])
        xp = xp + _merge(oa, ob, g_out_a[l], g_out_b[l], w_o[l])
        mk, mv = _mem_kv(mem_prompt, norm_memtok[l], w_mkv[l])
        xp = xp + _mem_attend(_rmsnorm(xp, norm_mem[l]), mk, mv, w_mq[l], w_mo[l])
        y, conv_p = _conv_ffn(_rmsnorm(xp, norm_ffn[l]), jnp.zeros((b, CONV_W - 1, D_FF), xp.dtype),
                              w_up[l], w_dw[l], b_dw[l], w_down[l])
        xp = xp + y
        p_ckv.append(ckv)
        p_krope.append(kr)
        p_bk.append(kb[:, s - band_keep:])
        p_bv.append(vb[:, s - band_keep:])
        p_mk.append(mk)
        p_mv.append(mv)
        p_conv.append(conv_p)

        h = _rmsnorm(xs, norm_mix[l])
        qn, qr, ckv, kr, qb, kb, vb = _mix_project(h, pos_s, w_in[l], norm_cq[l], norm_ckv[l], w_uq[l])
        ckv_all = jnp.concatenate([cache_mla_ckv[l], ckv], axis=1)
        kr_all = jnp.concatenate([cache_mla_krope[l], kr], axis=1)
        k_nope, v_a = _mla_decompress(ckv_all, w_uk[l], w_uv[l])
        oa = _mla_core(qn, qr, k_nope, kr_all, v_a, pos_s, mla_kpos_s).reshape(db, t, A_HEADS * A_VDIM)
        kb_all = jnp.concatenate([cache_band_k[l], kb], axis=1)
        vb_all = jnp.concatenate([cache_band_v[l], vb], axis=1)
        ob = _band_core(qb, kb_all, vb_all, pos_s, band_kpos_s, rel_bias[l]).reshape(db, t, B_HEADS * B_DIM)
        xs = xs + _merge(oa, ob, g_out_a[l], g_out_b[l], w_o[l])
        xs = xs + _mem_attend(_rmsnorm(xs, norm_mem[l]), cache_mem_k[l], cache_mem_v[l], w_mq[l], w_mo[l])
        y, conv_s = _conv_ffn(_rmsnorm(xs, norm_ffn[l]), state_conv[l], w_up[l], w_dw[l], b_dw[l], w_down[l])
        xs = xs + y
        s_ckv.append(ckv)
        s_krope.append(kr)
        s_bk.append(kb)
        s_bv.append(vb)
        s_conv.append(conv_s)

    y_prompt = _rmsnorm(xp, norm_final)
    y_sample = _rmsnorm(xs, norm_final)
    return (y_prompt, y_sample,
            jnp.stack(p_ckv), jnp.stack(p_krope), jnp.stack(p_bk), jnp.stack(p_bv),
            jnp.stack(p_mk), jnp.stack(p_mv), jnp.stack(p_conv),
            jnp.stack(s_ckv), jnp.stack(s_krope), jnp.stack(s_bk), jnp.stack(s_bv),
            jnp.stack(s_conv))
```

```python
import functools
import math

import jax
import jax.numpy as jnp
from jax import lax
from jax.experimental import pallas as pl
from jax.experimental.pallas import tpu as pltpu

CHUNK = 64
BAND_PREV = 8
MAX_REL = 128
CONV_W = 3
ROPE_THETA = 10000.0
EPS = 1e-6
NEG_INF = -1e30

LANES = 128
HEAD_SLOT = 256
VMEM_LIMIT = 56 * 1024 * 1024

F32 = jnp.float32
BF16 = jnp.bfloat16


def _cp(*sem):
    return pltpu.CompilerParams(dimension_semantics=sem, vmem_limit_bytes=VMEM_LIMIT)


def _pick(n, prefs):
    for p in prefs:
        if n % p == 0:
            return p
    return n


def _round_up(n, m):
    return (n + m - 1) // m * m


def _rmsnorm_rows(x, g):
    ms = jnp.mean(x * x, axis=-1, keepdims=True)
    return (x * lax.rsqrt(ms + EPS)) * g


def _rmsnorm_kernel(x_ref, g_ref, o_ref):
    o_ref[...] = _rmsnorm_rows(x_ref[...].astype(F32), g_ref[...]).astype(o_ref.dtype)


def rmsnorm(x, g, out_dtype, *, row0=0, rows=None, name="rmsnorm"):
    m, d = x.shape
    rows = m if rows is None else rows
    tm = _pick(math.gcd(rows, row0) if row0 else rows, (256, 128, 64, 32, 16, 8))
    off = row0 // tm
    return pl.pallas_call(
        _rmsnorm_kernel,
        grid=(rows // tm,),
        in_specs=[pl.BlockSpec((tm, d), lambda i: (i + off, 0)),
                  pl.BlockSpec((1, d), lambda i: (0, 0))],
        out_specs=pl.BlockSpec((tm, d), lambda i: (i, 0)),
        out_shape=jax.ShapeDtypeStruct((rows, d), out_dtype),
        compiler_params=_cp("parallel"),
        name=name,
    )(x, g.reshape(1, d).astype(F32))


def _mm_kernel(x_ref, w_ref, o_ref):
    o_ref[...] = jnp.dot(x_ref[...].astype(BF16), w_ref[...],
                         preferred_element_type=F32).astype(o_ref.dtype)


def matmul(x, w, out_dtype, *, tm_prefs=(1024, 512, 256, 128, 64, 32, 16, 8),
           tn_prefs=(512, 256, 128), name="matmul"):
    m, k = x.shape
    n = w.shape[1]
    tm, tn = _pick(m, tm_prefs), _pick(n, tn_prefs)
    return pl.pallas_call(
        _mm_kernel,
        grid=(m // tm, n // tn),
        in_specs=[pl.BlockSpec((tm, k), lambda i, j: (i, 0)),
                  pl.BlockSpec((k, tn), lambda i, j: (0, j))],
        out_specs=pl.BlockSpec((tm, tn), lambda i, j: (i, j)),
        out_shape=jax.ShapeDtypeStruct((m, n), out_dtype),
        compiler_params=_cp("parallel", "parallel"),
        name=name,
    )(x, w)


def _mm_res_kernel(x_ref, w_ref, r_ref, o_ref):
    o_ref[...] = r_ref[...] + jnp.dot(x_ref[...], w_ref[...], preferred_element_type=F32)


def matmul_residual(x, w, res, *, name="matmul_res"):
    m, k = x.shape
    n = w.shape[1]
    tm, tn = _pick(m, (1024, 512, 256, 128)), _pick(n, (512, 256, 128))
    return pl.pallas_call(
        _mm_res_kernel,
        grid=(m // tm, n // tn),
        in_specs=[pl.BlockSpec((tm, k), lambda i, j: (i, 0)),
                  pl.BlockSpec((k, tn), lambda i, j: (0, j)),
                  pl.BlockSpec((tm, tn), lambda i, j: (i, j))],
        out_specs=pl.BlockSpec((tm, tn), lambda i, j: (i, j)),
        out_shape=jax.ShapeDtypeStruct((m, n), F32),
        input_output_aliases={2: 0},
        compiler_params=_cp("parallel", "parallel"),
        name=name,
    )(x, w, res)


def _mm_res_ktiled_kernel(x_ref, w_ref, r_ref, o_ref, acc_ref):
    kk = pl.program_id(2)

    @pl.when(kk == 0)
    def _():
        acc_ref[...] = jnp.zeros_like(acc_ref)

    acc_ref[...] += jnp.dot(x_ref[...], w_ref[...], preferred_element_type=F32)

    @pl.when(kk == pl.num_programs(2) - 1)
    def _():
        o_ref[...] = r_ref[...] + acc_ref[...]


def matmul_residual_ktiled(x, w, res, *, tk, name="matmul_res_k"):
    m, k = x.shape
    n = w.shape[1]
    tm, tn = _pick(m, (1024, 512, 256, 128)), _pick(n, (1024, 512, 256, 128))
    return pl.pallas_call(
        _mm_res_ktiled_kernel,
        grid=(m // tm, n // tn, k // tk),
        in_specs=[pl.BlockSpec((tm, tk), lambda i, j, kk: (i, kk)),
                  pl.BlockSpec((tk, tn), lambda i, j, kk: (kk, j)),
                  pl.BlockSpec((tm, tn), lambda i, j, kk: (i, j))],
        out_specs=pl.BlockSpec((tm, tn), lambda i, j, kk: (i, j)),
        out_shape=jax.ShapeDtypeStruct((m, n), F32),
        scratch_shapes=[pltpu.VMEM((tm, tn), F32)],
        input_output_aliases={2: 0},
        compiler_params=_cp("parallel", "parallel", "arbitrary"),
        name=name,
    )(x, w, res)


def _rope_slot(r, cos, sin):
    return r * cos + pltpu.roll(r, 64, 1) * sin


def _lat_kernel(h_ref, w_ref, gq_ref, gkv_ref, cos_ref, sin_ref,
                cq_ref, ckv_ref, ckvb_ref, kr_ref, krb_ref, *, ql, kvl):
    acc = jnp.dot(h_ref[...], w_ref[...], preferred_element_type=F32)
    cq_ref[...] = _rmsnorm_rows(acc[:, :ql], gq_ref[...]).astype(BF16)
    ckv = _rmsnorm_rows(acc[:, ql:ql + kvl], gkv_ref[...])
    ckv_ref[...] = ckv
    ckvb_ref[...] = ckv.astype(BF16)
    kr = _rope_slot(acc[:, ql + kvl:], cos_ref[...], sin_ref[...])
    kr_ref[...] = kr
    krb_ref[...] = kr.astype(BF16)


def latent_project(h, w_lat, g_cq, g_ckv, cos_t, sin_t):
    m, d = h.shape
    ql, kvl = g_cq.shape[0], g_ckv.shape[0]
    n = w_lat.shape[1]
    tm = _pick(m, (512, 256, 128))
    row = lambda i: (i, 0)
    fix = lambda i: (0, 0)
    return pl.pallas_call(
        functools.partial(_lat_kernel, ql=ql, kvl=kvl),
        grid=(m // tm,),
        in_specs=[pl.BlockSpec((tm, d), row), pl.BlockSpec((d, n), fix),
                  pl.BlockSpec((1, ql), fix), pl.BlockSpec((1, kvl), fix),
                  pl.BlockSpec((tm, LANES), row), pl.BlockSpec((tm, LANES), row)],
        out_specs=[pl.BlockSpec((tm, ql), row), pl.BlockSpec((tm, kvl), row),
                   pl.BlockSpec((tm, kvl), row), pl.BlockSpec((tm, LANES), row),
                   pl.BlockSpec((tm, LANES), row)],
        out_shape=[jax.ShapeDtypeStruct((m, ql), BF16), jax.ShapeDtypeStruct((m, kvl), F32),
                   jax.ShapeDtypeStruct((m, kvl), BF16), jax.ShapeDtypeStruct((m, LANES), F32),
                   jax.ShapeDtypeStruct((m, LANES), BF16)],
        compiler_params=_cp("parallel"),
        name="latent_project",
    )(h, w_lat, g_cq.reshape(1, ql), g_ckv.reshape(1, kvl), cos_t, sin_t)


def _q_kernel(c_ref, w_ref, cos_ref, sin_ref, o_ref, *, heads, scale):
    acc = jnp.dot(c_ref[...], w_ref[...], preferred_element_type=F32)
    cos, sin = cos_ref[...], sin_ref[...]
    for h in range(heads):
        lo = h * HEAD_SLOT
        o_ref[:, lo:lo + LANES] = (acc[:, lo:lo + LANES] * scale).astype(BF16)
        r = _rope_slot(acc[:, lo + LANES:lo + HEAD_SLOT], cos, sin)
        o_ref[:, lo + LANES:lo + HEAD_SLOT] = (r * scale).astype(BF16)


def q_project(cq, w_uq, cos_t, sin_t, scale):
    m, ql = cq.shape
    n = w_uq.shape[1]
    tm = _pick(m, (512, 256, 128))
    tn = _pick(n, (1024, 512, 256))
    return pl.pallas_call(
        functools.partial(_q_kernel, heads=tn // HEAD_SLOT, scale=scale),
        grid=(m // tm, n // tn),
        in_specs=[pl.BlockSpec((tm, ql), lambda i, j: (i, 0)),
                  pl.BlockSpec((ql, tn), lambda i, j: (0, j)),
                  pl.BlockSpec((tm, LANES), lambda i, j: (i, 0)),
                  pl.BlockSpec((tm, LANES), lambda i, j: (i, 0))],
        out_specs=pl.BlockSpec((tm, tn), lambda i, j: (i, j)),
        out_shape=jax.ShapeDtypeStruct((m, n), BF16),
        compiler_params=_cp("parallel", "parallel"),
        name="q_project",
    )(cq, w_uq, cos_t, sin_t)


def _kv_kernel(c_ref, kr_ref, perm_ref, wk_ref, wv_ref, k_ref, v_ref, *, heads, permute):
    c = c_ref[...].astype(BF16)
    kn = jnp.dot(c, wk_ref[...], preferred_element_type=F32)
    v_ref[...] = jnp.dot(c, wv_ref[...], preferred_element_type=F32).astype(BF16)
    kr = kr_ref[...].astype(BF16)
    if permute:
        kr = jnp.dot(kr, perm_ref[...], preferred_element_type=F32).astype(BF16)
    for h in range(heads):
        lo = h * HEAD_SLOT
        k_ref[:, lo:lo + LANES] = kn[:, h * LANES:(h + 1) * LANES].astype(BF16)
        k_ref[:, lo + LANES:lo + HEAD_SLOT] = kr


def kv_decompress(ckv, kr, w_uk, w_uv, perm, *, permute):
    m, kvl = ckv.shape
    heads = w_uk.shape[1] // LANES
    tm = _pick(m, (512, 256, 128))
    row = lambda i: (i, 0)
    fix = lambda i: (0, 0)
    return pl.pallas_call(
        functools.partial(_kv_kernel, heads=heads, permute=permute),
        grid=(m // tm,),
        in_specs=[pl.BlockSpec((tm, kvl), row), pl.BlockSpec((tm, kr.shape[1]), row),
                  pl.BlockSpec(perm.shape, fix),
                  pl.BlockSpec(w_uk.shape, fix), pl.BlockSpec(w_uv.shape, fix)],
        out_specs=[pl.BlockSpec((tm, heads * HEAD_SLOT), row), pl.BlockSpec((tm, heads * LANES), row)],
        out_shape=[jax.ShapeDtypeStruct((m, heads * HEAD_SLOT), BF16),
                   jax.ShapeDtypeStruct((m, heads * LANES), BF16)],
        compiler_params=_cp("parallel"),
        name="kv_decompress",
    )(ckv, kr, perm, w_uk, w_uv)


def _qk(q, k):
    return lax.dot_general(q, k, (((1,), (1,)), ((), ())), preferred_element_type=F32)


def _mla_prompt_kernel(q_ref, k_ref, v_ref, o_ref, m_ref, l_ref, acc_ref, *, tq):
    qi = pl.program_id(2)
    q = q_ref[...]
    m_ref[...] = jnp.full_like(m_ref, NEG_INF)
    l_ref[...] = jnp.zeros_like(l_ref)
    acc_ref[...] = jnp.zeros_like(acc_ref)

    def step(start, mask):
        s = _qk(q, k_ref[pl.ds(start, tq), :])
        if mask is not None:
            s = jnp.where(mask, s, NEG_INF)
        m_old = m_ref[...]
        m_new = jnp.maximum(m_old, jnp.max(s, axis=-1, keepdims=True))
        alpha = jnp.exp(m_old - m_new)
        p = jnp.exp(s - m_new)
        l_ref[...] = alpha * l_ref[...] + jnp.sum(p, axis=-1, keepdims=True)
        acc_ref[...] = alpha * acc_ref[...] + jnp.dot(
            p.astype(BF16), v_ref[pl.ds(start, tq), :], preferred_element_type=F32)
        m_ref[...] = m_new

    def body(ki, carry):
        step(pl.multiple_of(ki * tq, tq), None)
        return carry

    lax.fori_loop(0, qi, body, 0)
    rc = lax.broadcasted_iota(jnp.int32, (tq, tq), 0) // CHUNK
    cc = lax.broadcasted_iota(jnp.int32, (tq, tq), 1) // CHUNK
    step(pl.multiple_of(qi * tq, tq), cc <= rc)
    o_ref[...] = acc_ref[...] / l_ref[...]


def mla_prompt(q, k, v, batch, seq, heads):
    t = q.shape[0]
    tq = _pick(seq, (512, 256, 128, 64))
    nq = seq // tq
    return pl.pallas_call(
        functools.partial(_mla_prompt_kernel, tq=tq),
        grid=(batch, heads, nq),
        in_specs=[pl.BlockSpec((tq, HEAD_SLOT), lambda b, h, i: (b * nq + i, h)),
                  pl.BlockSpec((seq, HEAD_SLOT), lambda b, h, i: (b, h)),
                  pl.BlockSpec((seq, LANES), lambda b, h, i: (b, h))],
        out_specs=pl.BlockSpec((tq, LANES), lambda b, h, i: (b * nq + i, h)),
        out_shape=jax.ShapeDtypeStruct((t, heads * LANES), F32),
        scratch_shapes=[pltpu.VMEM((tq, 1), F32), pltpu.VMEM((tq, 1), F32),
                        pltpu.VMEM((tq, LANES), F32)],
        compiler_params=_cp("parallel", "parallel", "arbitrary"),
        name="mla_prompt",
    )(q, k, v)


def _mla_sample_kernel(q_ref, kc_ref, vc_ref, kn_ref, vn_ref, oin_ref, o_ref):
    del oin_ref
    q = q_ref[...]
    s1 = _qk(q, kc_ref[...])
    s2 = _qk(q, kn_ref[...])
    m = jnp.maximum(jnp.max(s1, axis=-1, keepdims=True), jnp.max(s2, axis=-1, keepdims=True))
    p1 = jnp.exp(s1 - m)
    p2 = jnp.exp(s2 - m)
    l = jnp.sum(p1, axis=-1, keepdims=True) + jnp.sum(p2, axis=-1, keepdims=True)
    o = (jnp.dot(p1.astype(BF16), vc_ref[...], preferred_element_type=F32)
         + jnp.dot(p2.astype(BF16), vn_ref[...], preferred_element_type=F32))
    o_ref[...] = o / l


def mla_sample(q, kc, vc, k, v, o_buf, dbatch, dseq, past, heads, row0):
    r0 = row0 // dseq
    return pl.pallas_call(
        _mla_sample_kernel,
        grid=(dbatch, heads),
        in_specs=[pl.BlockSpec((dseq, HEAD_SLOT), lambda b, h: (r0 + b, h)),
                  pl.BlockSpec((past, HEAD_SLOT), lambda b, h: (b, h)),
                  pl.BlockSpec((past, LANES), lambda b, h: (b, h)),
                  pl.BlockSpec((dseq, HEAD_SLOT), lambda b, h: (r0 + b, h)),
                  pl.BlockSpec((dseq, LANES), lambda b, h: (r0 + b, h)),
                  pl.BlockSpec(memory_space=pl.ANY)],
        out_specs=pl.BlockSpec((dseq, LANES), lambda b, h: (r0 + b, h)),
        out_shape=jax.ShapeDtypeStruct(o_buf.shape, F32),
        input_output_aliases={5: 0},
        compiler_params=_cp("parallel", "parallel"),
        name="mla_sample",
    )(q, kc, vc, k, v, o_buf)


def _band_prompt_kernel(q_ref, k0_ref, k1_ref, k2_ref, v0_ref, v1_ref, v2_ref, bias_ref, o_ref,
                        *, tq, nprev, scale):
    qi = pl.program_id(2)
    q = q_ref[...]
    ks = (k0_ref, k1_ref, k2_ref)[3 - nprev - 1:]
    vs = (v0_ref, v1_ref, v2_ref)[3 - nprev - 1:]
    ss = []
    for d, kr in enumerate(ks):
        s = _qk(q, kr[...]) * scale + bias_ref[0, :, d * tq:(d + 1) * tq]
        if d < nprev:
            s = jnp.where(qi - (nprev - d) >= 0, s, NEG_INF)
        ss.append(s)
    m = functools.reduce(jnp.maximum, [jnp.max(s, axis=-1, keepdims=True) for s in ss])
    ps = [jnp.exp(s - m) for s in ss]
    l = functools.reduce(lambda a, b: a + b, [jnp.sum(p, axis=-1, keepdims=True) for p in ps])
    o = functools.reduce(lambda a, b: a + b,
                         [jnp.dot(p.astype(BF16), vr[...], preferred_element_type=F32)
                          for p, vr in zip(ps, vs)])
    o_ref[...] = o / l


def band_prompt(qkv, bias, batch, seq, heads, tq, nprev, scale):
    t = qkv.shape[0]
    nq = seq // tq

    def kspec(back, col0):
        return pl.BlockSpec((tq, LANES),
                            lambda b, h, i: (b * nq + jnp.maximum(i - back, 0), col0 + h))

    return pl.pallas_call(
        functools.partial(_band_prompt_kernel, tq=tq, nprev=nprev, scale=scale),
        grid=(batch, heads, nq),
        in_specs=[pl.BlockSpec((tq, LANES), lambda b, h, i: (b * nq + i, h)),
                  kspec(2, heads), kspec(1, heads), kspec(0, heads),
                  kspec(2, 2 * heads), kspec(1, 2 * heads), kspec(0, 2 * heads),
                  pl.BlockSpec((1, tq, (nprev + 1) * tq), lambda b, h, i: (h, 0, 0))],
        out_specs=pl.BlockSpec((tq, LANES), lambda b, h, i: (b * nq + i, h)),
        out_shape=jax.ShapeDtypeStruct((t, heads * LANES), F32),
        compiler_params=_cp("parallel", "parallel", "arbitrary"),
        name="band_prompt",
    )(qkv, qkv, qkv, qkv, qkv, qkv, qkv, bias)


def _band_sample_kernel(q_ref, kc_ref, vc_ref, kn_ref, vn_ref, bias_ref, oin_ref, o_ref, *, nb, scale):
    del oin_ref
    q = q_ref[...]
    s1 = _qk(q, kc_ref[...].astype(BF16)) * scale + bias_ref[0, :, :nb]
    s2 = _qk(q, kn_ref[...]) * scale + bias_ref[0, :, nb:]
    m = jnp.maximum(jnp.max(s1, axis=-1, keepdims=True), jnp.max(s2, axis=-1, keepdims=True))
    p1 = jnp.exp(s1 - m)
    p2 = jnp.exp(s2 - m)
    l = jnp.sum(p1, axis=-1, keepdims=True) + jnp.sum(p2, axis=-1, keepdims=True)
    o = (jnp.dot(p1.astype(BF16), vc_ref[...].astype(BF16), preferred_element_type=F32)
         + jnp.dot(p2.astype(BF16), vn_ref[...], preferred_element_type=F32))
    o_ref[...] = o / l


def band_sample(qkv, kc, vc, bias, o_buf, dbatch, dseq, nb, heads, row0, scale):
    r0 = row0 // dseq
    return pl.pallas_call(
        functools.partial(_band_sample_kernel, nb=nb, scale=scale),
        grid=(dbatch, heads),
        in_specs=[pl.BlockSpec((dseq, LANES), lambda b, h: (r0 + b, h)),
                  pl.BlockSpec((nb, LANES), lambda b, h: (b, h)),
                  pl.BlockSpec((nb, LANES), lambda b, h: (b, h)),
                  pl.BlockSpec((dseq, LANES), lambda b, h: (r0 + b, heads + h)),
                  pl.BlockSpec((dseq, LANES), lambda b, h: (r0 + b, 2 * heads + h)),
                  pl.BlockSpec((1, dseq, nb + dseq), lambda b, h: (h, 0, 0)),
                  pl.BlockSpec(memory_space=pl.ANY)],
        out_specs=pl.BlockSpec((dseq, LANES), lambda b, h: (r0 + b, h)),
        out_shape=jax.ShapeDtypeStruct(o_buf.shape, F32),
        input_output_aliases={6: 0},
        compiler_params=_cp("parallel", "parallel"),
        name="band_sample",
    )(qkv, kc, vc, qkv, qkv, bias, o_buf)


def _band_bias_table(rel_bias, q_pos, k_pos):
    rel = jnp.clip(q_pos[:, None] - k_pos[None, :], -MAX_REL, MAX_REL) + MAX_REL
    qc = q_pos[:, None] // CHUNK
    kc = k_pos[None, :] // CHUNK
    mask = (kc <= qc) & (kc >= qc - BAND_PREV)
    return jnp.where(mask[None], rel_bias.astype(F32)[:, rel], NEG_INF)


def _merge_kernel(a_ref, b_ref, ga_ref, gb_ref, o_ref):
    w = a_ref.shape[1]
    o_ref[:, :w] = _rmsnorm_rows(a_ref[...], ga_ref[...]).astype(BF16)
    o_ref[:, w:] = _rmsnorm_rows(b_ref[...], gb_ref[...]).astype(BF16)


def merge_norm(oa, ob, ga, gb):
    m, w = oa.shape
    tm = _pick(m, (256, 128))
    row = lambda i: (i, 0)
    fix = lambda i: (0, 0)
    return pl.pallas_call(
        _merge_kernel,
        grid=(m // tm,),
        in_specs=[pl.BlockSpec((tm, w), row), pl.BlockSpec((tm, w), row),
                  pl.BlockSpec((1, w), fix), pl.BlockSpec((1, w), fix)],
        out_specs=pl.BlockSpec((tm, 2 * w), row),
        out_shape=jax.ShapeDtypeStruct((m, 2 * w), BF16),
        compiler_params=_cp("parallel"),
        name="merge_norm",
    )(oa, ob, ga.reshape(1, w), gb.reshape(1, w))


def _mem_kernel(q_ref, k_ref, v_ref, *rest, heads, dim, scale):
    o_ref = rest[-1]
    for h in range(heads):
        sl = slice(h * dim, (h + 1) * dim)
        s = _qk(q_ref[:, sl], k_ref[:, sl].astype(BF16)) * scale
        m = jnp.max(s, axis=-1, keepdims=True)
        p = jnp.exp(s - m)
        l = jnp.sum(p, axis=-1, keepdims=True)
        o = jnp.dot(p.astype(BF16), v_ref[:, sl].astype(BF16), preferred_element_type=F32)
        o_ref[:, sl] = (o / l).astype(BF16)


def mem_attend(q, mk, mv, kcol, vcol, o_buf, *, nbatch, rows_per_batch, row0, heads, dim, name):
    t, w = q.shape
    mtok = mk.shape[0] // nbatch
    tq = _pick(rows_per_batch, (512, 256, 128, 64))
    nq = rows_per_batch // tq
    r0 = row0 // tq
    in_specs = [pl.BlockSpec((tq, w), lambda b, i: (r0 + b * nq + i, 0)),
                pl.BlockSpec((mtok, w), lambda b, i: (b, kcol)),
                pl.BlockSpec((mtok, w), lambda b, i: (b, vcol))]
    args = [q, mk, mv]
    alias = {}
    if o_buf is not None:
        in_specs.append(pl.BlockSpec(memory_space=pl.ANY))
        args.append(o_buf)
        alias = {3: 0}
    return pl.pallas_call(
        functools.partial(_mem_kernel, heads=heads, dim=dim, scale=dim ** -0.5),
        grid=(nbatch, nq),
        in_specs=in_specs,
        out_specs=pl.BlockSpec((tq, w), lambda b, i: (r0 + b * nq + i, 0)),
        out_shape=jax.ShapeDtypeStruct((t, w), BF16),
        input_output_aliases=alias,
        compiler_params=_cp("parallel", "parallel"),
        name=name,
    )(*args)


def _ffn_up_kernel(x_ref, wa_ref, wg_ref, wdw_ref, bdw_ref, st_ref, *rest,
                   seg, nseg, blocks_per_seq, use_state):
    if use_state:
        u_ref, tail_ref = rest
        carry_ref = None
    else:
        u_ref, tail_ref, carry_ref = rest
    i, j = pl.program_id(0), pl.program_id(1)
    x = x_ref[...]
    a = jnp.dot(x, wa_ref[...], preferred_element_type=F32)
    g = jnp.dot(x, wg_ref[...], preferred_element_type=F32)
    tn = g.shape[1]
    w0, w1, w2 = wdw_ref[0:1, :], wdw_ref[1:2, :], wdw_ref[2:3, :]
    row = lax.broadcasted_iota(jnp.int32, (seg, tn), 0)
    if not use_state:
        @pl.when(i % blocks_per_seq == 0)
        def _():
            carry_ref[j] = jnp.zeros((8, tn), F32)
    for s in range(nseg):
        gs = g[s * seg:(s + 1) * seg]
        if use_state:
            p2, p1 = st_ref[s, 0:1, :], st_ref[s, 1:2, :]
        else:
            prev = carry_ref[j]
            p2, p1 = prev[6:7, :], prev[7:8, :]
        gm1 = jnp.where(row == 0, p1, pltpu.roll(gs, 1, 0))
        gm2 = jnp.where(row == 0, p2, jnp.where(row == 1, p1, pltpu.roll(gs, 2, 0)))
        gc = ((bdw_ref[...] + w0 * gm2) + w1 * gm1) + w2 * gs
        u_ref[s * seg:(s + 1) * seg, :] = (a[s * seg:(s + 1) * seg] * (gc * jax.nn.sigmoid(gc))).astype(BF16)
        tail = gs[seg - 8:seg]
        tail_ref[s] = tail
        if not use_state:
            carry_ref[j] = tail


def ffn_up(h, wa, wg, wdw, bdw, state, *, row0, rows, seg, blocks_per_seq, use_state, name):
    d = h.shape[1]
    ffp = wa.shape[1]
    tn = _pick(ffp, (512, 256, 128))
    if use_state:
        tm, nseg = rows, rows // seg
    else:
        tm, nseg = seg, 1
    r0 = row0 // tm
    ni, nj = rows // tm, ffp // tn
    scratch = [] if use_state else [pltpu.VMEM((nj, 8, tn), F32)]
    return pl.pallas_call(
        functools.partial(_ffn_up_kernel, seg=seg, nseg=nseg, blocks_per_seq=blocks_per_seq,
                          use_state=use_state),
        grid=(ni, nj),
        in_specs=[pl.BlockSpec((tm, d), lambda i, j: (r0 + i, 0)),
                  pl.BlockSpec((d, tn), lambda i, j: (0, j)),
                  pl.BlockSpec((d, tn), lambda i, j: (0, j)),
                  pl.BlockSpec((CONV_W, tn), lambda i, j: (0, j)),
                  pl.BlockSpec((1, tn), lambda i, j: (0, j)),
                  pl.BlockSpec((state.shape[0], CONV_W - 1, tn), lambda i, j: (0, 0, j))],
        out_specs=[pl.BlockSpec((tm, tn), lambda i, j: (i, j)),
                   pl.BlockSpec((nseg, 8, tn), lambda i, j: (i, 0, j))],
        out_shape=[jax.ShapeDtypeStruct((rows, ffp), BF16),
                   jax.ShapeDtypeStruct((ni * nseg, 8, ffp), F32)],
        scratch_shapes=scratch,
        compiler_params=_cp("arbitrary", "arbitrary"),
        name=name,
    )(h, wa, wg, wdw, bdw, state)


def _rope_tables(pos):
    half = 32
    inv = ROPE_THETA ** (-jnp.arange(half, dtype=F32) / half)
    ang = pos.astype(F32)[:, None] * inv[None, :]
    c, s, z = jnp.cos(ang), jnp.sin(ang), jnp.zeros_like(ang)
    return jnp.concatenate([c, z, c, z], axis=1), jnp.concatenate([-s, z, s, z], axis=1)


def _slot_cols(w, half):
    z = jnp.zeros(w.shape[:-1] + (LANES // 2 - half,), w.dtype)
    return jnp.concatenate([w[..., :half], z, w[..., half:], z], axis=-1)


def kernel(x_prompt, x_sample, mem_prompt, cache_mla_ckv, cache_mla_krope, cache_band_k, cache_band_v, cache_mem_k, cache_mem_v, state_conv, norm_mix, w_in, norm_cq, norm_ckv, w_uq, w_uk, w_uv, rel_bias, g_out_a, g_out_b, w_o, norm_mem, norm_memtok, w_mq, w_mkv, w_mo, norm_ffn, w_up, w_dw, b_dw, w_down, norm_final):
    batch, seq, d = x_prompt.shape
    dbatch, dseq, _ = x_sample.shape
    depth = norm_mix.shape[0]
    past = cache_mla_ckv.shape[2]
    nband = cache_band_k.shape[2]
    ql, kvl = norm_cq.shape[1], norm_ckv.shape[1]
    rope = cache_mla_krope.shape[3]
    half = rope // 2
    a_heads, a_nope = w_uk.shape[2], w_uk.shape[3]
    a_vdim = w_uv.shape[3]
    b_heads, b_dim = cache_band_k.shape[3], cache_band_k.shape[4]
    mtok, m_heads, m_dim = cache_mem_k.shape[2], cache_mem_k.shape[3], cache_mem_k.shape[4]
    mem_w = m_heads * m_dim
    ff = b_dw.shape[1]
    assert a_nope == LANES and a_vdim == LANES and b_dim == LANES and rope == LANES // 2
    assert seq % CHUNK == 0 and dseq == CHUNK and past % CHUNK == 0

    tp, ts = batch * seq, dbatch * dseq
    t = tp + ts
    band_keep = min(BAND_PREV * CHUNK, seq)
    mla_scale = (a_nope + rope) ** -0.5
    b_scale = b_dim ** -0.5

    pos = jnp.concatenate([jnp.tile(jnp.arange(seq, dtype=jnp.int32), batch),
                           jnp.tile(past + jnp.arange(dseq, dtype=jnp.int32), dbatch)])
    cos_t, sin_t = _rope_tables(pos)
    band_tq = _pick(seq, (256, 128, 64))
    nprev = (BAND_PREV * CHUNK) // band_tq
    assert nprev * band_tq == BAND_PREV * CHUNK and nprev <= 2
    bq = nprev * band_tq + jnp.arange(band_tq, dtype=jnp.int32)
    bk = jnp.arange((nprev + 1) * band_tq, dtype=jnp.int32)
    sq = past + jnp.arange(dseq, dtype=jnp.int32)
    sk = past - nband + jnp.arange(nband + dseq, dtype=jnp.int32)
    perm = _slot_cols(jnp.eye(rope, dtype=BF16), half)
    ffp = _round_up(ff, 1024)
    tk_down = _pick(ffp, (2816, 2048, 1024, 512, 256))
    ffn_tm = _pick(math.gcd(seq, 1024), (1024, 512, 256, 128))

    x = jnp.concatenate([x_prompt.reshape(tp, d), x_sample.reshape(ts, d)], axis=0)
    outs = {k: [] for k in ("p_ckv", "p_kr", "p_bk", "p_bv", "p_mk", "p_mv", "p_conv",
                            "s_ckv", "s_kr", "s_bk", "s_bv", "s_conv")}

    def unslot(kr):
        return jnp.concatenate([kr[:, :half], kr[:, LANES // 2:LANES // 2 + half]], axis=1)

    for l in range(depth):
        wi = w_in[l]
        w_lat = jnp.concatenate([wi[:, :ql + kvl], _slot_cols(wi[:, ql + kvl:ql + kvl + rope], half)],
                                axis=1).astype(BF16)
        w_qkvb = wi[:, ql + kvl + rope:].astype(BF16)
        wq = w_uq[l].reshape(ql, a_heads, a_nope + rope)
        wq = jnp.concatenate([wq[..., :a_nope], _slot_cols(wq[..., a_nope:], half)], axis=-1)
        wq = wq.reshape(ql, a_heads * HEAD_SLOT).astype(BF16)
        wuk = w_uk[l].reshape(kvl, a_heads * a_nope).astype(BF16)
        wuv = w_uv[l].reshape(kvl, a_heads * a_vdim).astype(BF16)
        wo = w_o[l].astype(BF16)
        wmq, wmkv, wmo = w_mq[l].astype(BF16), w_mkv[l].astype(BF16), w_mo[l].astype(BF16)
        padc = ((0, 0), (0, ffp - ff))
        wa = jnp.pad(w_up[l][:, :ff], padc).astype(BF16)
        wg = jnp.pad(w_up[l][:, ff:], padc).astype(BF16)
        wdw = jnp.pad(w_dw[l], padc)
        bdw = jnp.pad(b_dw[l].reshape(1, ff), padc)
        wdn = jnp.pad(w_down[l], ((0, ffp - ff), (0, 0))).astype(BF16)
        st = jnp.pad(state_conv[l], ((0, 0), (0, 0), (0, ffp - ff)))

        h = rmsnorm(x, norm_mix[l], BF16, name="norm_mix")
        cq, ckv, ckv_b, kr, kr_b = latent_project(h, w_lat, norm_cq[l], norm_ckv[l], cos_t, sin_t)
        qkvb = matmul(h, w_qkvb, BF16, name="qkv_band")
        sel = jnp.concatenate([h[b * seq + seq - band_keep:(b + 1) * seq] for b in range(batch)]
                              + [h[tp:]], axis=0)
        kv_keep = matmul(sel, w_qkvb[:, b_heads * b_dim:], F32, name="kv_band_keep")
        q = q_project(cq, wq, cos_t, sin_t, mla_scale)
        k, v = kv_decompress(ckv_b, kr_b, wuk, wuv, perm, permute=False)
        kc, vc = kv_decompress(cache_mla_ckv[l].reshape(dbatch * past, kvl),
                               cache_mla_krope[l].reshape(dbatch * past, rope),
                               wuk, wuv, perm, permute=True)
        oa = mla_prompt(q, k, v, batch, seq, a_heads)
        oa = mla_sample(q, kc, vc, k, v, oa, dbatch, dseq, past, a_heads, tp)
        bias_p = _band_bias_table(rel_bias[l], bq, bk)
        bias_s = _band_bias_table(rel_bias[l], sq, sk)
        ob = band_prompt(qkvb, bias_p, batch, seq, b_heads, band_tq, nprev, b_scale)
        ob = band_sample(qkvb, cache_band_k[l].reshape(dbatch * nband, b_heads * b_dim),
                         cache_band_v[l].reshape(dbatch * nband, b_heads * b_dim),
                         bias_s, ob, dbatch, dseq, nband, b_heads, tp, b_scale)
        x = matmul_residual(merge_norm(oa, ob, g_out_a[l], g_out_b[l]), wo, x, name="out_proj")

        memn = rmsnorm(mem_prompt.reshape(batch * mtok, d), norm_memtok[l], BF16, name="norm_memtok")
        mkv = matmul(memn, wmkv, F32, name="mem_kv")
        h = rmsnorm(x, norm_mem[l], BF16, name="norm_mem")
        qm = matmul(h, wmq, BF16, name="mem_q")
        om = mem_attend(qm, mkv, mkv, 0, 1, None, nbatch=batch, rows_per_batch=seq, row0=0,
                        heads=m_heads, dim=m_dim, name="mem_prompt")
        cmk = cache_mem_k[l].reshape(dbatch * mtok, mem_w)
        cmv = cache_mem_v[l].reshape(dbatch * mtok, mem_w)
        om = mem_attend(qm, cmk, cmv, 0, 0, om, nbatch=dbatch, rows_per_batch=dseq, row0=tp,
                        heads=m_heads, dim=m_dim, name="mem_sample")
        x = matmul_residual(om, wmo, x, name="mem_out")

        h = rmsnorm(x, norm_ffn[l], BF16, name="norm_ffn")
        u_p, tail_p = ffn_up(h, wa, wg, wdw, bdw, st, row0=0, rows=tp, seg=ffn_tm,
                             blocks_per_seq=seq // ffn_tm, use_state=False, name="ffn_up_prompt")
        u_s, tail_s = ffn_up(h, wa, wg, wdw, bdw, st, row0=tp, rows=ts, seg=dseq,
                             blocks_per_seq=1, use_state=True, name="ffn_up_sample")
        u = jnp.concatenate([u_p, u_s], axis=0)
        x = matmul_residual_ktiled(u, wdn, x, tk=tk_down, name="ffn_down")

        outs["p_ckv"].append(ckv[:tp].reshape(batch, seq, kvl))
        outs["s_ckv"].append(ckv[tp:].reshape(dbatch, dseq, kvl))
        kr64 = unslot(kr)
        outs["p_kr"].append(kr64[:tp].reshape(batch, seq, rope))
        outs["s_kr"].append(kr64[tp:].reshape(dbatch, dseq, rope))
        hw = b_heads * b_dim
        nkp = batch * band_keep
        outs["p_bk"].append(kv_keep[:nkp, :hw].reshape(batch, band_keep, b_heads, b_dim))
        outs["p_bv"].append(kv_keep[:nkp, hw:].reshape(batch, band_keep, b_heads, b_dim))
        outs["s_bk"].append(kv_keep[nkp:, :hw].reshape(dbatch, dseq, b_heads, b_dim))
        outs["s_bv"].append(kv_keep[nkp:, hw:].reshape(dbatch, dseq, b_heads, b_dim))
        outs["p_mk"].append(mkv[:, :mem_w].reshape(batch, mtok, m_heads, m_dim))
        outs["p_mv"].append(mkv[:, mem_w:].reshape(batch, mtok, m_heads, m_dim))
        nblk = seq // ffn_tm
        tail_p = tail_p.reshape(batch, nblk, 8, ffp)[:, nblk - 1, 8 - (CONV_W - 1):, :ff]
        outs["p_conv"].append(tail_p)
        outs["s_conv"].append(tail_s[:, 8 - (CONV_W - 1):, :ff])

    y_prompt = rmsnorm(x, norm_final, F32, row0=0, rows=tp, name="norm_final_p").reshape(batch, seq, d)
    y_sample = rmsnorm(x, norm_final, F32, row0=tp, rows=ts, name="norm_final_s").reshape(dbatch, dseq, d)
    st = {k_: jnp.stack(v_) for k_, v_ in outs.items()}
    return (y_prompt, y_sample, st["p_ckv"], st["p_kr"], st["p_bk"], st["p_bv"], st["p_mk"],
            st["p_mv"], st["p_conv"], st["s_ckv"], st["s_kr"], st["s_bk"], st["s_bv"], st["s_conv"])
```

```python
import functools
import math

import jax
import jax.numpy as jnp
import numpy as np
from jax import lax
from jax.experimental import pallas as pl
from jax.experimental.pallas import tpu as pltpu

CHUNK = 64
BAND_PREV = 8
MAX_REL = 128
CONV_W = 3
ROPE_THETA = 10000.0
EPS = 1e-6
NEG_INF = -1e30
LOG2E = math.log2(math.e)

LANES = 128
HEAD_SLOT = 256
VMEM_LIMIT = 56 * 1024 * 1024

F32 = jnp.float32
BF16 = jnp.bfloat16


def _cp(*sem):
    return pltpu.CompilerParams(dimension_semantics=sem, vmem_limit_bytes=VMEM_LIMIT)


def _pick(n, prefs):
    for p in prefs:
        if n % p == 0:
            return p
    return n


def _round_up(n, m):
    return (n + m - 1) // m * m


def _rmsnorm_rows(x, g):
    ms = jnp.mean(x * x, axis=-1, keepdims=True)
    return (x * lax.rsqrt(ms + EPS)) * g


def _rmsnorm_kernel(x_ref, g_ref, o_ref):
    o_ref[...] = _rmsnorm_rows(x_ref[...].astype(F32), g_ref[...]).astype(o_ref.dtype)


def rmsnorm(x, g, out_dtype, *, row0=0, rows=None, name="rmsnorm"):
    m, d = x.shape
    rows = m if rows is None else rows
    tm = _pick(math.gcd(rows, row0) if row0 else rows, (256, 128, 64, 32, 16, 8))
    off = row0 // tm
    return pl.pallas_call(
        _rmsnorm_kernel,
        grid=(rows // tm,),
        in_specs=[pl.BlockSpec((tm, d), lambda i: (i + off, 0)),
                  pl.BlockSpec((1, d), lambda i: (0, 0))],
        out_specs=pl.BlockSpec((tm, d), lambda i: (i, 0)),
        out_shape=jax.ShapeDtypeStruct((rows, d), out_dtype),
        compiler_params=_cp("parallel"),
        name=name,
    )(x, g.reshape(1, d).astype(F32))


def _mm_kernel(x_ref, w_ref, o_ref):
    o_ref[...] = jnp.dot(x_ref[...].astype(BF16), w_ref[...],
                         preferred_element_type=F32).astype(o_ref.dtype)


def matmul(x, w, out_dtype, *, tm_prefs=(1024, 512, 256, 128, 64, 32, 16, 8),
           tn_prefs=(512, 256, 128), name="matmul"):
    m, k = x.shape
    n = w.shape[1]
    tm, tn = _pick(m, tm_prefs), _pick(n, tn_prefs)
    return pl.pallas_call(
        _mm_kernel,
        grid=(m // tm, n // tn),
        in_specs=[pl.BlockSpec((tm, k), lambda i, j: (i, 0)),
                  pl.BlockSpec((k, tn), lambda i, j: (0, j))],
        out_specs=pl.BlockSpec((tm, tn), lambda i, j: (i, j)),
        out_shape=jax.ShapeDtypeStruct((m, n), out_dtype),
        compiler_params=_cp("parallel", "parallel"),
        name=name,
    )(x, w)


def _mm_res_kernel(x_ref, w_ref, r_ref, o_ref):
    o_ref[...] = r_ref[...] + jnp.dot(x_ref[...], w_ref[...], preferred_element_type=F32)


def matmul_residual(x, w, res, *, name="matmul_res"):
    m, k = x.shape
    n = w.shape[1]
    tm, tn = _pick(m, (1024, 512, 256, 128)), _pick(n, (512, 256, 128))
    return pl.pallas_call(
        _mm_res_kernel,
        grid=(m // tm, n // tn),
        in_specs=[pl.BlockSpec((tm, k), lambda i, j: (i, 0)),
                  pl.BlockSpec((k, tn), lambda i, j: (0, j)),
                  pl.BlockSpec((tm, tn), lambda i, j: (i, j))],
        out_specs=pl.BlockSpec((tm, tn), lambda i, j: (i, j)),
        out_shape=jax.ShapeDtypeStruct((m, n), F32),
        input_output_aliases={2: 0},
        compiler_params=_cp("parallel", "parallel"),
        name=name,
    )(x, w, res)


def _mm_res_ktiled_kernel(x_ref, w_ref, r_ref, o_ref, acc_ref):
    kk = pl.program_id(2)

    @pl.when(kk == 0)
    def _():
        acc_ref[...] = jnp.zeros_like(acc_ref)

    acc_ref[...] += jnp.dot(x_ref[...], w_ref[...], preferred_element_type=F32)

    @pl.when(kk == pl.num_programs(2) - 1)
    def _():
        o_ref[...] = r_ref[...] + acc_ref[...]


def matmul_residual_ktiled(x, w, res, *, tk, name="matmul_res_k"):
    m, k = x.shape
    n = w.shape[1]
    tm, tn = _pick(m, (1024, 512, 256, 128)), _pick(n, (512, 256, 128))
    return pl.pallas_call(
        _mm_res_ktiled_kernel,
        grid=(m // tm, n // tn, k // tk),
        in_specs=[pl.BlockSpec((tm, tk), lambda i, j, kk: (i, kk)),
                  pl.BlockSpec((tk, tn), lambda i, j, kk: (kk, j)),
                  pl.BlockSpec((tm, tn), lambda i, j, kk: (i, j))],
        out_specs=pl.BlockSpec((tm, tn), lambda i, j, kk: (i, j)),
        out_shape=jax.ShapeDtypeStruct((m, n), F32),
        scratch_shapes=[pltpu.VMEM((tm, tn), F32)],
        input_output_aliases={2: 0},
        compiler_params=_cp("parallel", "parallel", "arbitrary"),
        name=name,
    )(x, w, res)


def _rope_slot(r, cos, sin):
    return r * cos + pltpu.roll(r, 64, 1) * sin


def _lat_kernel(h_ref, w_ref, gq_ref, gkv_ref, cos_ref, sin_ref,
                cq_ref, ckv_ref, ckvb_ref, kr_ref, krb_ref, *, ql, kvl):
    acc = jnp.dot(h_ref[...], w_ref[...], preferred_element_type=F32)
    cq_ref[...] = _rmsnorm_rows(acc[:, :ql], gq_ref[...]).astype(BF16)
    ckv = _rmsnorm_rows(acc[:, ql:ql + kvl], gkv_ref[...])
    ckv_ref[...] = ckv
    ckvb_ref[...] = ckv.astype(BF16)
    kr = _rope_slot(acc[:, ql + kvl:], cos_ref[...], sin_ref[...])
    kr_ref[...] = kr
    krb_ref[...] = kr.astype(BF16)


def latent_project(h, w_lat, g_cq, g_ckv, cos_t, sin_t):
    m, d = h.shape
    ql, kvl = g_cq.shape[0], g_ckv.shape[0]
    n = w_lat.shape[1]
    tm = _pick(m, (512, 256, 128))
    row = lambda i: (i, 0)
    fix = lambda i: (0, 0)
    return pl.pallas_call(
        functools.partial(_lat_kernel, ql=ql, kvl=kvl),
        grid=(m // tm,),
        in_specs=[pl.BlockSpec((tm, d), row), pl.BlockSpec((d, n), fix),
                  pl.BlockSpec((1, ql), fix), pl.BlockSpec((1, kvl), fix),
                  pl.BlockSpec((tm, LANES), row), pl.BlockSpec((tm, LANES), row)],
        out_specs=[pl.BlockSpec((tm, ql), row), pl.BlockSpec((tm, kvl), row),
                   pl.BlockSpec((tm, kvl), row), pl.BlockSpec((tm, LANES), row),
                   pl.BlockSpec((tm, LANES), row)],
        out_shape=[jax.ShapeDtypeStruct((m, ql), BF16), jax.ShapeDtypeStruct((m, kvl), F32),
                   jax.ShapeDtypeStruct((m, kvl), BF16), jax.ShapeDtypeStruct((m, LANES), F32),
                   jax.ShapeDtypeStruct((m, LANES), BF16)],
        compiler_params=_cp("parallel"),
        name="latent_project",
    )(h, w_lat, g_cq.reshape(1, ql), g_ckv.reshape(1, kvl), cos_t, sin_t)


def _q_kernel(c_ref, w_ref, cos_ref, sin_ref, o_ref, *, heads, scale):
    acc = jnp.dot(c_ref[...], w_ref[...], preferred_element_type=F32)
    cos, sin = cos_ref[...], sin_ref[...]
    for h in range(heads):
        lo = h * HEAD_SLOT
        o_ref[:, lo:lo + LANES] = (acc[:, lo:lo + LANES] * scale).astype(BF16)
        r = _rope_slot(acc[:, lo + LANES:lo + HEAD_SLOT], cos, sin)
        o_ref[:, lo + LANES:lo + HEAD_SLOT] = (r * scale).astype(BF16)


def q_project(cq, w_uq, cos_t, sin_t, scale, *, row0, rows):
    ql = cq.shape[1]
    n = w_uq.shape[1]
    tm = _pick(math.gcd(rows, row0), (512, 256, 128, 64))
    tn = _pick(n, (1024, 512, 256))
    r0 = row0 // tm
    return pl.pallas_call(
        functools.partial(_q_kernel, heads=tn // HEAD_SLOT, scale=scale),
        grid=(rows // tm, n // tn),
        in_specs=[pl.BlockSpec((tm, ql), lambda i, j: (r0 + i, 0)),
                  pl.BlockSpec((ql, tn), lambda i, j: (0, j)),
                  pl.BlockSpec((tm, LANES), lambda i, j: (r0 + i, 0)),
                  pl.BlockSpec((tm, LANES), lambda i, j: (r0 + i, 0))],
        out_specs=pl.BlockSpec((tm, tn), lambda i, j: (i, j)),
        out_shape=jax.ShapeDtypeStruct((rows, n), BF16),
        compiler_params=_cp("parallel", "parallel"),
        name="q_project",
    )(cq, w_uq, cos_t, sin_t)


def _qt_kernel(c_ref, wt_ref, cos_ref, sin_ref, o_ref, *, heads, scale):
    acc = lax.dot_general(wt_ref[...], c_ref[...], (((1,), (1,)), ((), ())),
                          preferred_element_type=F32)
    cos, sin = cos_ref[...], sin_ref[...]
    for h in range(heads):
        lo = h * HEAD_SLOT
        o_ref[0, lo:lo + LANES, :] = (acc[lo:lo + LANES] * scale).astype(BF16)
        r = acc[lo + LANES:lo + HEAD_SLOT]
        r = r * cos + pltpu.roll(r, 64, 0) * sin
        o_ref[0, lo + LANES:lo + HEAD_SLOT, :] = (r * scale).astype(BF16)


def q_project_t(cq, w_uq_t, cos_tt, sin_tt, scale, *, rows, tq):
    ql = cq.shape[1]
    n = w_uq_t.shape[0]
    tn = _pick(n, (1024, 512, 256))
    return pl.pallas_call(
        functools.partial(_qt_kernel, heads=tn // HEAD_SLOT, scale=scale),
        grid=(rows // tq, n // tn),
        in_specs=[pl.BlockSpec((tq, ql), lambda i, j: (i, 0)),
                  pl.BlockSpec((tn, ql), lambda i, j: (j, 0)),
                  pl.BlockSpec((LANES, tq), lambda i, j: (0, i)),
                  pl.BlockSpec((LANES, tq), lambda i, j: (0, i))],
        out_specs=pl.BlockSpec((1, tn, tq), lambda i, j: (i, j, 0)),
        out_shape=jax.ShapeDtypeStruct((rows // tq, n, tq), BF16),
        compiler_params=_cp("parallel", "parallel"),
        name="q_project_t",
    )(cq, w_uq_t, cos_tt, sin_tt)


def _kv_kernel(c_ref, kr_ref, perm_ref, wk_ref, wv_ref, *rest, heads, permute, with_vt):
    if with_vt:
        wvt_ref, k_ref, v_ref, vt_ref = rest
    else:
        k_ref, v_ref = rest
    c = c_ref[...].astype(BF16)
    kn = jnp.dot(c, wk_ref[...], preferred_element_type=F32)
    v_ref[...] = jnp.dot(c, wv_ref[...], preferred_element_type=F32).astype(BF16)
    if with_vt:
        vt_ref[0] = lax.dot_general(wvt_ref[...], c, (((1,), (1,)), ((), ())),
                                    preferred_element_type=F32).astype(BF16)
    kr = kr_ref[...].astype(BF16)
    if permute:
        kr = jnp.dot(kr, perm_ref[...], preferred_element_type=F32).astype(BF16)
    for h in range(heads):
        lo = h * HEAD_SLOT
        k_ref[:, lo:lo + LANES] = kn[:, h * LANES:(h + 1) * LANES].astype(BF16)
        k_ref[:, lo + LANES:lo + HEAD_SLOT] = kr


def kv_decompress(ckv, kr, w_uk, w_uv, perm, *, permute, tm, w_uv_t=None):
    m, kvl = ckv.shape
    heads = w_uk.shape[1] // LANES
    with_vt = w_uv_t is not None
    row = lambda i: (i, 0)
    fix = lambda i: (0, 0)
    in_specs = [pl.BlockSpec((tm, kvl), row), pl.BlockSpec((tm, kr.shape[1]), row),
                pl.BlockSpec(perm.shape, fix),
                pl.BlockSpec(w_uk.shape, fix), pl.BlockSpec(w_uv.shape, fix)]
    out_specs = [pl.BlockSpec((tm, heads * HEAD_SLOT), row), pl.BlockSpec((tm, heads * LANES), row)]
    out_shape = [jax.ShapeDtypeStruct((m, heads * HEAD_SLOT), BF16),
                 jax.ShapeDtypeStruct((m, heads * LANES), BF16)]
    args = [ckv, kr, perm, w_uk, w_uv]
    if with_vt:
        in_specs.append(pl.BlockSpec(w_uv_t.shape, fix))
        out_specs.append(pl.BlockSpec((1, heads * LANES, tm), lambda i: (i, 0, 0)))
        out_shape.append(jax.ShapeDtypeStruct((pl.cdiv(m, tm), heads * LANES, tm), BF16))
        args.append(w_uv_t)
    return pl.pallas_call(
        functools.partial(_kv_kernel, heads=heads, permute=permute, with_vt=with_vt),
        grid=(pl.cdiv(m, tm),),
        in_specs=in_specs,
        out_specs=out_specs,
        out_shape=out_shape,
        compiler_params=_cp("parallel"),
        name="kv_decompress",
    )(*args)


def _qk(q, k):
    return lax.dot_general(q, k, (((1,), (1,)), ((), ())), preferred_element_type=F32)


def _mla_prompt_kernel(qt_ref, k_ref, vt_ref, o_ref, m_ref, l_ref, acc_ref, *, t, nh):
    qi = pl.program_id(2)
    m_ref[...] = jnp.full_like(m_ref, NEG_INF)
    l_ref[...] = jnp.zeros_like(l_ref)
    acc_ref[...] = jnp.zeros_like(acc_ref)

    def step(ki, mask):
        start = pl.multiple_of(ki * t, t)
        for h in range(nh):
            k = k_ref[pl.ds(start, t), h * HEAD_SLOT:(h + 1) * HEAD_SLOT]
            s = jnp.dot(k, qt_ref[0, h * HEAD_SLOT:(h + 1) * HEAD_SLOT, :],
                        preferred_element_type=F32)
            if mask is not None:
                s = jnp.where(mask, s, NEG_INF)
            m_old = m_ref[h]
            m_new = jnp.maximum(m_old, jnp.max(s, axis=0, keepdims=True))
            alpha = jnp.exp2(m_old - m_new)
            p = jnp.exp2(s - m_new)
            l_ref[h] = alpha * l_ref[h] + jnp.sum(p, axis=0, keepdims=True)
            acc_ref[h] = alpha * acc_ref[h] + jnp.dot(
                vt_ref[ki, h * LANES:(h + 1) * LANES, :], p.astype(BF16),
                preferred_element_type=F32)
            m_ref[h] = m_new

    def body(ki, carry):
        step(ki, None)
        return carry

    lax.fori_loop(0, qi, body, 0)
    kc = lax.broadcasted_iota(jnp.int32, (t, t), 0) // CHUNK
    qc = lax.broadcasted_iota(jnp.int32, (t, t), 1) // CHUNK
    step(qi, kc <= qc)
    for h in range(nh):
        o_ref[:, h * LANES:(h + 1) * LANES] = (acc_ref[h] / l_ref[h]).T


def mla_prompt(qt, k, vt, batch, seq, heads, t, nh):
    rows = k.shape[0]
    nq = seq // t
    return pl.pallas_call(
        functools.partial(_mla_prompt_kernel, t=t, nh=nh),
        grid=(batch, heads // nh, nq),
        in_specs=[pl.BlockSpec((1, nh * HEAD_SLOT, t), lambda b, g, i: (b * nq + i, g, 0)),
                  pl.BlockSpec((seq, nh * HEAD_SLOT), lambda b, g, i: (b, g)),
                  pl.BlockSpec((nq, nh * LANES, t), lambda b, g, i: (b, g, 0))],
        out_specs=pl.BlockSpec((t, nh * LANES), lambda b, g, i: (b * nq + i, g)),
        out_shape=jax.ShapeDtypeStruct((rows, heads * LANES), F32),
        scratch_shapes=[pltpu.VMEM((nh, 1, t), F32), pltpu.VMEM((nh, 1, t), F32),
                        pltpu.VMEM((nh, LANES, t), F32)],
        compiler_params=_cp("parallel", "parallel", "arbitrary"),
        name="mla_prompt",
    )(qt, k, vt)


def _mla_sample_kernel(q_ref, kc_ref, vc_ref, kn_ref, vn_ref, oin_ref, o_ref):
    del oin_ref
    q = q_ref[...]
    s1 = _qk(q, kc_ref[...])
    s2 = _qk(q, kn_ref[...])
    m = jnp.maximum(jnp.max(s1, axis=-1, keepdims=True), jnp.max(s2, axis=-1, keepdims=True))
    p1 = jnp.exp2(s1 - m)
    p2 = jnp.exp2(s2 - m)
    l = jnp.sum(p1, axis=-1, keepdims=True) + jnp.sum(p2, axis=-1, keepdims=True)
    o = (jnp.dot(p1.astype(BF16), vc_ref[...], preferred_element_type=F32)
         + jnp.dot(p2.astype(BF16), vn_ref[...], preferred_element_type=F32))
    o_ref[...] = o / l


def mla_sample(q, kc, vc, k, v, o_buf, dbatch, dseq, past, heads, row0):
    r0 = row0 // dseq
    return pl.pallas_call(
        _mla_sample_kernel,
        grid=(dbatch, heads),
        in_specs=[pl.BlockSpec((dseq, HEAD_SLOT), lambda b, h: (b, h)),
                  pl.BlockSpec((past, HEAD_SLOT), lambda b, h: (b, h)),
                  pl.BlockSpec((past, LANES), lambda b, h: (b, h)),
                  pl.BlockSpec((dseq, HEAD_SLOT), lambda b, h: (r0 + b, h)),
                  pl.BlockSpec((dseq, LANES), lambda b, h: (r0 + b, h)),
                  pl.BlockSpec(memory_space=pl.ANY)],
        out_specs=pl.BlockSpec((dseq, LANES), lambda b, h: (r0 + b, h)),
        out_shape=jax.ShapeDtypeStruct(o_buf.shape, F32),
        input_output_aliases={5: 0},
        compiler_params=_cp("parallel", "parallel"),
        name="mla_sample",
    )(q, kc, vc, k, v, o_buf)


def _band_prompt_kernel(q_ref, k0_ref, k1_ref, k2_ref, v0_ref, v1_ref, v2_ref, bias_ref, o_ref,
                        *, tq, nprev, scale, nh):
    qi = pl.program_id(2)
    ks = (k0_ref, k1_ref, k2_ref)[3 - nprev - 1:]
    vs = (v0_ref, v1_ref, v2_ref)[3 - nprev - 1:]
    for h in range(nh):
        hs = slice(h * LANES, (h + 1) * LANES)
        q = q_ref[:, hs]
        ss = []
        for d, kr in enumerate(ks):
            s = _qk(q, kr[:, hs]) * scale + bias_ref[h, :, d * tq:(d + 1) * tq]
            if d < nprev:
                s = jnp.where(qi - (nprev - d) >= 0, s, NEG_INF)
            ss.append(s)
        m = functools.reduce(jnp.maximum, [jnp.max(s, axis=-1, keepdims=True) for s in ss])
        ps = [jnp.exp(s - m) for s in ss]
        l = functools.reduce(lambda a, b: a + b, [jnp.sum(p, axis=-1, keepdims=True) for p in ps])
        o = functools.reduce(lambda a, b: a + b,
                             [jnp.dot(p.astype(BF16), vr[:, hs], preferred_element_type=F32)
                              for p, vr in zip(ps, vs)])
        o_ref[:, hs] = o / l


def band_prompt(qkv, bias, batch, seq, heads, tq, nprev, scale):
    t = qkv.shape[0]
    nq = seq // tq
    nh = _pick(heads, (4, 2, 1))
    ng = heads // nh

    def kspec(back, col0):
        return pl.BlockSpec((tq, nh * LANES),
                            lambda b, g, i: (b * nq + jnp.maximum(i - back, 0), col0 + g))

    return pl.pallas_call(
        functools.partial(_band_prompt_kernel, tq=tq, nprev=nprev, scale=scale, nh=nh),
        grid=(batch, ng, nq),
        in_specs=[pl.BlockSpec((tq, nh * LANES), lambda b, g, i: (b * nq + i, g)),
                  kspec(2, ng), kspec(1, ng), kspec(0, ng),
                  kspec(2, 2 * ng), kspec(1, 2 * ng), kspec(0, 2 * ng),
                  pl.BlockSpec((nh, tq, (nprev + 1) * tq), lambda b, g, i: (g, 0, 0))],
        out_specs=pl.BlockSpec((tq, nh * LANES), lambda b, g, i: (b * nq + i, g)),
        out_shape=jax.ShapeDtypeStruct((t, heads * LANES), F32),
        compiler_params=_cp("parallel", "parallel", "arbitrary"),
        name="band_prompt",
    )(qkv, qkv, qkv, qkv, qkv, qkv, qkv, bias)


def _band_sample_kernel(q_ref, kc_ref, vc_ref, kn_ref, vn_ref, bias_ref, oin_ref, o_ref, *, nb, scale):
    del oin_ref
    q = q_ref[...]
    s1 = _qk(q, kc_ref[...].astype(BF16)) * scale + bias_ref[0, :, :nb]
    s2 = _qk(q, kn_ref[...]) * scale + bias_ref[0, :, nb:]
    m = jnp.maximum(jnp.max(s1, axis=-1, keepdims=True), jnp.max(s2, axis=-1, keepdims=True))
    p1 = jnp.exp(s1 - m)
    p2 = jnp.exp(s2 - m)
    l = jnp.sum(p1, axis=-1, keepdims=True) + jnp.sum(p2, axis=-1, keepdims=True)
    o = (jnp.dot(p1.astype(BF16), vc_ref[...].astype(BF16), preferred_element_type=F32)
         + jnp.dot(p2.astype(BF16), vn_ref[...], preferred_element_type=F32))
    o_ref[...] = o / l


def band_sample(qkv, kc, vc, bias, o_buf, dbatch, dseq, nb, heads, row0, scale):
    r0 = row0 // dseq
    return pl.pallas_call(
        functools.partial(_band_sample_kernel, nb=nb, scale=scale),
        grid=(dbatch, heads),
        in_specs=[pl.BlockSpec((dseq, LANES), lambda b, h: (r0 + b, h)),
                  pl.BlockSpec((nb, LANES), lambda b, h: (b, h)),
                  pl.BlockSpec((nb, LANES), lambda b, h: (b, h)),
                  pl.BlockSpec((dseq, LANES), lambda b, h: (r0 + b, heads + h)),
                  pl.BlockSpec((dseq, LANES), lambda b, h: (r0 + b, 2 * heads + h)),
                  pl.BlockSpec((1, dseq, nb + dseq), lambda b, h: (h, 0, 0)),
                  pl.BlockSpec(memory_space=pl.ANY)],
        out_specs=pl.BlockSpec((dseq, LANES), lambda b, h: (r0 + b, h)),
        out_shape=jax.ShapeDtypeStruct(o_buf.shape, F32),
        input_output_aliases={6: 0},
        compiler_params=_cp("parallel", "parallel"),
        name="band_sample",
    )(qkv, kc, vc, qkv, qkv, bias, o_buf)


def _band_bias_table(rel_bias, nq, nk, q0):
    span = nq + nk - 1
    k = np.concatenate([np.arange(0, nk), np.arange(-(nq - 1), 0)])
    idx = np.clip(q0 - k, -MAX_REL, MAX_REL) + MAX_REL
    u = rel_bias.astype(F32)[:, idx]
    tab = jnp.tile(u, (1, nq))[:, :nq * (span - 1)].reshape(-1, nq, span - 1)[:, :, :nk]
    qc = (q0 + np.arange(nq))[:, None] // CHUNK
    kc = np.arange(nk)[None, :] // CHUNK
    mask = (kc <= qc) & (kc >= qc - BAND_PREV)
    return jnp.where(mask[None], tab, NEG_INF)


def _merge_kernel(a_ref, b_ref, ga_ref, gb_ref, o_ref):
    w = a_ref.shape[1]
    o_ref[:, :w] = _rmsnorm_rows(a_ref[...], ga_ref[...]).astype(BF16)
    o_ref[:, w:] = _rmsnorm_rows(b_ref[...], gb_ref[...]).astype(BF16)


def merge_norm(oa, ob, ga, gb):
    m, w = oa.shape
    tm = _pick(m, (256, 128))
    row = lambda i: (i, 0)
    fix = lambda i: (0, 0)
    return pl.pallas_call(
        _merge_kernel,
        grid=(m // tm,),
        in_specs=[pl.BlockSpec((tm, w), row), pl.BlockSpec((tm, w), row),
                  pl.BlockSpec((1, w), fix), pl.BlockSpec((1, w), fix)],
        out_specs=pl.BlockSpec((tm, 2 * w), row),
        out_shape=jax.ShapeDtypeStruct((m, 2 * w), BF16),
        compiler_params=_cp("parallel"),
        name="merge_norm",
    )(oa, ob, ga.reshape(1, w), gb.reshape(1, w))


def _mem_kernel(q_ref, k_ref, v_ref, *rest, heads, dim, scale):
    o_ref = rest[-1]
    for h in range(heads):
        sl = slice(h * dim, (h + 1) * dim)
        s = _qk(q_ref[:, sl], k_ref[:, sl].astype(BF16)) * scale
        m = jnp.max(s, axis=-1, keepdims=True)
        p = jnp.exp(s - m)
        l = jnp.sum(p, axis=-1, keepdims=True)
        o = jnp.dot(p.astype(BF16), v_ref[:, sl].astype(BF16), preferred_element_type=F32)
        o_ref[:, sl] = (o / l).astype(BF16)


def mem_attend(q, mk, mv, kcol, vcol, o_buf, *, nbatch, rows_per_batch, row0, heads, dim, name):
    t, w = q.shape
    mtok = mk.shape[0] // nbatch
    tq = _pick(rows_per_batch, (512, 256, 128, 64))
    nq = rows_per_batch // tq
    r0 = row0 // tq
    in_specs = [pl.BlockSpec((tq, w), lambda b, i: (r0 + b * nq + i, 0)),
                pl.BlockSpec((mtok, w), lambda b, i: (b, kcol)),
                pl.BlockSpec((mtok, w), lambda b, i: (b, vcol))]
    args = [q, mk, mv]
    alias = {}
    if o_buf is not None:
        in_specs.append(pl.BlockSpec(memory_space=pl.ANY))
        args.append(o_buf)
        alias = {3: 0}
    return pl.pallas_call(
        functools.partial(_mem_kernel, heads=heads, dim=dim, scale=dim ** -0.5),
        grid=(nbatch, nq),
        in_specs=in_specs,
        out_specs=pl.BlockSpec((tq, w), lambda b, i: (r0 + b * nq + i, 0)),
        out_shape=jax.ShapeDtypeStruct((t, w), BF16),
        input_output_aliases=alias,
        compiler_params=_cp("parallel", "parallel"),
        name=name,
    )(*args)


def _ffn_up_kernel(x_ref, wa_ref, wg_ref, wdw_ref, bdw_ref, st_ref, *rest,
                   seg, nseg, blocks_per_seq, use_state, has_buf):
    if has_buf:
        rest = rest[1:]
    if use_state:
        u_ref, tail_ref = rest
        carry_ref = None
    else:
        u_ref, tail_ref, carry_ref = rest
    i, j = pl.program_id(0), pl.program_id(1)
    x = x_ref[...]
    a = jnp.dot(x, wa_ref[...], preferred_element_type=F32)
    g = jnp.dot(x, wg_ref[...], preferred_element_type=F32)
    tn = g.shape[1]
    w0, w1, w2 = wdw_ref[0:1, :], wdw_ref[1:2, :], wdw_ref[2:3, :]
    row = lax.broadcasted_iota(jnp.int32, (seg, tn), 0)
    if not use_state:
        @pl.when(i % blocks_per_seq == 0)
        def _():
            carry_ref[j] = jnp.zeros((8, tn), F32)
    for s in range(nseg):
        gs = g[s * seg:(s + 1) * seg]
        if use_state:
            p2, p1 = st_ref[s, 0:1, :], st_ref[s, 1:2, :]
        else:
            prev = carry_ref[j]
            p2, p1 = prev[6:7, :], prev[7:8, :]
        gm1 = jnp.where(row == 0, p1, pltpu.roll(gs, 1, 0))
        gm2 = jnp.where(row == 0, p2, jnp.where(row == 1, p1, pltpu.roll(gs, 2, 0)))
        gc = ((bdw_ref[...] + w0 * gm2) + w1 * gm1) + w2 * gs
        u_ref[s * seg:(s + 1) * seg, :] = (a[s * seg:(s + 1) * seg] * (gc * jax.nn.sigmoid(gc))).astype(BF16)
        tail = gs[seg - 8:seg]
        tail_ref[s] = tail
        if not use_state:
            carry_ref[j] = tail


def ffn_up(h, wa, wg, wdw, bdw, state, u_buf, *, row0, rows, seg, blocks_per_seq, use_state, name):
    t, d = h.shape
    ff = wa.shape[1]
    tn = min(512, ff) if u_buf is None else _pick(ff, (256, 128))
    if use_state:
        tm, nseg = rows, rows // seg
    else:
        tm, nseg = seg, 1
    r0 = row0 // tm
    ni, nj = rows // tm, pl.cdiv(ff, tn)
    scratch = [] if use_state else [pltpu.VMEM((nj, 8, tn), F32)]
    in_specs = [pl.BlockSpec((tm, d), lambda i, j: (r0 + i, 0)),
                pl.BlockSpec((d, tn), lambda i, j: (0, j)),
                pl.BlockSpec((d, tn), lambda i, j: (0, j)),
                pl.BlockSpec((CONV_W, tn), lambda i, j: (0, j)),
                pl.BlockSpec((1, tn), lambda i, j: (0, j)),
                pl.BlockSpec((state.shape[0], CONV_W - 1, tn), lambda i, j: (0, 0, j))]
    args = [h, wa, wg, wdw, bdw, state]
    alias = {}
    if u_buf is not None:
        in_specs.append(pl.BlockSpec(memory_space=pl.ANY))
        args.append(u_buf)
        alias = {6: 0}
    return pl.pallas_call(
        functools.partial(_ffn_up_kernel, seg=seg, nseg=nseg, blocks_per_seq=blocks_per_seq,
                          use_state=use_state, has_buf=u_buf is not None),
        grid=(ni, nj),
        in_specs=in_specs,
        out_specs=[pl.BlockSpec((tm, tn), lambda i, j: (r0 + i, j)),
                   pl.BlockSpec((nseg, 8, tn), lambda i, j: (i, 0, j))],
        out_shape=[jax.ShapeDtypeStruct((t, ff), BF16),
                   jax.ShapeDtypeStruct((ni * nseg, 8, ff), F32)],
        scratch_shapes=scratch,
        input_output_aliases=alias,
        compiler_params=_cp("arbitrary", "arbitrary"),
        name=name,
    )(*args)


def _rope_tables(pos):
    half = 32
    inv = ROPE_THETA ** (-jnp.arange(half, dtype=F32) / half)
    ang = pos.astype(F32)[:, None] * inv[None, :]
    c, s, z = jnp.cos(ang), jnp.sin(ang), jnp.zeros_like(ang)
    return jnp.concatenate([c, z, c, z], axis=1), jnp.concatenate([-s, z, s, z], axis=1)


def _slot_cols(w, half):
    z = jnp.zeros(w.shape[:-1] + (LANES // 2 - half,), w.dtype)
    return jnp.concatenate([w[..., :half], z, w[..., half:], z], axis=-1)


def kernel(x_prompt, x_sample, mem_prompt, cache_mla_ckv, cache_mla_krope, cache_band_k, cache_band_v, cache_mem_k, cache_mem_v, state_conv, norm_mix, w_in, norm_cq, norm_ckv, w_uq, w_uk, w_uv, rel_bias, g_out_a, g_out_b, w_o, norm_mem, norm_memtok, w_mq, w_mkv, w_mo, norm_ffn, w_up, w_dw, b_dw, w_down, norm_final):
    batch, seq, d = x_prompt.shape
    dbatch, dseq, _ = x_sample.shape
    depth = norm_mix.shape[0]
    past = cache_mla_ckv.shape[2]
    nband = cache_band_k.shape[2]
    ql, kvl = norm_cq.shape[1], norm_ckv.shape[1]
    rope = cache_mla_krope.shape[3]
    half = rope // 2
    a_heads, a_nope = w_uk.shape[2], w_uk.shape[3]
    a_vdim = w_uv.shape[3]
    b_heads, b_dim = cache_band_k.shape[3], cache_band_k.shape[4]
    mtok, m_heads, m_dim = cache_mem_k.shape[2], cache_mem_k.shape[3], cache_mem_k.shape[4]
    mem_w = m_heads * m_dim
    ff = b_dw.shape[1]
    assert a_nope == LANES and a_vdim == LANES and b_dim == LANES and rope == LANES // 2
    assert seq % CHUNK == 0 and dseq == CHUNK and past % CHUNK == 0

    tp, ts = batch * seq, dbatch * dseq
    t = tp + ts
    band_keep = min(BAND_PREV * CHUNK, seq)
    mla_scale = (a_nope + rope) ** -0.5
    b_scale = b_dim ** -0.5

    pos = jnp.concatenate([jnp.tile(jnp.arange(seq, dtype=jnp.int32), batch),
                           jnp.tile(past + jnp.arange(dseq, dtype=jnp.int32), dbatch)])
    cos_t, sin_t = _rope_tables(pos)
    band_tq = _pick(seq, (256, 128, 64))
    nprev = (BAND_PREV * CHUNK) // band_tq
    assert nprev * band_tq == BAND_PREV * CHUNK and nprev <= 2
    perm = _slot_cols(jnp.eye(rope, dtype=BF16), half)
    tk_down = _pick(ff, (5504, 2816, 2048, 1024, 512, 256))
    ffn_tm = _pick(math.gcd(seq, 1024), (1024, 512, 256, 128))
    mla_t = _pick(seq, (512, 256, 128))
    mla_nh = 2 if a_heads % 2 == 0 else 1
    cos_tt, sin_tt = cos_t[:tp].T, sin_t[:tp].T

    x = jnp.concatenate([x_prompt.reshape(tp, d), x_sample.reshape(ts, d)], axis=0)
    outs = {k: [] for k in ("p_ckv", "p_kr", "p_bk", "p_bv", "p_mk", "p_mv", "p_conv",
                            "s_ckv", "s_kr", "s_bk", "s_bv", "s_conv")}

    def unslot(kr):
        return jnp.concatenate([kr[:, :half], kr[:, LANES // 2:LANES // 2 + half]], axis=1)

    for l in range(depth):
        wi = w_in[l]
        w_lat = jnp.concatenate([wi[:, :ql + kvl], _slot_cols(wi[:, ql + kvl:ql + kvl + rope], half)],
                                axis=1).astype(BF16)
        w_qkvb = wi[:, ql + kvl + rope:].astype(BF16)
        wq = w_uq[l].reshape(ql, a_heads, a_nope + rope)
        wq = jnp.concatenate([wq[..., :a_nope], _slot_cols(wq[..., a_nope:], half)], axis=-1)
        wq = wq.reshape(ql, a_heads * HEAD_SLOT).astype(BF16)
        wq_t = wq.T
        wuk = w_uk[l].reshape(kvl, a_heads * a_nope).astype(BF16)
        wuv = w_uv[l].reshape(kvl, a_heads * a_vdim).astype(BF16)
        wuv_t = wuv.T
        wo = w_o[l].astype(BF16)
        wmq, wmkv, wmo = w_mq[l].astype(BF16), w_mkv[l].astype(BF16), w_mo[l].astype(BF16)
        wa = w_up[l][:, :ff].astype(BF16)
        wg = w_up[l][:, ff:].astype(BF16)
        wdw = w_dw[l]
        bdw = b_dw[l].reshape(1, ff)
        wdn = w_down[l].astype(BF16)
        st = state_conv[l]

        h = rmsnorm(x, norm_mix[l], BF16, name="norm_mix")
        cq, ckv, ckv_b, kr, kr_b = latent_project(h, w_lat, norm_cq[l], norm_ckv[l], cos_t, sin_t)
        qkvb = matmul(h, w_qkvb, BF16, name="qkv_band")
        sel = jnp.concatenate([h[b * seq + seq - band_keep:(b + 1) * seq] for b in range(batch)]
                              + [h[tp:]], axis=0)
        kv_keep = matmul(sel, w_qkvb[:, b_heads * b_dim:], F32, name="kv_band_keep")
        q_t = q_project_t(cq, wq_t, cos_tt, sin_tt, mla_scale * LOG2E, rows=tp, tq=mla_t)
        q_s = q_project(cq, wq, cos_t, sin_t, mla_scale * LOG2E, row0=tp, rows=ts)
        k, v, v_t = kv_decompress(ckv_b, kr_b, wuk, wuv, perm, permute=False, tm=mla_t, w_uv_t=wuv_t)
        kc, vc = kv_decompress(cache_mla_ckv[l].reshape(dbatch * past, kvl),
                               cache_mla_krope[l].reshape(dbatch * past, rope),
                               wuk, wuv, perm, permute=True, tm=_pick(dbatch * past, (512, 256, 128)))
        oa = mla_prompt(q_t, k, v_t, batch, seq, a_heads, mla_t, mla_nh)
        oa = mla_sample(q_s, kc, vc, k, v, oa, dbatch, dseq, past, a_heads, tp)
        bias_p = _band_bias_table(rel_bias[l], band_tq, (nprev + 1) * band_tq, nprev * band_tq)
        bias_s = _band_bias_table(rel_bias[l], dseq, nband + dseq, nband)
        ob = band_prompt(qkvb, bias_p, batch, seq, b_heads, band_tq, nprev, b_scale)
        ob = band_sample(qkvb, cache_band_k[l].reshape(dbatch * nband, b_heads * b_dim),
                         cache_band_v[l].reshape(dbatch * nband, b_heads * b_dim),
                         bias_s, ob, dbatch, dseq, nband, b_heads, tp, b_scale)
        x = matmul_residual(merge_norm(oa, ob, g_out_a[l], g_out_b[l]), wo, x, name="out_proj")

        memn = rmsnorm(mem_prompt.reshape(batch * mtok, d), norm_memtok[l], BF16, name="norm_memtok")
        mkv = matmul(memn, wmkv, F32, name="mem_kv")
        h = rmsnorm(x, norm_mem[l], BF16, name="norm_mem")
        qm = matmul(h, wmq, BF16, name="mem_q")
        om = mem_attend(qm, mkv, mkv, 0, 1, None, nbatch=batch, rows_per_batch=seq, row0=0,
                        heads=m_heads, dim=m_dim, name="mem_prompt")
        cmk = cache_mem_k[l].reshape(dbatch * mtok, mem_w)
        cmv = cache_mem_v[l].reshape(dbatch * mtok, mem_w)
        om = mem_attend(qm, cmk, cmv, 0, 0, om, nbatch=dbatch, rows_per_batch=dseq, row0=tp,
                        heads=m_heads, dim=m_dim, name="mem_sample")
        x = matmul_residual(om, wmo, x, name="mem_out")

        h = rmsnorm(x, norm_ffn[l], BF16, name="norm_ffn")
        u, tail_p = ffn_up(h, wa, wg, wdw, bdw, st, None, row0=0, rows=tp, seg=ffn_tm,
                           blocks_per_seq=seq // ffn_tm, use_state=False, name="ffn_up_prompt")
        u, tail_s = ffn_up(h, wa, wg, wdw, bdw, st, u, row0=tp, rows=ts, seg=dseq,
                           blocks_per_seq=1, use_state=True, name="ffn_up_sample")
        x = matmul_residual_ktiled(u, wdn, x, tk=tk_down, name="ffn_down")

        outs["p_ckv"].append(ckv[:tp].reshape(batch, seq, kvl))
        outs["s_ckv"].append(ckv[tp:].reshape(dbatch, dseq, kvl))
        kr64 = unslot(kr)
        outs["p_kr"].append(kr64[:tp].reshape(batch, seq, rope))
        outs["s_kr"].append(kr64[tp:].reshape(dbatch, dseq, rope))
        hw = b_heads * b_dim
        nkp = batch * band_keep
        outs["p_bk"].append(kv_keep[:nkp, :hw].reshape(batch, band_keep, b_heads, b_dim))
        outs["p_bv"].append(kv_keep[:nkp, hw:].reshape(batch, band_keep, b_heads, b_dim))
        outs["s_bk"].append(kv_keep[nkp:, :hw].reshape(dbatch, dseq, b_heads, b_dim))
        outs["s_bv"].append(kv_keep[nkp:, hw:].reshape(dbatch, dseq, b_heads, b_dim))
        outs["p_mk"].append(mkv[:, :mem_w].reshape(batch, mtok, m_heads, m_dim))
        outs["p_mv"].append(mkv[:, mem_w:].reshape(batch, mtok, m_heads, m_dim))
        nblk = seq // ffn_tm
        tail_p = tail_p.reshape(batch, nblk, 8, ff)[:, nblk - 1, 8 - (CONV_W - 1):]
        outs["p_conv"].append(tail_p)
        outs["s_conv"].append(tail_s[:, 8 - (CONV_W - 1):])

    y_prompt = rmsnorm(x, norm_final, F32, row0=0, rows=tp, name="norm_final_p").reshape(batch, seq, d)
    y_sample = rmsnorm(x, norm_final, F32, row0=tp, rows=ts, name="norm_final_s").reshape(dbatch, dseq, d)
    st = {k_: jnp.stack(v_) for k_, v_ in outs.items()}
    return (y_prompt, y_sample, st["p_ckv"], st["p_kr"], st["p_bk"], st["p_bv"], st["p_mk"],
            st["p_mv"], st["p_conv"], st["s_ckv"], st["s_kr"], st["s_bk"], st["s_bv"], st["s_conv"])
```

```python
import functools
import math

import jax
import jax.numpy as jnp
import numpy as np
from jax import lax
from jax.experimental import pallas as pl
from jax.experimental.pallas import tpu as pltpu

CHUNK = 64
BAND_PREV = 8
MAX_REL = 128
CONV_W = 3
ROPE_THETA = 10000.0
EPS = 1e-6
NEG_INF = -1e30
LOG2E = math.log2(math.e)

LANES = 128
HEAD_SLOT = 256
VMEM_LIMIT = 56 * 1024 * 1024
FFN_CHUNK = 256

F32 = jnp.float32
BF16 = jnp.bfloat16


def _cp(*sem):
    return pltpu.CompilerParams(dimension_semantics=sem, vmem_limit_bytes=VMEM_LIMIT)


def _pick(n, prefs):
    for p in prefs:
        if n % p == 0:
            return p
    return n


def _round_up(n, m):
    return (n + m - 1) // m * m


def _rmsnorm_rows(x, g):
    ms = jnp.mean(x * x, axis=-1, keepdims=True)
    return (x * lax.rsqrt(ms + EPS)) * g


def _rmsnorm_kernel(x_ref, g_ref, o_ref):
    o_ref[...] = _rmsnorm_rows(x_ref[...].astype(F32), g_ref[...]).astype(o_ref.dtype)


def rmsnorm(x, g, out_dtype, *, row0=0, rows=None, name="rmsnorm"):
    m, d = x.shape
    rows = m if rows is None else rows
    tm = _pick(math.gcd(rows, row0) if row0 else rows, (256, 128, 64, 32, 16, 8))
    off = row0 // tm
    return pl.pallas_call(
        _rmsnorm_kernel,
        grid=(rows // tm,),
        in_specs=[pl.BlockSpec((tm, d), lambda i: (i + off, 0)),
                  pl.BlockSpec((1, d), lambda i: (0, 0))],
        out_specs=pl.BlockSpec((tm, d), lambda i: (i, 0)),
        out_shape=jax.ShapeDtypeStruct((rows, d), out_dtype),
        compiler_params=_cp("parallel"),
        name=name,
    )(x, g.reshape(1, d).astype(F32))


def _mm_kernel(x_ref, w_ref, o_ref):
    o_ref[...] = jnp.dot(x_ref[...].astype(BF16), w_ref[...],
                         preferred_element_type=F32).astype(o_ref.dtype)


def matmul(x, w, out_dtype, *, tm_prefs=(1024, 512, 256, 128, 64, 32, 16, 8),
           tn_prefs=(512, 256, 128), name="matmul"):
    m, k = x.shape
    n = w.shape[1]
    tm, tn = _pick(m, tm_prefs), _pick(n, tn_prefs)
    return pl.pallas_call(
        _mm_kernel,
        grid=(m // tm, n // tn),
        in_specs=[pl.BlockSpec((tm, k), lambda i, j: (i, 0)),
                  pl.BlockSpec((k, tn), lambda i, j: (0, j))],
        out_specs=pl.BlockSpec((tm, tn), lambda i, j: (i, j)),
        out_shape=jax.ShapeDtypeStruct((m, n), out_dtype),
        compiler_params=_cp("parallel", "parallel"),
        name=name,
    )(x, w)


def _mm_res_kernel(x_ref, w_ref, r_ref, o_ref):
    o_ref[...] = r_ref[...] + jnp.dot(x_ref[...], w_ref[...], preferred_element_type=F32)


def matmul_residual(x, w, res, *, name="matmul_res"):
    m, k = x.shape
    n = w.shape[1]
    tm, tn = _pick(m, (1024, 512, 256, 128)), _pick(n, (512, 256, 128))
    return pl.pallas_call(
        _mm_res_kernel,
        grid=(m // tm, n // tn),
        in_specs=[pl.BlockSpec((tm, k), lambda i, j: (i, 0)),
                  pl.BlockSpec((k, tn), lambda i, j: (0, j)),
                  pl.BlockSpec((tm, tn), lambda i, j: (i, j))],
        out_specs=pl.BlockSpec((tm, tn), lambda i, j: (i, j)),
        out_shape=jax.ShapeDtypeStruct((m, n), F32),
        input_output_aliases={2: 0},
        compiler_params=_cp("parallel", "parallel"),
        name=name,
    )(x, w, res)


def _mm_res_ktiled_kernel(x_ref, w_ref, r_ref, o_ref, acc_ref):
    kk = pl.program_id(2)

    @pl.when(kk == 0)
    def _():
        acc_ref[...] = jnp.zeros_like(acc_ref)

    acc_ref[...] += jnp.dot(x_ref[...], w_ref[...], preferred_element_type=F32)

    @pl.when(kk == pl.num_programs(2) - 1)
    def _():
        o_ref[...] = r_ref[...] + acc_ref[...]


def matmul_residual_ktiled(x, w, res, *, tk, name="matmul_res_k"):
    m, k = x.shape
    n = w.shape[1]
    tm, tn = _pick(m, (1024, 512, 256, 128)), _pick(n, (512, 256, 128))
    return pl.pallas_call(
        _mm_res_ktiled_kernel,
        grid=(m // tm, n // tn, k // tk),
        in_specs=[pl.BlockSpec((tm, tk), lambda i, j, kk: (i, kk)),
                  pl.BlockSpec((tk, tn), lambda i, j, kk: (kk, j)),
                  pl.BlockSpec((tm, tn), lambda i, j, kk: (i, j))],
        out_specs=pl.BlockSpec((tm, tn), lambda i, j, kk: (i, j)),
        out_shape=jax.ShapeDtypeStruct((m, n), F32),
        scratch_shapes=[pltpu.VMEM((tm, tn), F32)],
        input_output_aliases={2: 0},
        compiler_params=_cp("parallel", "parallel", "arbitrary"),
        name=name,
    )(x, w, res)


def _rope_slot(r, cos, sin):
    return r * cos + pltpu.roll(r, 64, 1) * sin


def _lat_kernel(h_ref, w_ref, gq_ref, gkv_ref, cos_ref, sin_ref,
                cq_ref, ckv_ref, ckvb_ref, kr_ref, krb_ref, *, ql, kvl):
    acc = jnp.dot(h_ref[...], w_ref[...], preferred_element_type=F32)
    cq_ref[...] = _rmsnorm_rows(acc[:, :ql], gq_ref[...]).astype(BF16)
    ckv = _rmsnorm_rows(acc[:, ql:ql + kvl], gkv_ref[...])
    ckv_ref[...] = ckv
    ckvb_ref[...] = ckv.astype(BF16)
    kr = _rope_slot(acc[:, ql + kvl:], cos_ref[...], sin_ref[...])
    kr_ref[...] = kr
    krb_ref[...] = kr.astype(BF16)


def latent_project(h, w_lat, g_cq, g_ckv, cos_t, sin_t):
    m, d = h.shape
    ql, kvl = g_cq.shape[0], g_ckv.shape[0]
    n = w_lat.shape[1]
    tm = _pick(m, (512, 256, 128))
    row = lambda i: (i, 0)
    fix = lambda i: (0, 0)
    return pl.pallas_call(
        functools.partial(_lat_kernel, ql=ql, kvl=kvl),
        grid=(m // tm,),
        in_specs=[pl.BlockSpec((tm, d), row), pl.BlockSpec((d, n), fix),
                  pl.BlockSpec((1, ql), fix), pl.BlockSpec((1, kvl), fix),
                  pl.BlockSpec((tm, LANES), row), pl.BlockSpec((tm, LANES), row)],
        out_specs=[pl.BlockSpec((tm, ql), row), pl.BlockSpec((tm, kvl), row),
                   pl.BlockSpec((tm, kvl), row), pl.BlockSpec((tm, LANES), row),
                   pl.BlockSpec((tm, LANES), row)],
        out_shape=[jax.ShapeDtypeStruct((m, ql), BF16), jax.ShapeDtypeStruct((m, kvl), F32),
                   jax.ShapeDtypeStruct((m, kvl), BF16), jax.ShapeDtypeStruct((m, LANES), F32),
                   jax.ShapeDtypeStruct((m, LANES), BF16)],
        compiler_params=_cp("parallel"),
        name="latent_project",
    )(h, w_lat, g_cq.reshape(1, ql), g_ckv.reshape(1, kvl), cos_t, sin_t)


def _q_kernel(c_ref, w_ref, cos_ref, sin_ref, o_ref, *, heads, scale):
    acc = jnp.dot(c_ref[...], w_ref[...], preferred_element_type=F32)
    cos, sin = cos_ref[...], sin_ref[...]
    for h in range(heads):
        lo = h * HEAD_SLOT
        o_ref[:, lo:lo + LANES] = (acc[:, lo:lo + LANES] * scale).astype(BF16)
        r = _rope_slot(acc[:, lo + LANES:lo + HEAD_SLOT], cos, sin)
        o_ref[:, lo + LANES:lo + HEAD_SLOT] = (r * scale).astype(BF16)


def q_project(cq, w_uq, cos_t, sin_t, scale, *, row0, rows):
    ql = cq.shape[1]
    n = w_uq.shape[1]
    tm = _pick(math.gcd(rows, row0), (512, 256, 128, 64))
    tn = _pick(n, (1024, 512, 256))
    r0 = row0 // tm
    return pl.pallas_call(
        functools.partial(_q_kernel, heads=tn // HEAD_SLOT, scale=scale),
        grid=(rows // tm, n // tn),
        in_specs=[pl.BlockSpec((tm, ql), lambda i, j: (r0 + i, 0)),
                  pl.BlockSpec((ql, tn), lambda i, j: (0, j)),
                  pl.BlockSpec((tm, LANES), lambda i, j: (r0 + i, 0)),
                  pl.BlockSpec((tm, LANES), lambda i, j: (r0 + i, 0))],
        out_specs=pl.BlockSpec((tm, tn), lambda i, j: (i, j)),
        out_shape=jax.ShapeDtypeStruct((rows, n), BF16),
        compiler_params=_cp("parallel", "parallel"),
        name="q_project",
    )(cq, w_uq, cos_t, sin_t)


def _qt_kernel(c_ref, wt_ref, cos_ref, sin_ref, o_ref, *, heads, scale):
    acc = lax.dot_general(wt_ref[...], c_ref[...], (((1,), (1,)), ((), ())),
                          preferred_element_type=F32)
    cos, sin = cos_ref[...], sin_ref[...]
    for h in range(heads):
        lo = h * HEAD_SLOT
        o_ref[0, lo:lo + LANES, :] = (acc[lo:lo + LANES] * scale).astype(BF16)
        r = acc[lo + LANES:lo + HEAD_SLOT]
        r = r * cos + pltpu.roll(r, 64, 0) * sin
        o_ref[0, lo + LANES:lo + HEAD_SLOT, :] = (r * scale).astype(BF16)


def q_project_t(cq, w_uq_t, cos_tt, sin_tt, scale, *, rows, tq):
    ql = cq.shape[1]
    n = w_uq_t.shape[0]
    tn = _pick(n, (1024, 512, 256))
    return pl.pallas_call(
        functools.partial(_qt_kernel, heads=tn // HEAD_SLOT, scale=scale),
        grid=(rows // tq, n // tn),
        in_specs=[pl.BlockSpec((tq, ql), lambda i, j: (i, 0)),
                  pl.BlockSpec((tn, ql), lambda i, j: (j, 0)),
                  pl.BlockSpec((LANES, tq), lambda i, j: (0, i)),
                  pl.BlockSpec((LANES, tq), lambda i, j: (0, i))],
        out_specs=pl.BlockSpec((1, tn, tq), lambda i, j: (i, j, 0)),
        out_shape=jax.ShapeDtypeStruct((rows // tq, n, tq), BF16),
        compiler_params=_cp("parallel", "parallel"),
        name="q_project_t",
    )(cq, w_uq_t, cos_tt, sin_tt)


def _kv_kernel(c_ref, kr_ref, perm_ref, wk_ref, wv_ref, *rest, heads, permute, with_vt):
    if with_vt:
        wvt_ref, k_ref, v_ref, vt_ref = rest
    else:
        k_ref, v_ref = rest
    c = c_ref[...].astype(BF16)
    kn = jnp.dot(c, wk_ref[...], preferred_element_type=F32)
    v_ref[...] = jnp.dot(c, wv_ref[...], preferred_element_type=F32).astype(BF16)
    if with_vt:
        vt_ref[0] = lax.dot_general(wvt_ref[...], c, (((1,), (1,)), ((), ())),
                                    preferred_element_type=F32).astype(BF16)
    kr = kr_ref[...].astype(BF16)
    if permute:
        kr = jnp.dot(kr, perm_ref[...], preferred_element_type=F32).astype(BF16)
    for h in range(heads):
        lo = h * HEAD_SLOT
        k_ref[:, lo:lo + LANES] = kn[:, h * LANES:(h + 1) * LANES].astype(BF16)
        k_ref[:, lo + LANES:lo + HEAD_SLOT] = kr


def kv_decompress(ckv, kr, w_uk, w_uv, perm, *, permute, tm, w_uv_t=None):
    m, kvl = ckv.shape
    heads = w_uk.shape[1] // LANES
    with_vt = w_uv_t is not None
    row = lambda i: (i, 0)
    fix = lambda i: (0, 0)
    in_specs = [pl.BlockSpec((tm, kvl), row), pl.BlockSpec((tm, kr.shape[1]), row),
                pl.BlockSpec(perm.shape, fix),
                pl.BlockSpec(w_uk.shape, fix), pl.BlockSpec(w_uv.shape, fix)]
    out_specs = [pl.BlockSpec((tm, heads * HEAD_SLOT), row), pl.BlockSpec((tm, heads * LANES), row)]
    out_shape = [jax.ShapeDtypeStruct((m, heads * HEAD_SLOT), BF16),
                 jax.ShapeDtypeStruct((m, heads * LANES), BF16)]
    args = [ckv, kr, perm, w_uk, w_uv]
    if with_vt:
        in_specs.append(pl.BlockSpec(w_uv_t.shape, fix))
        out_specs.append(pl.BlockSpec((1, heads * LANES, tm), lambda i: (i, 0, 0)))
        out_shape.append(jax.ShapeDtypeStruct((pl.cdiv(m, tm), heads * LANES, tm), BF16))
        args.append(w_uv_t)
    return pl.pallas_call(
        functools.partial(_kv_kernel, heads=heads, permute=permute, with_vt=with_vt),
        grid=(pl.cdiv(m, tm),),
        in_specs=in_specs,
        out_specs=out_specs,
        out_shape=out_shape,
        compiler_params=_cp("parallel"),
        name="kv_decompress",
    )(*args)


def _qk(q, k):
    return lax.dot_general(q, k, (((1,), (1,)), ((), ())), preferred_element_type=F32)


def _mla_prompt_kernel(qt_ref, k_ref, vt_ref, o_ref, m_ref, l_ref, acc_ref, s0_ref, s1_ref, *, t, nh):
    qi = pl.program_id(2)
    m_ref[...] = jnp.full_like(m_ref, NEG_INF)
    l_ref[...] = jnp.zeros_like(l_ref)
    acc_ref[...] = jnp.zeros_like(acc_ref)

    def scores(ki, s_ref):
        start = pl.multiple_of(ki * t, t)
        for h in range(nh):
            s_ref[h] = jnp.dot(k_ref[pl.ds(start, t), h * HEAD_SLOT:(h + 1) * HEAD_SLOT],
                               qt_ref[0, h * HEAD_SLOT:(h + 1) * HEAD_SLOT, :],
                               preferred_element_type=F32)

    def consume(ki, s_ref, mask):
        for h in range(nh):
            s = s_ref[h]
            if mask is not None:
                s = jnp.where(mask, s, NEG_INF)
            m_old = m_ref[h]
            m_new = jnp.maximum(m_old, jnp.max(s, axis=0, keepdims=True))
            alpha = jnp.exp2(m_old - m_new)
            p = jnp.exp2(s - m_new)
            l_ref[h] = alpha * l_ref[h] + jnp.sum(p, axis=0, keepdims=True)
            acc_ref[h] = alpha * acc_ref[h] + jnp.dot(
                vt_ref[ki, h * LANES:(h + 1) * LANES, :], p.astype(BF16),
                preferred_element_type=F32)
            m_ref[h] = m_new

    scores(0, s0_ref)

    def body(j, carry):
        scores(2 * j + 1, s1_ref)
        consume(2 * j, s0_ref, None)
        scores(2 * j + 2, s0_ref)
        consume(2 * j + 1, s1_ref, None)
        return carry

    lax.fori_loop(0, qi // 2, body, 0)
    kc = lax.broadcasted_iota(jnp.int32, (t, t), 0) // CHUNK
    qc = lax.broadcasted_iota(jnp.int32, (t, t), 1) // CHUNK
    mask = kc <= qc

    @pl.when(qi % 2 == 0)
    def _():
        consume(qi, s0_ref, mask)

    @pl.when(qi % 2 == 1)
    def _():
        scores(qi, s1_ref)
        consume(qi - 1, s0_ref, None)
        consume(qi, s1_ref, mask)

    for h in range(nh):
        o_ref[:, h * LANES:(h + 1) * LANES] = (acc_ref[h] / l_ref[h]).T


def mla_prompt(qt, k, vt, batch, seq, heads, t, nh):
    rows = k.shape[0]
    nq = seq // t
    return pl.pallas_call(
        functools.partial(_mla_prompt_kernel, t=t, nh=nh),
        grid=(batch, heads // nh, nq),
        in_specs=[pl.BlockSpec((1, nh * HEAD_SLOT, t), lambda b, g, i: (b * nq + i, g, 0)),
                  pl.BlockSpec((seq, nh * HEAD_SLOT), lambda b, g, i: (b, g)),
                  pl.BlockSpec((nq, nh * LANES, t), lambda b, g, i: (b, g, 0))],
        out_specs=pl.BlockSpec((t, nh * LANES), lambda b, g, i: (b * nq + i, g)),
        out_shape=jax.ShapeDtypeStruct((rows, heads * LANES), F32),
        scratch_shapes=[pltpu.VMEM((nh, 1, t), F32), pltpu.VMEM((nh, 1, t), F32),
                        pltpu.VMEM((nh, LANES, t), F32),
                        pltpu.VMEM((nh, t, t), F32), pltpu.VMEM((nh, t, t), F32)],
        compiler_params=_cp("parallel", "parallel", "arbitrary"),
        name="mla_prompt",
    )(qt, k, vt)


def _mla_sample_kernel(q_ref, kc_ref, vc_ref, kn_ref, vn_ref, oin_ref, o_ref, *, nh):
    del oin_ref
    for h in range(nh):
        qs = slice(h * HEAD_SLOT, (h + 1) * HEAD_SLOT)
        vs = slice(h * LANES, (h + 1) * LANES)
        q = q_ref[:, qs]
        s1 = _qk(q, kc_ref[:, qs])
        s2 = _qk(q, kn_ref[:, qs])
        m = jnp.maximum(jnp.max(s1, axis=-1, keepdims=True), jnp.max(s2, axis=-1, keepdims=True))
        p1 = jnp.exp2(s1 - m)
        p2 = jnp.exp2(s2 - m)
        l = jnp.sum(p1, axis=-1, keepdims=True) + jnp.sum(p2, axis=-1, keepdims=True)
        o = (jnp.dot(p1.astype(BF16), vc_ref[:, vs], preferred_element_type=F32)
             + jnp.dot(p2.astype(BF16), vn_ref[:, vs], preferred_element_type=F32))
        o_ref[:, vs] = o / l


def mla_sample(q, kc, vc, k, v, o_buf, dbatch, dseq, past, heads, row0):
    r0 = row0 // dseq
    nh = _pick(heads, (4, 2, 1))
    return pl.pallas_call(
        functools.partial(_mla_sample_kernel, nh=nh),
        grid=(dbatch, heads // nh),
        in_specs=[pl.BlockSpec((dseq, nh * HEAD_SLOT), lambda b, g: (b, g)),
                  pl.BlockSpec((past, nh * HEAD_SLOT), lambda b, g: (b, g)),
                  pl.BlockSpec((past, nh * LANES), lambda b, g: (b, g)),
                  pl.BlockSpec((dseq, nh * HEAD_SLOT), lambda b, g: (r0 + b, g)),
                  pl.BlockSpec((dseq, nh * LANES), lambda b, g: (r0 + b, g)),
                  pl.BlockSpec(memory_space=pl.ANY)],
        out_specs=pl.BlockSpec((dseq, nh * LANES), lambda b, g: (r0 + b, g)),
        out_shape=jax.ShapeDtypeStruct(o_buf.shape, F32),
        input_output_aliases={5: 0},
        compiler_params=_cp("parallel", "parallel"),
        name="mla_sample",
    )(q, kc, vc, k, v, o_buf)


def _band_prompt_kernel(q_ref, k0_ref, k1_ref, k2_ref, v0_ref, v1_ref, v2_ref, bias_ref, o_ref,
                        *, tq, nprev, scale, nh):
    qi = pl.program_id(2)
    ks = (k0_ref, k1_ref, k2_ref)[3 - nprev - 1:]
    vs = (v0_ref, v1_ref, v2_ref)[3 - nprev - 1:]
    for h in range(nh):
        hs = slice(h * LANES, (h + 1) * LANES)
        q = q_ref[:, hs]
        ss = []
        for d, kr in enumerate(ks):
            s = _qk(q, kr[:, hs]) * scale + bias_ref[h, :, d * tq:(d + 1) * tq]
            if d < nprev:
                s = jnp.where(qi - (nprev - d) >= 0, s, NEG_INF)
            ss.append(s)
        m = functools.reduce(jnp.maximum, [jnp.max(s, axis=-1, keepdims=True) for s in ss])
        ps = [jnp.exp(s - m) for s in ss]
        l = functools.reduce(lambda a, b: a + b, [jnp.sum(p, axis=-1, keepdims=True) for p in ps])
        o = functools.reduce(lambda a, b: a + b,
                             [jnp.dot(p.astype(BF16), vr[:, hs], preferred_element_type=F32)
                              for p, vr in zip(ps, vs)])
        o_ref[:, hs] = o / l


def band_prompt(qkv, bias, batch, seq, heads, tq, nprev, scale):
    t = qkv.shape[0]
    nq = seq // tq
    nh = _pick(heads, (4, 2, 1))
    ng = heads // nh

    def kspec(back, col0):
        return pl.BlockSpec((tq, nh * LANES),
                            lambda b, g, i: (b * nq + jnp.maximum(i - back, 0), col0 + g))

    return pl.pallas_call(
        functools.partial(_band_prompt_kernel, tq=tq, nprev=nprev, scale=scale, nh=nh),
        grid=(batch, ng, nq),
        in_specs=[pl.BlockSpec((tq, nh * LANES), lambda b, g, i: (b * nq + i, g)),
                  kspec(2, ng), kspec(1, ng), kspec(0, ng),
                  kspec(2, 2 * ng), kspec(1, 2 * ng), kspec(0, 2 * ng),
                  pl.BlockSpec((nh, tq, (nprev + 1) * tq), lambda b, g, i: (g, 0, 0))],
        out_specs=pl.BlockSpec((tq, nh * LANES), lambda b, g, i: (b * nq + i, g)),
        out_shape=jax.ShapeDtypeStruct((t, heads * LANES), F32),
        compiler_params=_cp("parallel", "parallel", "arbitrary"),
        name="band_prompt",
    )(qkv, qkv, qkv, qkv, qkv, qkv, qkv, bias)


def _band_sample_kernel(q_ref, kc_ref, vc_ref, kn_ref, vn_ref, bias_ref, oin_ref, o_ref,
                        *, nb, scale, nh):
    del oin_ref
    for h in range(nh):
        hs = slice(h * LANES, (h + 1) * LANES)
        q = q_ref[:, hs]
        s1 = _qk(q, kc_ref[:, hs].astype(BF16)) * scale + bias_ref[h, :, :nb]
        s2 = _qk(q, kn_ref[:, hs]) * scale + bias_ref[h, :, nb:]
        m = jnp.maximum(jnp.max(s1, axis=-1, keepdims=True), jnp.max(s2, axis=-1, keepdims=True))
        p1 = jnp.exp(s1 - m)
        p2 = jnp.exp(s2 - m)
        l = jnp.sum(p1, axis=-1, keepdims=True) + jnp.sum(p2, axis=-1, keepdims=True)
        o = (jnp.dot(p1.astype(BF16), vc_ref[:, hs].astype(BF16), preferred_element_type=F32)
             + jnp.dot(p2.astype(BF16), vn_ref[:, hs], preferred_element_type=F32))
        o_ref[:, hs] = o / l


def band_sample(qkv, kc, vc, bias, o_buf, dbatch, dseq, nb, heads, row0, scale):
    r0 = row0 // dseq
    nh = _pick(heads, (4, 2, 1))
    ng = heads // nh
    w = nh * LANES
    return pl.pallas_call(
        functools.partial(_band_sample_kernel, nb=nb, scale=scale, nh=nh),
        grid=(dbatch, ng),
        in_specs=[pl.BlockSpec((dseq, w), lambda b, g: (r0 + b, g)),
                  pl.BlockSpec((nb, w), lambda b, g: (b, g)),
                  pl.BlockSpec((nb, w), lambda b, g: (b, g)),
                  pl.BlockSpec((dseq, w), lambda b, g: (r0 + b, ng + g)),
                  pl.BlockSpec((dseq, w), lambda b, g: (r0 + b, 2 * ng + g)),
                  pl.BlockSpec((nh, dseq, nb + dseq), lambda b, g: (g, 0, 0)),
                  pl.BlockSpec(memory_space=pl.ANY)],
        out_specs=pl.BlockSpec((dseq, w), lambda b, g: (r0 + b, g)),
        out_shape=jax.ShapeDtypeStruct(o_buf.shape, F32),
        input_output_aliases={6: 0},
        compiler_params=_cp("parallel", "parallel"),
        name="band_sample",
    )(qkv, kc, vc, qkv, qkv, bias, o_buf)


def _band_bias_table(rel_bias, nq, nk, q0):
    span = nq + nk - 1
    k = np.concatenate([np.arange(0, nk), np.arange(-(nq - 1), 0)])
    idx = np.clip(q0 - k, -MAX_REL, MAX_REL) + MAX_REL
    u = rel_bias.astype(F32)[:, idx]
    tab = jnp.tile(u, (1, nq))[:, :nq * (span - 1)].reshape(-1, nq, span - 1)[:, :, :nk]
    qc = (q0 + np.arange(nq))[:, None] // CHUNK
    kc = np.arange(nk)[None, :] // CHUNK
    mask = (kc <= qc) & (kc >= qc - BAND_PREV)
    return jnp.where(mask[None], tab, NEG_INF)


def _merge_kernel(a_ref, b_ref, ga_ref, gb_ref, o_ref):
    w = a_ref.shape[1]
    o_ref[:, :w] = _rmsnorm_rows(a_ref[...], ga_ref[...]).astype(BF16)
    o_ref[:, w:] = _rmsnorm_rows(b_ref[...], gb_ref[...]).astype(BF16)


def merge_norm(oa, ob, ga, gb):
    m, w = oa.shape
    tm = _pick(m, (256, 128))
    row = lambda i: (i, 0)
    fix = lambda i: (0, 0)
    return pl.pallas_call(
        _merge_kernel,
        grid=(m // tm,),
        in_specs=[pl.BlockSpec((tm, w), row), pl.BlockSpec((tm, w), row),
                  pl.BlockSpec((1, w), fix), pl.BlockSpec((1, w), fix)],
        out_specs=pl.BlockSpec((tm, 2 * w), row),
        out_shape=jax.ShapeDtypeStruct((m, 2 * w), BF16),
        compiler_params=_cp("parallel"),
        name="merge_norm",
    )(oa, ob, ga.reshape(1, w), gb.reshape(1, w))


def _mem_kernel(q_ref, k_ref, v_ref, *rest, heads, dim, scale):
    o_ref = rest[-1]
    for h in range(heads):
        sl = slice(h * dim, (h + 1) * dim)
        s = _qk(q_ref[:, sl], k_ref[:, sl].astype(BF16)) * scale
        m = jnp.max(s, axis=-1, keepdims=True)
        p = jnp.exp(s - m)
        l = jnp.sum(p, axis=-1, keepdims=True)
        o = jnp.dot(p.astype(BF16), v_ref[:, sl].astype(BF16), preferred_element_type=F32)
        o_ref[:, sl] = (o / l).astype(BF16)


def mem_attend(q, mk, mv, kcol, vcol, o_buf, *, nbatch, rows_per_batch, row0, heads, dim, name):
    t, w = q.shape
    mtok = mk.shape[0] // nbatch
    tq = _pick(rows_per_batch, (512, 256, 128, 64))
    nq = rows_per_batch // tq
    r0 = row0 // tq
    in_specs = [pl.BlockSpec((tq, w), lambda b, i: (r0 + b * nq + i, 0)),
                pl.BlockSpec((mtok, w), lambda b, i: (b, kcol)),
                pl.BlockSpec((mtok, w), lambda b, i: (b, vcol))]
    args = [q, mk, mv]
    alias = {}
    if o_buf is not None:
        in_specs.append(pl.BlockSpec(memory_space=pl.ANY))
        args.append(o_buf)
        alias = {3: 0}
    return pl.pallas_call(
        functools.partial(_mem_kernel, heads=heads, dim=dim, scale=dim ** -0.5),
        grid=(nbatch, nq),
        in_specs=in_specs,
        out_specs=pl.BlockSpec((tq, w), lambda b, i: (r0 + b * nq + i, 0)),
        out_shape=jax.ShapeDtypeStruct((t, w), BF16),
        input_output_aliases=alias,
        compiler_params=_cp("parallel", "parallel"),
        name=name,
    )(*args)


def _ffn_up_kernel(x_ref, wa_ref, wg_ref, wdw_ref, bdw_ref, st_ref, *rest,
                   seg, nseg, blocks_per_seq, use_state, has_buf):
    if has_buf:
        rest = rest[1:]
    if use_state:
        u_ref, tail_ref = rest
        carry_ref = None
    else:
        u_ref, tail_ref, carry_ref = rest
    i, j = pl.program_id(0), pl.program_id(1)
    tn = wa_ref.shape[1]
    tc = min(tn, FFN_CHUNK)
    row = lax.broadcasted_iota(jnp.int32, (8, tc), 0)
    if not use_state:
        @pl.when(i % blocks_per_seq == 0)
        def _():
            carry_ref[j] = jnp.zeros((8, tn), F32)
    for c in range(tn // tc):
        cs = slice(c * tc, (c + 1) * tc)
        a = jnp.dot(x_ref[...], wa_ref[:, cs], preferred_element_type=F32)
        g = jnp.dot(x_ref[...], wg_ref[:, cs], preferred_element_type=F32)
        w0, w1, w2 = wdw_ref[0:1, cs], wdw_ref[1:2, cs], wdw_ref[2:3, cs]
        for s in range(nseg):
            gs = g[s * seg:(s + 1) * seg]
            if use_state:
                p2, p1 = st_ref[s, 0:1, cs], st_ref[s, 1:2, cs]
            else:
                p2, p1 = carry_ref[j, 6:7, cs], carry_ref[j, 7:8, cs]
            prev8 = jnp.where(row == 6, p2, jnp.where(row == 7, p1, 0.0))
            g3 = gs.reshape(seg // 8, 8, tc)
            shifted = []
            for sh in (1, 2):
                cur = pltpu.roll(g3, sh, 1)
                before = jnp.concatenate([pltpu.roll(prev8, sh, 0)[None], cur[:-1]], axis=0)
                shifted.append(jnp.where(row[None] >= sh, cur, before).reshape(seg, tc))
            gm1, gm2 = shifted
            gc = ((bdw_ref[:, cs] + w0 * gm2) + w1 * gm1) + w2 * gs
            u_ref[s * seg:(s + 1) * seg, cs] = (
                a[s * seg:(s + 1) * seg] * (gc * jax.nn.sigmoid(gc))).astype(BF16)
            tail = gs[seg - 8:seg]
            tail_ref[s, :, cs] = tail
            if not use_state:
                carry_ref[j, :, cs] = tail


def ffn_up(h, wa, wg, wdw, bdw, state, u_buf, *, row0, rows, seg, blocks_per_seq, use_state, name):
    t, d = h.shape
    ff = wa.shape[1]
    tn = min(512, ff) if u_buf is None else _pick(ff, (256, 128))
    if use_state:
        tm, nseg = rows, rows // seg
    else:
        tm, nseg = seg, 1
    r0 = row0 // tm
    ni, nj = rows // tm, pl.cdiv(ff, tn)
    scratch = [] if use_state else [pltpu.VMEM((nj, 8, tn), F32)]
    in_specs = [pl.BlockSpec((tm, d), lambda i, j: (r0 + i, 0)),
                pl.BlockSpec((d, tn), lambda i, j: (0, j)),
                pl.BlockSpec((d, tn), lambda i, j: (0, j)),
                pl.BlockSpec((CONV_W, tn), lambda i, j: (0, j)),
                pl.BlockSpec((1, tn), lambda i, j: (0, j)),
                pl.BlockSpec((state.shape[0], CONV_W - 1, tn), lambda i, j: (0, 0, j))]
    args = [h, wa, wg, wdw, bdw, state]
    alias = {}
    if u_buf is not None:
        in_specs.append(pl.BlockSpec(memory_space=pl.ANY))
        args.append(u_buf)
        alias = {6: 0}
    return pl.pallas_call(
        functools.partial(_ffn_up_kernel, seg=seg, nseg=nseg, blocks_per_seq=blocks_per_seq,
                          use_state=use_state, has_buf=u_buf is not None),
        grid=(ni, nj),
        in_specs=in_specs,
        out_specs=[pl.BlockSpec((tm, tn), lambda i, j: (r0 + i, j)),
                   pl.BlockSpec((nseg, 8, tn), lambda i, j: (i, 0, j))],
        out_shape=[jax.ShapeDtypeStruct((t, ff), BF16),
                   jax.ShapeDtypeStruct((ni * nseg, 8, ff), F32)],
        scratch_shapes=scratch,
        input_output_aliases=alias,
        compiler_params=_cp("arbitrary", "arbitrary"),
        name=name,
    )(*args)


def _rope_tables(pos):
    half = 32
    inv = ROPE_THETA ** (-jnp.arange(half, dtype=F32) / half)
    ang = pos.astype(F32)[:, None] * inv[None, :]
    c, s, z = jnp.cos(ang), jnp.sin(ang), jnp.zeros_like(ang)
    return jnp.concatenate([c, z, c, z], axis=1), jnp.concatenate([-s, z, s, z], axis=1)


def _slot_cols(w, half):
    z = jnp.zeros(w.shape[:-1] + (LANES // 2 - half,), w.dtype)
    return jnp.concatenate([w[..., :half], z, w[..., half:], z], axis=-1)


def kernel(x_prompt, x_sample, mem_prompt, cache_mla_ckv, cache_mla_krope, cache_band_k, cache_band_v, cache_mem_k, cache_mem_v, state_conv, norm_mix, w_in, norm_cq, norm_ckv, w_uq, w_uk, w_uv, rel_bias, g_out_a, g_out_b, w_o, norm_mem, norm_memtok, w_mq, w_mkv, w_mo, norm_ffn, w_up, w_dw, b_dw, w_down, norm_final):
    batch, seq, d = x_prompt.shape
    dbatch, dseq, _ = x_sample.shape
    depth = norm_mix.shape[0]
    past = cache_mla_ckv.shape[2]
    nband = cache_band_k.shape[2]
    ql, kvl = norm_cq.shape[1], norm_ckv.shape[1]
    rope = cache_mla_krope.shape[3]
    half = rope // 2
    a_heads, a_nope = w_uk.shape[2], w_uk.shape[3]
    a_vdim = w_uv.shape[3]
    b_heads, b_dim = cache_band_k.shape[3], cache_band_k.shape[4]
    mtok, m_heads, m_dim = cache_mem_k.shape[2], cache_mem_k.shape[3], cache_mem_k.shape[4]
    mem_w = m_heads * m_dim
    ff = b_dw.shape[1]
    assert a_nope == LANES and a_vdim == LANES and b_dim == LANES and rope == LANES // 2
    assert seq % CHUNK == 0 and dseq == CHUNK and past % CHUNK == 0

    tp, ts = batch * seq, dbatch * dseq
    t = tp + ts
    band_keep = min(BAND_PREV * CHUNK, seq)
    mla_scale = (a_nope + rope) ** -0.5
    b_scale = b_dim ** -0.5

    pos = jnp.concatenate([jnp.tile(jnp.arange(seq, dtype=jnp.int32), batch),
                           jnp.tile(past + jnp.arange(dseq, dtype=jnp.int32), dbatch)])
    cos_t, sin_t = _rope_tables(pos)
    band_tq = _pick(seq, (256, 128, 64))
    nprev = (BAND_PREV * CHUNK) // band_tq
    assert nprev * band_tq == BAND_PREV * CHUNK and nprev <= 2
    perm = _slot_cols(jnp.eye(rope, dtype=BF16), half)
    tk_down = _pick(ff, (5504, 2816, 2048, 1024, 512, 256))
    ffn_tm = _pick(math.gcd(seq, 1024), (1024, 512, 256, 128))
    mla_t = _pick(seq, (512, 256, 128))
    mla_nh = 2 if a_heads % 2 == 0 else 1
    cos_tt, sin_tt = cos_t[:tp].T, sin_t[:tp].T

    x = jnp.concatenate([x_prompt.reshape(tp, d), x_sample.reshape(ts, d)], axis=0)
    outs = {k: [] for k in ("p_ckv", "p_kr", "p_bk", "p_bv", "p_mk", "p_mv", "p_conv",
                            "s_ckv", "s_kr", "s_bk", "s_bv", "s_conv")}

    def unslot(kr):
        return jnp.concatenate([kr[:, :half], kr[:, LANES // 2:LANES // 2 + half]], axis=1)

    for l in range(depth):
        wi = w_in[l]
        w_lat = jnp.concatenate([wi[:, :ql + kvl], _slot_cols(wi[:, ql + kvl:ql + kvl + rope], half)],
                                axis=1).astype(BF16)
        w_qkvb = wi[:, ql + kvl + rope:].astype(BF16)
        wq = w_uq[l].reshape(ql, a_heads, a_nope + rope)
        wq = jnp.concatenate([wq[..., :a_nope], _slot_cols(wq[..., a_nope:], half)], axis=-1)
        wq = wq.reshape(ql, a_heads * HEAD_SLOT).astype(BF16)
        wq_t = wq.T
        wuk = w_uk[l].reshape(kvl, a_heads * a_nope).astype(BF16)
        wuv = w_uv[l].reshape(kvl, a_heads * a_vdim).astype(BF16)
        wuv_t = wuv.T
        wo = w_o[l].astype(BF16)
        wmq, wmkv, wmo = w_mq[l].astype(BF16), w_mkv[l].astype(BF16), w_mo[l].astype(BF16)
        wa = w_up[l][:, :ff].astype(BF16)
        wg = w_up[l][:, ff:].astype(BF16)
        wdw = w_dw[l]
        bdw = b_dw[l].reshape(1, ff)
        wdn = w_down[l].astype(BF16)
        st = state_conv[l]

        h = rmsnorm(x, norm_mix[l], BF16, name="norm_mix")
        cq, ckv, ckv_b, kr, kr_b = latent_project(h, w_lat, norm_cq[l], norm_ckv[l], cos_t, sin_t)
        qkvb = matmul(h, w_qkvb, BF16, name="qkv_band")
        sel = jnp.concatenate([h[b * seq + seq - band_keep:(b + 1) * seq] for b in range(batch)]
                              + [h[tp:]], axis=0)
        kv_keep = matmul(sel, w_qkvb[:, b_heads * b_dim:], F32, name="kv_band_keep")
        q_t = q_project_t(cq, wq_t, cos_tt, sin_tt, mla_scale * LOG2E, rows=tp, tq=mla_t)
        q_s = q_project(cq, wq, cos_t, sin_t, mla_scale * LOG2E, row0=tp, rows=ts)
        k, v, v_t = kv_decompress(ckv_b, kr_b, wuk, wuv, perm, permute=False, tm=mla_t, w_uv_t=wuv_t)
        kc, vc = kv_decompress(cache_mla_ckv[l].reshape(dbatch * past, kvl),
                               cache_mla_krope[l].reshape(dbatch * past, rope),
                               wuk, wuv, perm, permute=True, tm=_pick(dbatch * past, (512, 256, 128)))
        oa = mla_prompt(q_t, k, v_t, batch, seq, a_heads, mla_t, mla_nh)
        oa = mla_sample(q_s, kc, vc, k, v, oa, dbatch, dseq, past, a_heads, tp)
        bias_p = _band_bias_table(rel_bias[l], band_tq, (nprev + 1) * band_tq, nprev * band_tq)
        bias_s = _band_bias_table(rel_bias[l], dseq, nband + dseq, nband)
        ob = band_prompt(qkvb, bias_p, batch, seq, b_heads, band_tq, nprev, b_scale)
        ob = band_sample(qkvb, cache_band_k[l].reshape(dbatch * nband, b_heads * b_dim),
                         cache_band_v[l].reshape(dbatch * nband, b_heads * b_dim),
                         bias_s, ob, dbatch, dseq, nband, b_heads, tp, b_scale)
        x = matmul_residual(merge_norm(oa, ob, g_out_a[l], g_out_b[l]), wo, x, name="out_proj")

        memn = rmsnorm(mem_prompt.reshape(batch * mtok, d), norm_memtok[l], BF16, name="norm_memtok")
        mkv = matmul(memn, wmkv, F32, name="mem_kv")
        h = rmsnorm(x, norm_mem[l], BF16, name="norm_mem")
        qm = matmul(h, wmq, BF16, name="mem_q")
        om = mem_attend(qm, mkv, mkv, 0, 1, None, nbatch=batch, rows_per_batch=seq, row0=0,
                        heads=m_heads, dim=m_dim, name="mem_prompt")
        cmk = cache_mem_k[l].reshape(dbatch * mtok, mem_w)
        cmv = cache_mem_v[l].reshape(dbatch * mtok, mem_w)
        om = mem_attend(qm, cmk, cmv, 0, 0, om, nbatch=dbatch, rows_per_batch=dseq, row0=tp,
                        heads=m_heads, dim=m_dim, name="mem_sample")
        x = matmul_residual(om, wmo, x, name="mem_out")

        h = rmsnorm(x, norm_ffn[l], BF16, name="norm_ffn")
        u, tail_p = ffn_up(h, wa, wg, wdw, bdw, st, None, row0=0, rows=tp, seg=ffn_tm,
                           blocks_per_seq=seq // ffn_tm, use_state=False, name="ffn_up_prompt")
        u, tail_s = ffn_up(h, wa, wg, wdw, bdw, st, u, row0=tp, rows=ts, seg=dseq,
                           blocks_per_seq=1, use_state=True, name="ffn_up_sample")
        x = matmul_residual_ktiled(u, wdn, x, tk=tk_down, name="ffn_down")

        outs["p_ckv"].append(ckv[:tp].reshape(batch, seq, kvl))
        outs["s_ckv"].append(ckv[tp:].reshape(dbatch, dseq, kvl))
        kr64 = unslot(kr)
        outs["p_kr"].append(kr64[:tp].reshape(batch, seq, rope))
        outs["s_kr"].append(kr64[tp:].reshape(dbatch, dseq, rope))
        hw = b_heads * b_dim
        nkp = batch * band_keep
        outs["p_bk"].append(kv_keep[:nkp, :hw].reshape(batch, band_keep, b_heads, b_dim))
        outs["p_bv"].append(kv_keep[:nkp, hw:].reshape(batch, band_keep, b_heads, b_dim))
        outs["s_bk"].append(kv_keep[nkp:, :hw].reshape(dbatch, dseq, b_heads, b_dim))
        outs["s_bv"].append(kv_keep[nkp:, hw:].reshape(dbatch, dseq, b_heads, b_dim))
        outs["p_mk"].append(mkv[:, :mem_w].reshape(batch, mtok, m_heads, m_dim))
        outs["p_mv"].append(mkv[:, mem_w:].reshape(batch, mtok, m_heads, m_dim))
        nblk = seq // ffn_tm
        tail_p = tail_p.reshape(batch, nblk, 8, ff)[:, nblk - 1, 8 - (CONV_W - 1):]
        outs["p_conv"].append(tail_p)
        outs["s_conv"].append(tail_s[:, 8 - (CONV_W - 1):])

    y_prompt = rmsnorm(x, norm_final, F32, row0=0, rows=tp, name="norm_final_p").reshape(batch, seq, d)
    y_sample = rmsnorm(x, norm_final, F32, row0=tp, rows=ts, name="norm_final_s").reshape(dbatch, dseq, d)
    st = {k_: jnp.stack(v_) for k_, v_ in outs.items()}
    return (y_prompt, y_sample, st["p_ckv"], st["p_kr"], st["p_bk"], st["p_bv"], st["p_mk"],
            st["p_mv"], st["p_conv"], st["s_ckv"], st["s_kr"], st["s_bk"], st["s_bv"], st["s_conv"])
```

```python
import functools
import math

import jax
import jax.numpy as jnp
import numpy as np
from jax import lax
from jax.experimental import pallas as pl
from jax.experimental.pallas import tpu as pltpu

CHUNK = 64
BAND_PREV = 8
MAX_REL = 128
CONV_W = 3
ROPE_THETA = 10000.0
EPS = 1e-6
NEG_INF = -1e30
LOG2E = math.log2(math.e)

LANES = 128
HEAD_SLOT = 256
VMEM_LIMIT = 56 * 1024 * 1024
FFN_CHUNK = 256

F32 = jnp.float32
BF16 = jnp.bfloat16


def _cp(*sem):
    return pltpu.CompilerParams(dimension_semantics=sem, vmem_limit_bytes=VMEM_LIMIT)


def _pick(n, prefs):
    for p in prefs:
        if n % p == 0:
            return p
    return n


def _round_up(n, m):
    return (n + m - 1) // m * m


def _rmsnorm_rows(x, g):
    ms = jnp.mean(x * x, axis=-1, keepdims=True)
    return (x * lax.rsqrt(ms + EPS)) * g


def _rmsnorm_kernel(x_ref, g_ref, o_ref):
    o_ref[...] = _rmsnorm_rows(x_ref[...].astype(F32), g_ref[...]).astype(o_ref.dtype)


def rmsnorm(x, g, out_dtype, *, row0=0, rows=None, name="rmsnorm"):
    m, d = x.shape
    rows = m if rows is None else rows
    tm = _pick(math.gcd(rows, row0) if row0 else rows, (256, 128, 64, 32, 16, 8))
    off = row0 // tm
    return pl.pallas_call(
        _rmsnorm_kernel,
        grid=(rows // tm,),
        in_specs=[pl.BlockSpec((tm, d), lambda i: (i + off, 0)),
                  pl.BlockSpec((1, d), lambda i: (0, 0))],
        out_specs=pl.BlockSpec((tm, d), lambda i: (i, 0)),
        out_shape=jax.ShapeDtypeStruct((rows, d), out_dtype),
        compiler_params=_cp("parallel"),
        name=name,
    )(x, g.reshape(1, d).astype(F32))


def _mm_kernel(x_ref, w_ref, o_ref):
    o_ref[...] = jnp.dot(x_ref[...].astype(BF16), w_ref[...],
                         preferred_element_type=F32).astype(o_ref.dtype)


def _wspec(w, layer, block, index_map):
    if w.ndim == 2:
        return pl.BlockSpec(block, index_map)
    return pl.BlockSpec((None,) + block, lambda *a: (layer,) + index_map(*a))


def matmul(x, w, out_dtype, *, layer=0, tm_prefs=(1024, 512, 256, 128, 64, 32, 16, 8),
           tn_prefs=(512, 256, 128), name="matmul"):
    m, k = x.shape
    n = w.shape[-1]
    tm, tn = _pick(m, tm_prefs), _pick(n, tn_prefs)
    return pl.pallas_call(
        _mm_kernel,
        grid=(m // tm, n // tn),
        in_specs=[pl.BlockSpec((tm, k), lambda i, j: (i, 0)),
                  _wspec(w, layer, (k, tn), lambda i, j: (0, j))],
        out_specs=pl.BlockSpec((tm, tn), lambda i, j: (i, j)),
        out_shape=jax.ShapeDtypeStruct((m, n), out_dtype),
        compiler_params=_cp("parallel", "parallel"),
        name=name,
    )(x, w)


def _mm_res_kernel(x_ref, w_ref, r_ref, o_ref):
    o_ref[...] = r_ref[...] + jnp.dot(x_ref[...], w_ref[...], preferred_element_type=F32)


def matmul_residual(x, w, res, *, layer=0, name="matmul_res"):
    m, k = x.shape
    n = w.shape[-1]
    tm, tn = _pick(m, (1024, 512, 256, 128)), _pick(n, (512, 256, 128))
    return pl.pallas_call(
        _mm_res_kernel,
        grid=(m // tm, n // tn),
        in_specs=[pl.BlockSpec((tm, k), lambda i, j: (i, 0)),
                  _wspec(w, layer, (k, tn), lambda i, j: (0, j)),
                  pl.BlockSpec((tm, tn), lambda i, j: (i, j))],
        out_specs=pl.BlockSpec((tm, tn), lambda i, j: (i, j)),
        out_shape=jax.ShapeDtypeStruct((m, n), F32),
        input_output_aliases={2: 0},
        compiler_params=_cp("parallel", "parallel"),
        name=name,
    )(x, w, res)


def _mm_res_ktiled_kernel(x_ref, w_ref, r_ref, o_ref, acc_ref):
    kk = pl.program_id(2)

    @pl.when(kk == 0)
    def _():
        acc_ref[...] = jnp.zeros_like(acc_ref)

    acc_ref[...] += jnp.dot(x_ref[...], w_ref[...], preferred_element_type=F32)

    @pl.when(kk == pl.num_programs(2) - 1)
    def _():
        o_ref[...] = r_ref[...] + acc_ref[...]


def matmul_residual_ktiled(x, w, res, *, tk, layer=0, name="matmul_res_k"):
    m, k = x.shape
    n = w.shape[-1]
    tm, tn = _pick(m, (1024, 512, 256, 128)), _pick(n, (512, 256, 128))
    return pl.pallas_call(
        _mm_res_ktiled_kernel,
        grid=(m // tm, n // tn, k // tk),
        in_specs=[pl.BlockSpec((tm, tk), lambda i, j, kk: (i, kk)),
                  _wspec(w, layer, (tk, tn), lambda i, j, kk: (kk, j)),
                  pl.BlockSpec((tm, tn), lambda i, j, kk: (i, j))],
        out_specs=pl.BlockSpec((tm, tn), lambda i, j, kk: (i, j)),
        out_shape=jax.ShapeDtypeStruct((m, n), F32),
        scratch_shapes=[pltpu.VMEM((tm, tn), F32)],
        input_output_aliases={2: 0},
        compiler_params=_cp("parallel", "parallel", "arbitrary"),
        name=name,
    )(x, w, res)


def _rope_slot(r, cos, sin):
    return r * cos + pltpu.roll(r, 64, 1) * sin


def _lat_kernel(h_ref, w_ref, gq_ref, gkv_ref, cos_ref, sin_ref,
                cq_ref, ckv_ref, ckvb_ref, kr_ref, krb_ref, *, ql, kvl):
    acc = jnp.dot(h_ref[...], w_ref[...], preferred_element_type=F32)
    cq_ref[...] = _rmsnorm_rows(acc[:, :ql], gq_ref[...]).astype(BF16)
    ckv = _rmsnorm_rows(acc[:, ql:ql + kvl], gkv_ref[...])
    ckv_ref[...] = ckv
    ckvb_ref[...] = ckv.astype(BF16)
    kr = _rope_slot(acc[:, ql + kvl:], cos_ref[...], sin_ref[...])
    kr_ref[...] = kr
    krb_ref[...] = kr.astype(BF16)


def latent_project(h, w_lat, g_cq, g_ckv, cos_t, sin_t):
    m, d = h.shape
    ql, kvl = g_cq.shape[0], g_ckv.shape[0]
    n = w_lat.shape[1]
    tm = _pick(m, (512, 256, 128))
    row = lambda i: (i, 0)
    fix = lambda i: (0, 0)
    return pl.pallas_call(
        functools.partial(_lat_kernel, ql=ql, kvl=kvl),
        grid=(m // tm,),
        in_specs=[pl.BlockSpec((tm, d), row), pl.BlockSpec((d, n), fix),
                  pl.BlockSpec((1, ql), fix), pl.BlockSpec((1, kvl), fix),
                  pl.BlockSpec((tm, LANES), row), pl.BlockSpec((tm, LANES), row)],
        out_specs=[pl.BlockSpec((tm, ql), row), pl.BlockSpec((tm, kvl), row),
                   pl.BlockSpec((tm, kvl), row), pl.BlockSpec((tm, LANES), row),
                   pl.BlockSpec((tm, LANES), row)],
        out_shape=[jax.ShapeDtypeStruct((m, ql), BF16), jax.ShapeDtypeStruct((m, kvl), F32),
                   jax.ShapeDtypeStruct((m, kvl), BF16), jax.ShapeDtypeStruct((m, LANES), F32),
                   jax.ShapeDtypeStruct((m, LANES), BF16)],
        compiler_params=_cp("parallel"),
        name="latent_project",
    )(h, w_lat, g_cq.reshape(1, ql), g_ckv.reshape(1, kvl), cos_t, sin_t)


def _q_kernel(c_ref, w_ref, cos_ref, sin_ref, o_ref, *, heads, scale):
    acc = jnp.dot(c_ref[...], w_ref[...], preferred_element_type=F32)
    cos, sin = cos_ref[...], sin_ref[...]
    for h in range(heads):
        lo = h * HEAD_SLOT
        o_ref[:, lo:lo + LANES] = (acc[:, lo:lo + LANES] * scale).astype(BF16)
        r = _rope_slot(acc[:, lo + LANES:lo + HEAD_SLOT], cos, sin)
        o_ref[:, lo + LANES:lo + HEAD_SLOT] = (r * scale).astype(BF16)


def q_project(cq, w_uq, cos_t, sin_t, scale, *, row0, rows):
    ql = cq.shape[1]
    n = w_uq.shape[1]
    tm = _pick(math.gcd(rows, row0), (512, 256, 128, 64))
    tn = _pick(n, (1024, 512, 256))
    r0 = row0 // tm
    return pl.pallas_call(
        functools.partial(_q_kernel, heads=tn // HEAD_SLOT, scale=scale),
        grid=(rows // tm, n // tn),
        in_specs=[pl.BlockSpec((tm, ql), lambda i, j: (r0 + i, 0)),
                  pl.BlockSpec((ql, tn), lambda i, j: (0, j)),
                  pl.BlockSpec((tm, LANES), lambda i, j: (r0 + i, 0)),
                  pl.BlockSpec((tm, LANES), lambda i, j: (r0 + i, 0))],
        out_specs=pl.BlockSpec((tm, tn), lambda i, j: (i, j)),
        out_shape=jax.ShapeDtypeStruct((rows, n), BF16),
        compiler_params=_cp("parallel", "parallel"),
        name="q_project",
    )(cq, w_uq, cos_t, sin_t)


def _qt_kernel(c_ref, wt_ref, cos_ref, sin_ref, o_ref, *, heads, scale):
    acc = lax.dot_general(wt_ref[...], c_ref[...], (((1,), (1,)), ((), ())),
                          preferred_element_type=F32)
    cos, sin = cos_ref[...], sin_ref[...]
    for h in range(heads):
        lo = h * HEAD_SLOT
        o_ref[0, lo:lo + LANES, :] = (acc[lo:lo + LANES] * scale).astype(BF16)
        r = acc[lo + LANES:lo + HEAD_SLOT]
        r = r * cos + pltpu.roll(r, 64, 0) * sin
        o_ref[0, lo + LANES:lo + HEAD_SLOT, :] = (r * scale).astype(BF16)


def q_project_t(cq, w_uq_t, cos_tt, sin_tt, scale, *, rows, tq):
    ql = cq.shape[1]
    n = w_uq_t.shape[0]
    tn = _pick(n, (1024, 512, 256))
    return pl.pallas_call(
        functools.partial(_qt_kernel, heads=tn // HEAD_SLOT, scale=scale),
        grid=(rows // tq, n // tn),
        in_specs=[pl.BlockSpec((tq, ql), lambda i, j: (i, 0)),
                  pl.BlockSpec((tn, ql), lambda i, j: (j, 0)),
                  pl.BlockSpec((LANES, tq), lambda i, j: (0, i)),
                  pl.BlockSpec((LANES, tq), lambda i, j: (0, i))],
        out_specs=pl.BlockSpec((1, tn, tq), lambda i, j: (i, j, 0)),
        out_shape=jax.ShapeDtypeStruct((rows // tq, n, tq), BF16),
        compiler_params=_cp("parallel", "parallel"),
        name="q_project_t",
    )(cq, w_uq_t, cos_tt, sin_tt)


def _kv_kernel(c_ref, kr_ref, perm_ref, wk_ref, wv_ref, *rest, heads, permute, with_vt):
    if with_vt:
        wvt_ref, k_ref, v_ref, vt_ref = rest
    else:
        k_ref, v_ref = rest
    c = c_ref[...].astype(BF16)
    kn = jnp.dot(c, wk_ref[...], preferred_element_type=F32)
    v_ref[...] = jnp.dot(c, wv_ref[...], preferred_element_type=F32).astype(BF16)
    if with_vt:
        vt_ref[0] = lax.dot_general(wvt_ref[...], c, (((1,), (1,)), ((), ())),
                                    preferred_element_type=F32).astype(BF16)
    kr = kr_ref[...].astype(BF16)
    if permute:
        kr = jnp.dot(kr, perm_ref[...], preferred_element_type=F32).astype(BF16)
    for h in range(heads):
        lo = h * HEAD_SLOT
        k_ref[:, lo:lo + LANES] = kn[:, h * LANES:(h + 1) * LANES].astype(BF16)
        k_ref[:, lo + LANES:lo + HEAD_SLOT] = kr


def kv_decompress(ckv, kr, w_uk, w_uv, perm, *, permute, tm, w_uv_t=None, layer=0):
    m, kvl = ckv.shape[-2:]
    heads = w_uk.shape[1] // LANES
    with_vt = w_uv_t is not None
    row = lambda i: (i, 0)
    fix = lambda i: (0, 0)
    in_specs = [_wspec(ckv, layer, (tm, kvl), row), _wspec(kr, layer, (tm, kr.shape[-1]), row),
                pl.BlockSpec(perm.shape, fix),
                pl.BlockSpec(w_uk.shape, fix), pl.BlockSpec(w_uv.shape, fix)]
    out_specs = [pl.BlockSpec((tm, heads * HEAD_SLOT), row), pl.BlockSpec((tm, heads * LANES), row)]
    out_shape = [jax.ShapeDtypeStruct((m, heads * HEAD_SLOT), BF16),
                 jax.ShapeDtypeStruct((m, heads * LANES), BF16)]
    args = [ckv, kr, perm, w_uk, w_uv]
    if with_vt:
        in_specs.append(pl.BlockSpec(w_uv_t.shape, fix))
        out_specs.append(pl.BlockSpec((1, heads * LANES, tm), lambda i: (i, 0, 0)))
        out_shape.append(jax.ShapeDtypeStruct((pl.cdiv(m, tm), heads * LANES, tm), BF16))
        args.append(w_uv_t)
    return pl.pallas_call(
        functools.partial(_kv_kernel, heads=heads, permute=permute, with_vt=with_vt),
        grid=(pl.cdiv(m, tm),),
        in_specs=in_specs,
        out_specs=out_specs,
        out_shape=out_shape,
        compiler_params=_cp("parallel"),
        name="kv_decompress",
    )(*args)


def _qk(q, k):
    return lax.dot_general(q, k, (((1,), (1,)), ((), ())), preferred_element_type=F32)


def _mla_prompt_kernel(qt_ref, k_ref, vt_ref, o_ref, m_ref, l_ref, acc_ref, s0_ref, s1_ref, *, t, nh):
    qi = pl.program_id(2)
    m_ref[...] = jnp.full_like(m_ref, NEG_INF)
    l_ref[...] = jnp.zeros_like(l_ref)
    acc_ref[...] = jnp.zeros_like(acc_ref)

    def scores(ki, s_ref):
        start = pl.multiple_of(ki * t, t)
        for h in range(nh):
            s_ref[h] = jnp.dot(k_ref[pl.ds(start, t), h * HEAD_SLOT:(h + 1) * HEAD_SLOT],
                               qt_ref[0, h * HEAD_SLOT:(h + 1) * HEAD_SLOT, :],
                               preferred_element_type=F32)

    def consume(ki, s_ref, mask):
        for h in range(nh):
            s = s_ref[h]
            if mask is not None:
                s = jnp.where(mask, s, NEG_INF)
            m_old = m_ref[h]
            m_new = jnp.maximum(m_old, jnp.max(s, axis=0, keepdims=True))
            alpha = jnp.exp2(m_old - m_new)
            p = jnp.exp2(s - m_new)
            l_ref[h] = alpha * l_ref[h] + jnp.sum(p, axis=0, keepdims=True)
            acc_ref[h] = alpha * acc_ref[h] + jnp.dot(
                vt_ref[ki, h * LANES:(h + 1) * LANES, :], p.astype(BF16),
                preferred_element_type=F32)
            m_ref[h] = m_new

    scores(0, s0_ref)

    def body(j, carry):
        scores(2 * j + 1, s1_ref)
        consume(2 * j, s0_ref, None)
        scores(2 * j + 2, s0_ref)
        consume(2 * j + 1, s1_ref, None)
        return carry

    lax.fori_loop(0, qi // 2, body, 0)
    kc = lax.broadcasted_iota(jnp.int32, (t, t), 0) // CHUNK
    qc = lax.broadcasted_iota(jnp.int32, (t, t), 1) // CHUNK
    mask = kc <= qc

    @pl.when(qi % 2 == 0)
    def _():
        consume(qi, s0_ref, mask)

    @pl.when(qi % 2 == 1)
    def _():
        scores(qi, s1_ref)
        consume(qi - 1, s0_ref, None)
        consume(qi, s1_ref, mask)

    for h in range(nh):
        o_ref[:, h * LANES:(h + 1) * LANES] = (acc_ref[h] / l_ref[h]).T


def mla_prompt(qt, k, vt, batch, seq, heads, t, nh):
    rows = k.shape[0]
    nq = seq // t
    return pl.pallas_call(
        functools.partial(_mla_prompt_kernel, t=t, nh=nh),
        grid=(batch, heads // nh, nq),
        in_specs=[pl.BlockSpec((1, nh * HEAD_SLOT, t), lambda b, g, i: (b * nq + i, g, 0)),
                  pl.BlockSpec((seq, nh * HEAD_SLOT), lambda b, g, i: (b, g)),
                  pl.BlockSpec((nq, nh * LANES, t), lambda b, g, i: (b, g, 0))],
        out_specs=pl.BlockSpec((t, nh * LANES), lambda b, g, i: (b * nq + i, g)),
        out_shape=jax.ShapeDtypeStruct((rows, heads * LANES), F32),
        scratch_shapes=[pltpu.VMEM((nh, 1, t), F32), pltpu.VMEM((nh, 1, t), F32),
                        pltpu.VMEM((nh, LANES, t), F32),
                        pltpu.VMEM((nh, t, t), F32), pltpu.VMEM((nh, t, t), F32)],
        compiler_params=_cp("parallel", "parallel", "arbitrary"),
        name="mla_prompt",
    )(qt, k, vt)


def _mla_sample_kernel(q_ref, kc_ref, vc_ref, kn_ref, vn_ref, oin_ref, o_ref, *, nh):
    del oin_ref
    for h in range(nh):
        qs = slice(h * HEAD_SLOT, (h + 1) * HEAD_SLOT)
        vs = slice(h * LANES, (h + 1) * LANES)
        q = q_ref[:, qs]
        s1 = _qk(q, kc_ref[:, qs])
        s2 = _qk(q, kn_ref[:, qs])
        m = jnp.maximum(jnp.max(s1, axis=-1, keepdims=True), jnp.max(s2, axis=-1, keepdims=True))
        p1 = jnp.exp2(s1 - m)
        p2 = jnp.exp2(s2 - m)
        l = jnp.sum(p1, axis=-1, keepdims=True) + jnp.sum(p2, axis=-1, keepdims=True)
        o = (jnp.dot(p1.astype(BF16), vc_ref[:, vs], preferred_element_type=F32)
             + jnp.dot(p2.astype(BF16), vn_ref[:, vs], preferred_element_type=F32))
        o_ref[:, vs] = o / l


def mla_sample(q, kc, vc, k, v, o_buf, dbatch, dseq, past, heads, row0):
    r0 = row0 // dseq
    nh = _pick(heads, (4, 2, 1))
    return pl.pallas_call(
        functools.partial(_mla_sample_kernel, nh=nh),
        grid=(dbatch, heads // nh),
        in_specs=[pl.BlockSpec((dseq, nh * HEAD_SLOT), lambda b, g: (b, g)),
                  pl.BlockSpec((past, nh * HEAD_SLOT), lambda b, g: (b, g)),
                  pl.BlockSpec((past, nh * LANES), lambda b, g: (b, g)),
                  pl.BlockSpec((dseq, nh * HEAD_SLOT), lambda b, g: (r0 + b, g)),
                  pl.BlockSpec((dseq, nh * LANES), lambda b, g: (r0 + b, g)),
                  pl.BlockSpec(memory_space=pl.ANY)],
        out_specs=pl.BlockSpec((dseq, nh * LANES), lambda b, g: (r0 + b, g)),
        out_shape=jax.ShapeDtypeStruct(o_buf.shape, F32),
        input_output_aliases={5: 0},
        compiler_params=_cp("parallel", "parallel"),
        name="mla_sample",
    )(q, kc, vc, k, v, o_buf)


def _band_prompt_kernel(q_ref, k0_ref, k1_ref, k2_ref, v0_ref, v1_ref, v2_ref, bias_ref, o_ref,
                        *, tq, nprev, scale, nh):
    qi = pl.program_id(2)
    ks = (k0_ref, k1_ref, k2_ref)[3 - nprev - 1:]
    vs = (v0_ref, v1_ref, v2_ref)[3 - nprev - 1:]
    raw = [[_qk(q_ref[:, h * LANES:(h + 1) * LANES], kr[:, h * LANES:(h + 1) * LANES]) for kr in ks]
           for h in range(nh)]
    for h in range(nh):
        hs = slice(h * LANES, (h + 1) * LANES)
        ss = []
        for d in range(len(ks)):
            s = raw[h][d] * scale + bias_ref[h, :, d * tq:(d + 1) * tq]
            if d < nprev:
                s = jnp.where(qi - (nprev - d) >= 0, s, NEG_INF)
            ss.append(s)
        m = functools.reduce(jnp.maximum, [jnp.max(s, axis=-1, keepdims=True) for s in ss])
        ps = [jnp.exp(s - m) for s in ss]
        l = functools.reduce(lambda a, b: a + b, [jnp.sum(p, axis=-1, keepdims=True) for p in ps])
        o = functools.reduce(lambda a, b: a + b,
                             [jnp.dot(p.astype(BF16), vr[:, hs], preferred_element_type=F32)
                              for p, vr in zip(ps, vs)])
        o_ref[:, hs] = o / l


def band_prompt(qkv, bias, batch, seq, heads, tq, nprev, scale):
    t = qkv.shape[0]
    nq = seq // tq
    nh = _pick(heads, (4, 2, 1))
    ng = heads // nh

    def kspec(back, col0):
        return pl.BlockSpec((tq, nh * LANES),
                            lambda b, g, i: (b * nq + jnp.maximum(i - back, 0), col0 + g))

    return pl.pallas_call(
        functools.partial(_band_prompt_kernel, tq=tq, nprev=nprev, scale=scale, nh=nh),
        grid=(batch, ng, nq),
        in_specs=[pl.BlockSpec((tq, nh * LANES), lambda b, g, i: (b * nq + i, g)),
                  kspec(2, ng), kspec(1, ng), kspec(0, ng),
                  kspec(2, 2 * ng), kspec(1, 2 * ng), kspec(0, 2 * ng),
                  pl.BlockSpec((nh, tq, (nprev + 1) * tq), lambda b, g, i: (g, 0, 0))],
        out_specs=pl.BlockSpec((tq, nh * LANES), lambda b, g, i: (b * nq + i, g)),
        out_shape=jax.ShapeDtypeStruct((t, heads * LANES), F32),
        compiler_params=_cp("parallel", "parallel", "arbitrary"),
        name="band_prompt",
    )(qkv, qkv, qkv, qkv, qkv, qkv, qkv, bias)


def _band_sample_kernel(q_ref, kc_ref, vc_ref, kn_ref, vn_ref, bias_ref, oin_ref, o_ref,
                        *, nb, scale, nh):
    del oin_ref
    for h in range(nh):
        hs = slice(h * LANES, (h + 1) * LANES)
        q = q_ref[:, hs]
        s1 = _qk(q, kc_ref[:, h, :].astype(BF16)) * scale + bias_ref[h, :, :nb]
        s2 = _qk(q, kn_ref[:, hs]) * scale + bias_ref[h, :, nb:]
        m = jnp.maximum(jnp.max(s1, axis=-1, keepdims=True), jnp.max(s2, axis=-1, keepdims=True))
        p1 = jnp.exp(s1 - m)
        p2 = jnp.exp(s2 - m)
        l = jnp.sum(p1, axis=-1, keepdims=True) + jnp.sum(p2, axis=-1, keepdims=True)
        o = (jnp.dot(p1.astype(BF16), vc_ref[:, h, :].astype(BF16), preferred_element_type=F32)
             + jnp.dot(p2.astype(BF16), vn_ref[:, hs], preferred_element_type=F32))
        o_ref[:, hs] = o / l


def band_sample(qkv, kc, vc, bias, o_buf, layer, dbatch, dseq, nb, heads, row0, scale):
    r0 = row0 // dseq
    nh, ng = heads, 1
    w = nh * LANES
    cspec = pl.BlockSpec((None, None, nb, heads, LANES), lambda b, g: (layer, b, 0, 0, 0))
    return pl.pallas_call(
        functools.partial(_band_sample_kernel, nb=nb, scale=scale, nh=nh),
        grid=(dbatch, ng),
        in_specs=[pl.BlockSpec((dseq, w), lambda b, g: (r0 + b, g)),
                  cspec, cspec,
                  pl.BlockSpec((dseq, w), lambda b, g: (r0 + b, ng + g)),
                  pl.BlockSpec((dseq, w), lambda b, g: (r0 + b, 2 * ng + g)),
                  pl.BlockSpec((nh, dseq, nb + dseq), lambda b, g: (g, 0, 0)),
                  pl.BlockSpec(memory_space=pl.ANY)],
        out_specs=pl.BlockSpec((dseq, w), lambda b, g: (r0 + b, g)),
        out_shape=jax.ShapeDtypeStruct(o_buf.shape, F32),
        input_output_aliases={6: 0},
        compiler_params=_cp("parallel", "parallel"),
        name="band_sample",
    )(qkv, kc, vc, qkv, qkv, bias, o_buf)


def _band_bias_table(rel_bias, nq, nk, q0):
    span = nq + nk - 1
    k = np.concatenate([np.arange(0, nk), np.arange(-(nq - 1), 0)])
    idx = np.clip(q0 - k, -MAX_REL, MAX_REL) + MAX_REL
    u = rel_bias.astype(F32)[:, idx]
    tab = jnp.tile(u, (1, nq))[:, :nq * (span - 1)].reshape(-1, nq, span - 1)[:, :, :nk]
    qc = (q0 + np.arange(nq))[:, None] // CHUNK
    kc = np.arange(nk)[None, :] // CHUNK
    mask = (kc <= qc) & (kc >= qc - BAND_PREV)
    return jnp.where(mask[None], tab, NEG_INF)


def _merge_kernel(a_ref, b_ref, ga_ref, gb_ref, o_ref):
    w = a_ref.shape[1]
    o_ref[:, :w] = _rmsnorm_rows(a_ref[...], ga_ref[...]).astype(BF16)
    o_ref[:, w:] = _rmsnorm_rows(b_ref[...], gb_ref[...]).astype(BF16)


def merge_norm(oa, ob, ga, gb):
    m, w = oa.shape
    tm = _pick(m, (256, 128))
    row = lambda i: (i, 0)
    fix = lambda i: (0, 0)
    return pl.pallas_call(
        _merge_kernel,
        grid=(m // tm,),
        in_specs=[pl.BlockSpec((tm, w), row), pl.BlockSpec((tm, w), row),
                  pl.BlockSpec((1, w), fix), pl.BlockSpec((1, w), fix)],
        out_specs=pl.BlockSpec((tm, 2 * w), row),
        out_shape=jax.ShapeDtypeStruct((m, 2 * w), BF16),
        compiler_params=_cp("parallel"),
        name="merge_norm",
    )(oa, ob, ga.reshape(1, w), gb.reshape(1, w))


def _mem_kernel(q_ref, k_ref, v_ref, *rest, heads, dim, scale):
    o_ref = rest[-1]
    for h in range(heads):
        sl = slice(h * dim, (h + 1) * dim)
        k = k_ref[:, sl] if len(k_ref.shape) == 2 else k_ref[:, h, :]
        v = v_ref[:, sl] if len(v_ref.shape) == 2 else v_ref[:, h, :]
        s = _qk(q_ref[:, sl], k.astype(BF16)) * scale
        m = jnp.max(s, axis=-1, keepdims=True)
        p = jnp.exp(s - m)
        l = jnp.sum(p, axis=-1, keepdims=True)
        o = jnp.dot(p.astype(BF16), v.astype(BF16), preferred_element_type=F32)
        o_ref[:, sl] = (o / l).astype(BF16)


def mem_attend(q, mk, mv, kcol, vcol, o_buf, *, nbatch, rows_per_batch, row0, heads, dim, name):
    t, w = q.shape
    tq = _pick(rows_per_batch, (512, 256, 128, 64))
    nq = rows_per_batch // tq
    r0 = row0 // tq
    if mk.ndim == 2:
        mtok = mk.shape[0] // nbatch
        kspec = pl.BlockSpec((mtok, w), lambda b, i: (b, kcol))
        vspec = pl.BlockSpec((mtok, w), lambda b, i: (b, vcol))
    else:
        kspec = vspec = pl.BlockSpec((None, None) + mk.shape[2:], lambda b, i: (kcol, b, 0, 0, 0))
    in_specs = [pl.BlockSpec((tq, w), lambda b, i: (r0 + b * nq + i, 0)), kspec, vspec]
    args = [q, mk, mv]
    alias = {}
    if o_buf is not None:
        in_specs.append(pl.BlockSpec(memory_space=pl.ANY))
        args.append(o_buf)
        alias = {3: 0}
    return pl.pallas_call(
        functools.partial(_mem_kernel, heads=heads, dim=dim, scale=dim ** -0.5),
        grid=(nbatch, nq),
        in_specs=in_specs,
        out_specs=pl.BlockSpec((tq, w), lambda b, i: (r0 + b * nq + i, 0)),
        out_shape=jax.ShapeDtypeStruct((t, w), BF16),
        input_output_aliases=alias,
        compiler_params=_cp("parallel", "parallel"),
        name=name,
    )(*args)


def _ffn_up_kernel(x_ref, *refs, seg, nseg, nch, blocks_per_seq, use_state, has_buf):
    wa_refs, wg_refs = refs[:nch], refs[nch:2 * nch]
    wdw_ref, bdw_ref, st_ref = refs[2 * nch:2 * nch + 3]
    rest = refs[2 * nch + 3 + (1 if has_buf else 0):]
    if use_state:
        u_ref, tail_ref = rest
        carry_ref = None
    else:
        u_ref, tail_ref, carry_ref = rest
    i, j = pl.program_id(0), pl.program_id(1)
    tc = FFN_CHUNK
    row = lax.broadcasted_iota(jnp.int32, (8, tc), 0)
    if not use_state:
        @pl.when(i % blocks_per_seq == 0)
        def _():
            carry_ref[j] = jnp.zeros(carry_ref.shape[1:], F32)
    for c in range(nch):
        cs = slice(c * tc, (c + 1) * tc)
        a = jnp.dot(x_ref[...], wa_refs[c][...], preferred_element_type=F32)
        g = jnp.dot(x_ref[...], wg_refs[c][...], preferred_element_type=F32)
        w0, w1, w2 = wdw_ref[0:1, cs], wdw_ref[1:2, cs], wdw_ref[2:3, cs]
        for s in range(nseg):
            gs = g[s * seg:(s + 1) * seg]
            if use_state:
                p2, p1 = st_ref[s, 0:1, cs], st_ref[s, 1:2, cs]
            else:
                p2, p1 = carry_ref[j, 6:7, cs], carry_ref[j, 7:8, cs]
            prev8 = jnp.where(row == 6, p2, jnp.where(row == 7, p1, 0.0))
            g3 = gs.reshape(seg // 8, 8, tc)
            shifted = []
            for sh in (1, 2):
                cur = pltpu.roll(g3, sh, 1)
                before = jnp.concatenate([pltpu.roll(prev8, sh, 0)[None], cur[:-1]], axis=0)
                shifted.append(jnp.where(row[None] >= sh, cur, before).reshape(seg, tc))
            gm1, gm2 = shifted
            gc = ((bdw_ref[:, cs] + w0 * gm2) + w1 * gm1) + w2 * gs
            u_ref[s * seg:(s + 1) * seg, cs] = (
                a[s * seg:(s + 1) * seg] * (gc * jax.nn.sigmoid(gc))).astype(BF16)
            tail = gs[seg - 8:seg]
            tail_ref[s, :, cs] = tail
            if not use_state:
                carry_ref[j, :, cs] = tail


def ffn_up(h, w_up, w_dw, b_dw, state, u_buf, layer, *, row0, rows, seg, blocks_per_seq, use_state,
           name):
    t, d = h.shape
    ff = w_dw.shape[2]
    nblk = ff // FFN_CHUNK
    assert nblk * FFN_CHUNK == ff
    nch = 2 if (u_buf is None and ff >= 2 * FFN_CHUNK) else 1
    tn = nch * FFN_CHUNK
    if use_state:
        tm, nseg = rows, rows // seg
    else:
        tm, nseg = seg, 1
    r0 = row0 // tm
    ni, nj = rows // tm, pl.cdiv(ff, tn)
    scratch = [] if use_state else [pltpu.VMEM((nj, 8, tn), F32)]

    def wspec(half, c):
        return pl.BlockSpec(
            (None, d, FFN_CHUNK),
            lambda i, j: (layer, 0, jnp.minimum(half * nblk + j * nch + c, 2 * nblk - 1)))

    in_specs = ([pl.BlockSpec((tm, d), lambda i, j: (r0 + i, 0))]
                + [wspec(0, c) for c in range(nch)] + [wspec(1, c) for c in range(nch)]
                + [pl.BlockSpec((None, CONV_W, tn), lambda i, j: (layer, 0, j)),
                   pl.BlockSpec((None, 1, tn), lambda i, j: (layer, 0, j)),
                   pl.BlockSpec((None, state.shape[1], CONV_W - 1, tn),
                                lambda i, j: (layer, 0, 0, j))])
    args = [h] + [w_up] * (2 * nch) + [w_dw, b_dw.reshape(b_dw.shape[0], 1, ff), state]
    alias = {}
    if u_buf is not None:
        alias = {len(args): 0}
        in_specs.append(pl.BlockSpec(memory_space=pl.ANY))
        args.append(u_buf)
    return pl.pallas_call(
        functools.partial(_ffn_up_kernel, seg=seg, nseg=nseg, nch=nch,
                          blocks_per_seq=blocks_per_seq, use_state=use_state,
                          has_buf=u_buf is not None),
        grid=(ni, nj),
        in_specs=in_specs,
        out_specs=[pl.BlockSpec((tm, tn), lambda i, j: (r0 + i, j)),
                   pl.BlockSpec((nseg, 8, tn), lambda i, j: (i, 0, j))],
        out_shape=[jax.ShapeDtypeStruct((t, ff), BF16),
                   jax.ShapeDtypeStruct((ni * nseg, 8, ff), F32)],
        scratch_shapes=scratch,
        input_output_aliases=alias,
        compiler_params=_cp("arbitrary", "arbitrary"),
        name=name,
    )(*args)


def _rope_tables(pos):
    half = 32
    inv = ROPE_THETA ** (-jnp.arange(half, dtype=F32) / half)
    ang = pos.astype(F32)[:, None] * inv[None, :]
    c, s, z = jnp.cos(ang), jnp.sin(ang), jnp.zeros_like(ang)
    return jnp.concatenate([c, z, c, z], axis=1), jnp.concatenate([-s, z, s, z], axis=1)


def _slot_cols(w, half):
    z = jnp.zeros(w.shape[:-1] + (LANES // 2 - half,), w.dtype)
    return jnp.concatenate([w[..., :half], z, w[..., half:], z], axis=-1)


def kernel(x_prompt, x_sample, mem_prompt, cache_mla_ckv, cache_mla_krope, cache_band_k, cache_band_v, cache_mem_k, cache_mem_v, state_conv, norm_mix, w_in, norm_cq, norm_ckv, w_uq, w_uk, w_uv, rel_bias, g_out_a, g_out_b, w_o, norm_mem, norm_memtok, w_mq, w_mkv, w_mo, norm_ffn, w_up, w_dw, b_dw, w_down, norm_final):
    batch, seq, d = x_prompt.shape
    dbatch, dseq, _ = x_sample.shape
    depth = norm_mix.shape[0]
    past = cache_mla_ckv.shape[2]
    nband = cache_band_k.shape[2]
    ql, kvl = norm_cq.shape[1], norm_ckv.shape[1]
    rope = cache_mla_krope.shape[3]
    half = rope // 2
    a_heads, a_nope = w_uk.shape[2], w_uk.shape[3]
    a_vdim = w_uv.shape[3]
    b_heads, b_dim = cache_band_k.shape[3], cache_band_k.shape[4]
    mtok, m_heads, m_dim = cache_mem_k.shape[2], cache_mem_k.shape[3], cache_mem_k.shape[4]
    mem_w = m_heads * m_dim
    ff = b_dw.shape[1]
    assert a_nope == LANES and a_vdim == LANES and b_dim == LANES and rope == LANES // 2
    assert seq % CHUNK == 0 and dseq == CHUNK and past % CHUNK == 0

    tp, ts = batch * seq, dbatch * dseq
    t = tp + ts
    band_keep = min(BAND_PREV * CHUNK, seq)
    mla_scale = (a_nope + rope) ** -0.5
    b_scale = b_dim ** -0.5

    pos = jnp.concatenate([jnp.tile(jnp.arange(seq, dtype=jnp.int32), batch),
                           jnp.tile(past + jnp.arange(dseq, dtype=jnp.int32), dbatch)])
    cos_t, sin_t = _rope_tables(pos)
    band_tq = _pick(seq, (256, 128, 64))
    nprev = (BAND_PREV * CHUNK) // band_tq
    assert nprev * band_tq == BAND_PREV * CHUNK and nprev <= 2
    perm = _slot_cols(jnp.eye(rope, dtype=BF16), half)
    tk_down = _pick(ff, (5504, 2816, 2048, 1024, 512, 256))
    ffn_tm = _pick(math.gcd(seq, 1024), (1024, 512, 256, 128))
    mla_t = _pick(seq, (512, 256, 128))
    mla_nh = 2 if a_heads % 2 == 0 else 1
    cos_tt, sin_tt = cos_t[:tp].T, sin_t[:tp].T

    w_up_b, w_down_b, w_o_b = w_up.astype(BF16), w_down.astype(BF16), w_o.astype(BF16)
    w_mq_b, w_mkv_b, w_mo_b = w_mq.astype(BF16), w_mkv.astype(BF16), w_mo.astype(BF16)
    c_ckv = cache_mla_ckv.reshape(depth, dbatch * past, kvl)
    c_kr = cache_mla_krope.reshape(depth, dbatch * past, rope)

    x = jnp.concatenate([x_prompt.reshape(tp, d), x_sample.reshape(ts, d)], axis=0)
    outs = {k: [] for k in ("p_ckv", "p_kr", "p_bk", "p_bv", "p_mk", "p_mv", "p_conv",
                            "s_ckv", "s_kr", "s_bk", "s_bv", "s_conv")}

    def unslot(kr):
        return jnp.concatenate([kr[:, :half], kr[:, LANES // 2:LANES // 2 + half]], axis=1)

    for l in range(depth):
        wi = w_in[l]
        w_lat = jnp.concatenate([wi[:, :ql + kvl], _slot_cols(wi[:, ql + kvl:ql + kvl + rope], half)],
                                axis=1).astype(BF16)
        w_qkvb = wi[:, ql + kvl + rope:].astype(BF16)
        wq = w_uq[l].reshape(ql, a_heads, a_nope + rope)
        wq = jnp.concatenate([wq[..., :a_nope], _slot_cols(wq[..., a_nope:], half)], axis=-1)
        wq = wq.reshape(ql, a_heads * HEAD_SLOT).astype(BF16)
        wq_t = wq.T
        wuk = w_uk[l].reshape(kvl, a_heads * a_nope).astype(BF16)
        wuv = w_uv[l].reshape(kvl, a_heads * a_vdim).astype(BF16)
        wuv_t = wuv.T

        h = rmsnorm(x, norm_mix[l], BF16, name="norm_mix")
        cq, ckv, ckv_b, kr, kr_b = latent_project(h, w_lat, norm_cq[l], norm_ckv[l], cos_t, sin_t)
        qkvb = matmul(h, w_qkvb, BF16, name="qkv_band")
        sel = jnp.concatenate([h[b * seq + seq - band_keep:(b + 1) * seq] for b in range(batch)]
                              + [h[tp:]], axis=0)
        kv_keep = matmul(sel, w_qkvb[:, b_heads * b_dim:], F32, name="kv_band_keep")
        q_t = q_project_t(cq, wq_t, cos_tt, sin_tt, mla_scale * LOG2E, rows=tp, tq=mla_t)
        q_s = q_project(cq, wq, cos_t, sin_t, mla_scale * LOG2E, row0=tp, rows=ts)
        k, v, v_t = kv_decompress(ckv_b, kr_b, wuk, wuv, perm, permute=False, tm=mla_t, w_uv_t=wuv_t)
        kc, vc = kv_decompress(c_ckv, c_kr, wuk, wuv, perm, permute=True, layer=l,
                               tm=_pick(dbatch * past, (512, 256, 128)))
        oa = mla_prompt(q_t, k, v_t, batch, seq, a_heads, mla_t, mla_nh)
        oa = mla_sample(q_s, kc, vc, k, v, oa, dbatch, dseq, past, a_heads, tp)
        bias_p = _band_bias_table(rel_bias[l], band_tq, (nprev + 1) * band_tq, nprev * band_tq)
        bias_s = _band_bias_table(rel_bias[l], dseq, nband + dseq, nband)
        ob = band_prompt(qkvb, bias_p, batch, seq, b_heads, band_tq, nprev, b_scale)
        ob = band_sample(qkvb, cache_band_k, cache_band_v, bias_s, ob, l, dbatch, dseq, nband,
                         b_heads, tp, b_scale)
        x = matmul_residual(merge_norm(oa, ob, g_out_a[l], g_out_b[l]), w_o_b, x, layer=l,
                            name="out_proj")

        memn = rmsnorm(mem_prompt.reshape(batch * mtok, d), norm_memtok[l], BF16, name="norm_memtok")
        mkv = matmul(memn, w_mkv_b, F32, layer=l, name="mem_kv")
        h = rmsnorm(x, norm_mem[l], BF16, name="norm_mem")
        qm = matmul(h, w_mq_b, BF16, layer=l, name="mem_q")
        om = mem_attend(qm, mkv, mkv, 0, 1, None, nbatch=batch, rows_per_batch=seq, row0=0,
                        heads=m_heads, dim=m_dim, name="mem_prompt")
        om = mem_attend(qm, cache_mem_k, cache_mem_v, l, l, om, nbatch=dbatch, rows_per_batch=dseq,
                        row0=tp, heads=m_heads, dim=m_dim, name="mem_sample")
        x = matmul_residual(om, w_mo_b, x, layer=l, name="mem_out")

        h = rmsnorm(x, norm_ffn[l], BF16, name="norm_ffn")
        u, tail_p = ffn_up(h, w_up_b, w_dw, b_dw, state_conv, None, l, row0=0, rows=tp, seg=ffn_tm,
                           blocks_per_seq=seq // ffn_tm, use_state=False, name="ffn_up_prompt")
        u, tail_s = ffn_up(h, w_up_b, w_dw, b_dw, state_conv, u, l, row0=tp, rows=ts, seg=dseq,
                           blocks_per_seq=1, use_state=True, name="ffn_up_sample")
        x = matmul_residual_ktiled(u, w_down_b, x, tk=tk_down, layer=l, name="ffn_down")

        outs["p_ckv"].append(ckv[:tp].reshape(batch, seq, kvl))
        outs["s_ckv"].append(ckv[tp:].reshape(dbatch, dseq, kvl))
        kr64 = unslot(kr)
        outs["p_kr"].append(kr64[:tp].reshape(batch, seq, rope))
        outs["s_kr"].append(kr64[tp:].reshape(dbatch, dseq, rope))
        hw = b_heads * b_dim
        nkp = batch * band_keep
        outs["p_bk"].append(kv_keep[:nkp, :hw].reshape(batch, band_keep, b_heads, b_dim))
        outs["p_bv"].append(kv_keep[:nkp, hw:].reshape(batch, band_keep, b_heads, b_dim))
        outs["s_bk"].append(kv_keep[nkp:, :hw].reshape(dbatch, dseq, b_heads, b_dim))
        outs["s_bv"].append(kv_keep[nkp:, hw:].reshape(dbatch, dseq, b_heads, b_dim))
        outs["p_mk"].append(mkv[:, :mem_w].reshape(batch, mtok, m_heads, m_dim))
        outs["p_mv"].append(mkv[:, mem_w:].reshape(batch, mtok, m_heads, m_dim))
        nblk = seq // ffn_tm
        tail_p = tail_p.reshape(batch, nblk, 8, ff)[:, nblk - 1, 8 - (CONV_W - 1):]
        outs["p_conv"].append(tail_p)
        outs["s_conv"].append(tail_s[:, 8 - (CONV_W - 1):])

    y_prompt = rmsnorm(x, norm_final, F32, row0=0, rows=tp, name="norm_final_p").reshape(batch, seq, d)
    y_sample = rmsnorm(x, norm_final, F32, row0=tp, rows=ts, name="norm_final_s").reshape(dbatch, dseq, d)
    st = {k_: jnp.stack(v_) for k_, v_ in outs.items()}
    return (y_prompt, y_sample, st["p_ckv"], st["p_kr"], st["p_bk"], st["p_bv"], st["p_mk"],
            st["p_mv"], st["p_conv"], st["s_ckv"], st["s_kr"], st["s_bk"], st["s_bv"], st["s_conv"])
```

```python
import functools
import math

import jax
import jax.numpy as jnp
import numpy as np
from jax import lax
from jax.experimental import pallas as pl
from jax.experimental.pallas import tpu as pltpu

CHUNK = 64
BAND_PREV = 8
MAX_REL = 128
CONV_W = 3
ROPE_THETA = 10000.0
EPS = 1e-6
NEG_INF = -1e30
LOG2E = math.log2(math.e)

LANES = 128
HEAD_SLOT = 256
VMEM_LIMIT = 56 * 1024 * 1024
FFN_CHUNK = 256
FFN_NCHUNK = 3

F32 = jnp.float32
BF16 = jnp.bfloat16


def _cp(*sem):
    return pltpu.CompilerParams(dimension_semantics=sem, vmem_limit_bytes=VMEM_LIMIT)


def _pick(n, prefs):
    for p in prefs:
        if n % p == 0:
            return p
    return n


def _round_up(n, m):
    return (n + m - 1) // m * m


def _rmsnorm_rows(x, g):
    ms = jnp.mean(x * x, axis=-1, keepdims=True)
    return (x * lax.rsqrt(ms + EPS)) * g


def _rmsnorm_kernel(x_ref, g_ref, o_ref):
    o_ref[...] = _rmsnorm_rows(x_ref[...].astype(F32), g_ref[...]).astype(o_ref.dtype)


def rmsnorm(x, g, out_dtype, *, row0=0, rows=None, name="rmsnorm"):
    m, d = x.shape
    rows = m if rows is None else rows
    tm = _pick(math.gcd(rows, row0) if row0 else rows, (256, 128, 64, 32, 16, 8))
    off = row0 // tm
    return pl.pallas_call(
        _rmsnorm_kernel,
        grid=(rows // tm,),
        in_specs=[pl.BlockSpec((tm, d), lambda i: (i + off, 0)),
                  pl.BlockSpec((1, d), lambda i: (0, 0))],
        out_specs=pl.BlockSpec((tm, d), lambda i: (i, 0)),
        out_shape=jax.ShapeDtypeStruct((rows, d), out_dtype),
        compiler_params=_cp("parallel"),
        name=name,
    )(x, g.reshape(1, d).astype(F32))


def _mm_kernel(x_ref, w_ref, o_ref):
    o_ref[...] = jnp.dot(x_ref[...].astype(BF16), w_ref[...],
                         preferred_element_type=F32).astype(o_ref.dtype)


def _wspec(w, layer, block, index_map):
    if w.ndim == 2:
        return pl.BlockSpec(block, index_map)
    return pl.BlockSpec((None,) + block, lambda *a: (layer,) + index_map(*a))


def matmul(x, w, out_dtype, *, layer=0, tm_prefs=(1024, 512, 256, 128, 64, 32, 16, 8),
           tn_prefs=(512, 256, 128), name="matmul"):
    m, k = x.shape
    n = w.shape[-1]
    tm, tn = _pick(m, tm_prefs), _pick(n, tn_prefs)
    return pl.pallas_call(
        _mm_kernel,
        grid=(m // tm, n // tn),
        in_specs=[pl.BlockSpec((tm, k), lambda i, j: (i, 0)),
                  _wspec(w, layer, (k, tn), lambda i, j: (0, j))],
        out_specs=pl.BlockSpec((tm, tn), lambda i, j: (i, j)),
        out_shape=jax.ShapeDtypeStruct((m, n), out_dtype),
        compiler_params=_cp("parallel", "parallel"),
        name=name,
    )(x, w)


def _mm_res_kernel(x_ref, w_ref, r_ref, o_ref):
    o_ref[...] = r_ref[...] + jnp.dot(x_ref[...], w_ref[...], preferred_element_type=F32)


def matmul_residual(x, w, res, *, layer=0, name="matmul_res"):
    m, k = x.shape
    n = w.shape[-1]
    tm, tn = _pick(m, (1024, 512, 256, 128)), _pick(n, (512, 256, 128))
    return pl.pallas_call(
        _mm_res_kernel,
        grid=(m // tm, n // tn),
        in_specs=[pl.BlockSpec((tm, k), lambda i, j: (i, 0)),
                  _wspec(w, layer, (k, tn), lambda i, j: (0, j)),
                  pl.BlockSpec((tm, tn), lambda i, j: (i, j))],
        out_specs=pl.BlockSpec((tm, tn), lambda i, j: (i, j)),
        out_shape=jax.ShapeDtypeStruct((m, n), F32),
        input_output_aliases={2: 0},
        compiler_params=_cp("parallel", "parallel"),
        name=name,
    )(x, w, res)


def _mm_res_ktiled_kernel(x_ref, w_ref, r_ref, o_ref, acc_ref):
    kk = pl.program_id(2)

    @pl.when(kk == 0)
    def _():
        acc_ref[...] = jnp.zeros_like(acc_ref)

    acc_ref[...] += jnp.dot(x_ref[...], w_ref[...], preferred_element_type=F32)

    @pl.when(kk == pl.num_programs(2) - 1)
    def _():
        o_ref[...] = r_ref[...] + acc_ref[...]


def matmul_residual_ktiled(x, w, res, *, tk, layer=0, name="matmul_res_k"):
    m, k = x.shape
    n = w.shape[-1]
    tm, tn = _pick(m, (1024, 512, 256, 128)), _pick(n, (512, 256, 128))
    return pl.pallas_call(
        _mm_res_ktiled_kernel,
        grid=(m // tm, n // tn, k // tk),
        in_specs=[pl.BlockSpec((tm, tk), lambda i, j, kk: (i, kk)),
                  _wspec(w, layer, (tk, tn), lambda i, j, kk: (kk, j)),
                  pl.BlockSpec((tm, tn), lambda i, j, kk: (i, j))],
        out_specs=pl.BlockSpec((tm, tn), lambda i, j, kk: (i, j)),
        out_shape=jax.ShapeDtypeStruct((m, n), F32),
        scratch_shapes=[pltpu.VMEM((tm, tn), F32)],
        input_output_aliases={2: 0},
        compiler_params=_cp("parallel", "parallel", "arbitrary"),
        name=name,
    )(x, w, res)


def _rope_slot(r, cos, sin):
    return r * cos + pltpu.roll(r, 64, 1) * sin


def _lat_kernel(h_ref, w_ref, gq_ref, gkv_ref, cos_ref, sin_ref,
                cq_ref, ckv_ref, ckvb_ref, kr_ref, krb_ref, *, ql, kvl):
    acc = jnp.dot(h_ref[...], w_ref[...], preferred_element_type=F32)
    cq_ref[...] = _rmsnorm_rows(acc[:, :ql], gq_ref[...]).astype(BF16)
    ckv = _rmsnorm_rows(acc[:, ql:ql + kvl], gkv_ref[...])
    ckv_ref[...] = ckv
    ckvb_ref[...] = ckv.astype(BF16)
    kr = _rope_slot(acc[:, ql + kvl:], cos_ref[...], sin_ref[...])
    kr_ref[...] = kr
    krb_ref[...] = kr.astype(BF16)


def latent_project(h, w_lat, g_cq, g_ckv, cos_t, sin_t):
    m, d = h.shape
    ql, kvl = g_cq.shape[0], g_ckv.shape[0]
    n = w_lat.shape[1]
    tm = _pick(m, (512, 256, 128))
    row = lambda i: (i, 0)
    fix = lambda i: (0, 0)
    return pl.pallas_call(
        functools.partial(_lat_kernel, ql=ql, kvl=kvl),
        grid=(m // tm,),
        in_specs=[pl.BlockSpec((tm, d), row), pl.BlockSpec((d, n), fix),
                  pl.BlockSpec((1, ql), fix), pl.BlockSpec((1, kvl), fix),
                  pl.BlockSpec((tm, LANES), row), pl.BlockSpec((tm, LANES), row)],
        out_specs=[pl.BlockSpec((tm, ql), row), pl.BlockSpec((tm, kvl), row),
                   pl.BlockSpec((tm, kvl), row), pl.BlockSpec((tm, LANES), row),
                   pl.BlockSpec((tm, LANES), row)],
        out_shape=[jax.ShapeDtypeStruct((m, ql), BF16), jax.ShapeDtypeStruct((m, kvl), F32),
                   jax.ShapeDtypeStruct((m, kvl), BF16), jax.ShapeDtypeStruct((m, LANES), F32),
                   jax.ShapeDtypeStruct((m, LANES), BF16)],
        compiler_params=_cp("parallel"),
        name="latent_project",
    )(h, w_lat, g_cq.reshape(1, ql), g_ckv.reshape(1, kvl), cos_t, sin_t)


def _q_kernel(c_ref, w_ref, cos_ref, sin_ref, o_ref, *, heads, scale):
    acc = jnp.dot(c_ref[...], w_ref[...], preferred_element_type=F32)
    cos, sin = cos_ref[...], sin_ref[...]
    for h in range(heads):
        lo = h * HEAD_SLOT
        o_ref[:, lo:lo + LANES] = (acc[:, lo:lo + LANES] * scale).astype(BF16)
        r = _rope_slot(acc[:, lo + LANES:lo + HEAD_SLOT], cos, sin)
        o_ref[:, lo + LANES:lo + HEAD_SLOT] = (r * scale).astype(BF16)


def q_project(cq, w_uq, cos_t, sin_t, scale, *, row0, rows):
    ql = cq.shape[1]
    n = w_uq.shape[1]
    tm = _pick(math.gcd(rows, row0), (512, 256, 128, 64))
    tn = _pick(n, (1024, 512, 256))
    r0 = row0 // tm
    return pl.pallas_call(
        functools.partial(_q_kernel, heads=tn // HEAD_SLOT, scale=scale),
        grid=(rows // tm, n // tn),
        in_specs=[pl.BlockSpec((tm, ql), lambda i, j: (r0 + i, 0)),
                  pl.BlockSpec((ql, tn), lambda i, j: (0, j)),
                  pl.BlockSpec((tm, LANES), lambda i, j: (r0 + i, 0)),
                  pl.BlockSpec((tm, LANES), lambda i, j: (r0 + i, 0))],
        out_specs=pl.BlockSpec((tm, tn), lambda i, j: (i, j)),
        out_shape=jax.ShapeDtypeStruct((rows, n), BF16),
        compiler_params=_cp("parallel", "parallel"),
        name="q_project",
    )(cq, w_uq, cos_t, sin_t)


def _qt_kernel(c_ref, wt_ref, cos_ref, sin_ref, o_ref, *, heads, scale):
    acc = lax.dot_general(wt_ref[...], c_ref[...], (((1,), (1,)), ((), ())),
                          preferred_element_type=F32)
    cos, sin = cos_ref[...], sin_ref[...]
    for h in range(heads):
        lo = h * HEAD_SLOT
        o_ref[0, lo:lo + LANES, :] = (acc[lo:lo + LANES] * scale).astype(BF16)
        r = acc[lo + LANES:lo + HEAD_SLOT]
        r = r * cos + pltpu.roll(r, 64, 0) * sin
        o_ref[0, lo + LANES:lo + HEAD_SLOT, :] = (r * scale).astype(BF16)


def q_project_t(cq, w_uq_t, cos_tt, sin_tt, scale, *, rows, tq):
    ql = cq.shape[1]
    n = w_uq_t.shape[0]
    tn = _pick(n, (1024, 512, 256))
    return pl.pallas_call(
        functools.partial(_qt_kernel, heads=tn // HEAD_SLOT, scale=scale),
        grid=(rows // tq, n // tn),
        in_specs=[pl.BlockSpec((tq, ql), lambda i, j: (i, 0)),
                  pl.BlockSpec((tn, ql), lambda i, j: (j, 0)),
                  pl.BlockSpec((LANES, tq), lambda i, j: (0, i)),
                  pl.BlockSpec((LANES, tq), lambda i, j: (0, i))],
        out_specs=pl.BlockSpec((1, tn, tq), lambda i, j: (i, j, 0)),
        out_shape=jax.ShapeDtypeStruct((rows // tq, n, tq), BF16),
        compiler_params=_cp("parallel", "parallel"),
        name="q_project_t",
    )(cq, w_uq_t, cos_tt, sin_tt)


def _kv_kernel(c_ref, kr_ref, perm_ref, wk_ref, wv_ref, *rest, heads, permute, with_vt):
    if with_vt:
        wvt_ref, k_ref, v_ref, vt_ref = rest
    else:
        k_ref, v_ref = rest
    c = c_ref[...].astype(BF16)
    kn = jnp.dot(c, wk_ref[...], preferred_element_type=F32)
    v_ref[...] = jnp.dot(c, wv_ref[...], preferred_element_type=F32).astype(BF16)
    if with_vt:
        vt_ref[0] = lax.dot_general(wvt_ref[...], c, (((1,), (1,)), ((), ())),
                                    preferred_element_type=F32).astype(BF16)
    kr = kr_ref[...].astype(BF16)
    if permute:
        kr = jnp.dot(kr, perm_ref[...], preferred_element_type=F32).astype(BF16)
    for h in range(heads):
        lo = h * HEAD_SLOT
        k_ref[:, lo:lo + LANES] = kn[:, h * LANES:(h + 1) * LANES].astype(BF16)
        k_ref[:, lo + LANES:lo + HEAD_SLOT] = kr


def kv_decompress(ckv, kr, w_uk, w_uv, perm, *, permute, tm, w_uv_t=None, layer=0):
    m, kvl = ckv.shape[-2:]
    heads = w_uk.shape[1] // LANES
    with_vt = w_uv_t is not None
    row = lambda i: (i, 0)
    fix = lambda i: (0, 0)
    in_specs = [_wspec(ckv, layer, (tm, kvl), row), _wspec(kr, layer, (tm, kr.shape[-1]), row),
                pl.BlockSpec(perm.shape, fix),
                pl.BlockSpec(w_uk.shape, fix), pl.BlockSpec(w_uv.shape, fix)]
    out_specs = [pl.BlockSpec((tm, heads * HEAD_SLOT), row), pl.BlockSpec((tm, heads * LANES), row)]
    out_shape = [jax.ShapeDtypeStruct((m, heads * HEAD_SLOT), BF16),
                 jax.ShapeDtypeStruct((m, heads * LANES), BF16)]
    args = [ckv, kr, perm, w_uk, w_uv]
    if with_vt:
        in_specs.append(pl.BlockSpec(w_uv_t.shape, fix))
        out_specs.append(pl.BlockSpec((1, heads * LANES, tm), lambda i: (i, 0, 0)))
        out_shape.append(jax.ShapeDtypeStruct((pl.cdiv(m, tm), heads * LANES, tm), BF16))
        args.append(w_uv_t)
    return pl.pallas_call(
        functools.partial(_kv_kernel, heads=heads, permute=permute, with_vt=with_vt),
        grid=(pl.cdiv(m, tm),),
        in_specs=in_specs,
        out_specs=out_specs,
        out_shape=out_shape,
        compiler_params=_cp("parallel"),
        name="kv_decompress",
    )(*args)


def _qk(q, k):
    return lax.dot_general(q, k, (((1,), (1,)), ((), ())), preferred_element_type=F32)


def _mla_prompt_kernel(qt_ref, k_ref, vt_ref, o_ref, m_ref, l_ref, acc_ref, s0_ref, s1_ref, *, t, nh):
    qi = pl.program_id(2)
    m_ref[...] = jnp.full_like(m_ref, NEG_INF)
    l_ref[...] = jnp.zeros_like(l_ref)
    acc_ref[...] = jnp.zeros_like(acc_ref)

    def scores(ki, s_ref):
        start = pl.multiple_of(ki * t, t)
        for h in range(nh):
            s_ref[h] = jnp.dot(k_ref[pl.ds(start, t), h * HEAD_SLOT:(h + 1) * HEAD_SLOT],
                               qt_ref[0, h * HEAD_SLOT:(h + 1) * HEAD_SLOT, :],
                               preferred_element_type=F32)

    def consume(ki, s_ref, mask):
        for h in range(nh):
            s = s_ref[h]
            if mask is not None:
                s = jnp.where(mask, s, NEG_INF)
            m_old = m_ref[h]
            m_new = jnp.maximum(m_old, jnp.max(s, axis=0, keepdims=True))
            alpha = jnp.exp2(m_old - m_new)
            p = jnp.exp2(s - m_new)
            l_ref[h] = alpha * l_ref[h] + jnp.sum(p, axis=0, keepdims=True)
            acc_ref[h] = alpha * acc_ref[h] + jnp.dot(
                vt_ref[ki, h * LANES:(h + 1) * LANES, :], p.astype(BF16),
                preferred_element_type=F32)
            m_ref[h] = m_new

    scores(0, s0_ref)

    def body(j, carry):
        scores(2 * j + 1, s1_ref)
        consume(2 * j, s0_ref, None)
        scores(2 * j + 2, s0_ref)
        consume(2 * j + 1, s1_ref, None)
        return carry

    lax.fori_loop(0, qi // 2, body, 0)
    kc = lax.broadcasted_iota(jnp.int32, (t, t), 0) // CHUNK
    qc = lax.broadcasted_iota(jnp.int32, (t, t), 1) // CHUNK
    mask = kc <= qc

    @pl.when(qi % 2 == 0)
    def _():
        consume(qi, s0_ref, mask)

    @pl.when(qi % 2 == 1)
    def _():
        scores(qi, s1_ref)
        consume(qi - 1, s0_ref, None)
        consume(qi, s1_ref, mask)

    for h in range(nh):
        o_ref[:, h * LANES:(h + 1) * LANES] = (acc_ref[h] / l_ref[h]).T


def mla_prompt(qt, k, vt, batch, seq, heads, t, nh):
    rows = k.shape[0]
    nq = seq // t
    return pl.pallas_call(
        functools.partial(_mla_prompt_kernel, t=t, nh=nh),
        grid=(batch, heads // nh, nq),
        in_specs=[pl.BlockSpec((1, nh * HEAD_SLOT, t), lambda b, g, i: (b * nq + i, g, 0)),
                  pl.BlockSpec((seq, nh * HEAD_SLOT), lambda b, g, i: (b, g)),
                  pl.BlockSpec((nq, nh * LANES, t), lambda b, g, i: (b, g, 0))],
        out_specs=pl.BlockSpec((t, nh * LANES), lambda b, g, i: (b * nq + i, g)),
        out_shape=jax.ShapeDtypeStruct((rows, heads * LANES), F32),
        scratch_shapes=[pltpu.VMEM((nh, 1, t), F32), pltpu.VMEM((nh, 1, t), F32),
                        pltpu.VMEM((nh, LANES, t), F32),
                        pltpu.VMEM((nh, t, t), F32), pltpu.VMEM((nh, t, t), F32)],
        compiler_params=_cp("parallel", "parallel", "arbitrary"),
        name="mla_prompt",
    )(qt, k, vt)


def _mla_sample_kernel(q_ref, kc_ref, vc_ref, kn_ref, vn_ref, oin_ref, o_ref, *, nh):
    del oin_ref
    for h in range(nh):
        qs = slice(h * HEAD_SLOT, (h + 1) * HEAD_SLOT)
        vs = slice(h * LANES, (h + 1) * LANES)
        q = q_ref[:, qs]
        s1 = _qk(q, kc_ref[:, qs])
        s2 = _qk(q, kn_ref[:, qs])
        m = jnp.maximum(jnp.max(s1, axis=-1, keepdims=True), jnp.max(s2, axis=-1, keepdims=True))
        p1 = jnp.exp2(s1 - m)
        p2 = jnp.exp2(s2 - m)
        l = jnp.sum(p1, axis=-1, keepdims=True) + jnp.sum(p2, axis=-1, keepdims=True)
        o = (jnp.dot(p1.astype(BF16), vc_ref[:, vs], preferred_element_type=F32)
             + jnp.dot(p2.astype(BF16), vn_ref[:, vs], preferred_element_type=F32))
        o_ref[:, vs] = o / l


def mla_sample(q, kc, vc, k, v, o_buf, dbatch, dseq, past, heads, row0):
    r0 = row0 // dseq
    nh = _pick(heads, (4, 2, 1))
    return pl.pallas_call(
        functools.partial(_mla_sample_kernel, nh=nh),
        grid=(dbatch, heads // nh),
        in_specs=[pl.BlockSpec((dseq, nh * HEAD_SLOT), lambda b, g: (b, g)),
                  pl.BlockSpec((past, nh * HEAD_SLOT), lambda b, g: (b, g)),
                  pl.BlockSpec((past, nh * LANES), lambda b, g: (b, g)),
                  pl.BlockSpec((dseq, nh * HEAD_SLOT), lambda b, g: (r0 + b, g)),
                  pl.BlockSpec((dseq, nh * LANES), lambda b, g: (r0 + b, g)),
                  pl.BlockSpec(memory_space=pl.ANY)],
        out_specs=pl.BlockSpec((dseq, nh * LANES), lambda b, g: (r0 + b, g)),
        out_shape=jax.ShapeDtypeStruct(o_buf.shape, F32),
        input_output_aliases={5: 0},
        compiler_params=_cp("parallel", "parallel"),
        name="mla_sample",
    )(q, kc, vc, k, v, o_buf)


def _band_prompt_kernel(q_ref, k0_ref, k1_ref, k2_ref, v0_ref, v1_ref, v2_ref, bias_ref, o_ref,
                        *, tq, nprev, scale, nh):
    qi = pl.program_id(2)
    ks = (k0_ref, k1_ref, k2_ref)[3 - nprev - 1:]
    vs = (v0_ref, v1_ref, v2_ref)[3 - nprev - 1:]
    raw = [[_qk(q_ref[:, h * LANES:(h + 1) * LANES], kr[:, h * LANES:(h + 1) * LANES]) for kr in ks]
           for h in range(nh)]
    for h in range(nh):
        hs = slice(h * LANES, (h + 1) * LANES)
        ss = []
        for d in range(len(ks)):
            s = raw[h][d] * scale + bias_ref[h, :, d * tq:(d + 1) * tq]
            if d < nprev:
                s = jnp.where(qi - (nprev - d) >= 0, s, NEG_INF)
            ss.append(s)
        m = functools.reduce(jnp.maximum, [jnp.max(s, axis=-1, keepdims=True) for s in ss])
        ps = [jnp.exp(s - m) for s in ss]
        l = functools.reduce(lambda a, b: a + b, [jnp.sum(p, axis=-1, keepdims=True) for p in ps])
        o = functools.reduce(lambda a, b: a + b,
                             [jnp.dot(p.astype(BF16), vr[:, hs], preferred_element_type=F32)
                              for p, vr in zip(ps, vs)])
        o_ref[:, hs] = o / l


def band_prompt(qkv, bias, batch, seq, heads, tq, nprev, scale):
    t = qkv.shape[0]
    nq = seq // tq
    nh = _pick(heads, (4, 2, 1))
    ng = heads // nh

    def kspec(back, col0):
        return pl.BlockSpec((tq, nh * LANES),
                            lambda b, g, i: (b * nq + jnp.maximum(i - back, 0), col0 + g))

    return pl.pallas_call(
        functools.partial(_band_prompt_kernel, tq=tq, nprev=nprev, scale=scale, nh=nh),
        grid=(batch, ng, nq),
        in_specs=[pl.BlockSpec((tq, nh * LANES), lambda b, g, i: (b * nq + i, g)),
                  kspec(2, ng), kspec(1, ng), kspec(0, ng),
                  kspec(2, 2 * ng), kspec(1, 2 * ng), kspec(0, 2 * ng),
                  pl.BlockSpec((nh, tq, (nprev + 1) * tq), lambda b, g, i: (g, 0, 0))],
        out_specs=pl.BlockSpec((tq, nh * LANES), lambda b, g, i: (b * nq + i, g)),
        out_shape=jax.ShapeDtypeStruct((t, heads * LANES), F32),
        compiler_params=_cp("parallel", "parallel", "arbitrary"),
        name="band_prompt",
    )(qkv, qkv, qkv, qkv, qkv, qkv, qkv, bias)


def _band_sample_kernel(q_ref, kc_ref, vc_ref, kn_ref, vn_ref, bias_ref, oin_ref, o_ref,
                        *, nb, scale, nh):
    del oin_ref
    for h in range(nh):
        hs = slice(h * LANES, (h + 1) * LANES)
        q = q_ref[:, hs]
        s1 = _qk(q, kc_ref[:, h, :].astype(BF16)) * scale + bias_ref[h, :, :nb]
        s2 = _qk(q, kn_ref[:, hs]) * scale + bias_ref[h, :, nb:]
        m = jnp.maximum(jnp.max(s1, axis=-1, keepdims=True), jnp.max(s2, axis=-1, keepdims=True))
        p1 = jnp.exp(s1 - m)
        p2 = jnp.exp(s2 - m)
        l = jnp.sum(p1, axis=-1, keepdims=True) + jnp.sum(p2, axis=-1, keepdims=True)
        o = (jnp.dot(p1.astype(BF16), vc_ref[:, h, :].astype(BF16), preferred_element_type=F32)
             + jnp.dot(p2.astype(BF16), vn_ref[:, hs], preferred_element_type=F32))
        o_ref[:, hs] = o / l


def band_sample(qkv, kc, vc, bias, o_buf, layer, dbatch, dseq, nb, heads, row0, scale):
    r0 = row0 // dseq
    nh, ng = heads, 1
    w = nh * LANES
    cspec = pl.BlockSpec((None, None, nb, heads, LANES), lambda b, g: (layer, b, 0, 0, 0))
    return pl.pallas_call(
        functools.partial(_band_sample_kernel, nb=nb, scale=scale, nh=nh),
        grid=(dbatch, ng),
        in_specs=[pl.BlockSpec((dseq, w), lambda b, g: (r0 + b, g)),
                  cspec, cspec,
                  pl.BlockSpec((dseq, w), lambda b, g: (r0 + b, ng + g)),
                  pl.BlockSpec((dseq, w), lambda b, g: (r0 + b, 2 * ng + g)),
                  pl.BlockSpec((nh, dseq, nb + dseq), lambda b, g: (g, 0, 0)),
                  pl.BlockSpec(memory_space=pl.ANY)],
        out_specs=pl.BlockSpec((dseq, w), lambda b, g: (r0 + b, g)),
        out_shape=jax.ShapeDtypeStruct(o_buf.shape, F32),
        input_output_aliases={6: 0},
        compiler_params=_cp("parallel", "parallel"),
        name="band_sample",
    )(qkv, kc, vc, qkv, qkv, bias, o_buf)


def _band_bias_table(rel_bias, nq, nk, q0):
    span = nq + nk - 1
    k = np.concatenate([np.arange(0, nk), np.arange(-(nq - 1), 0)])
    idx = np.clip(q0 - k, -MAX_REL, MAX_REL) + MAX_REL
    u = rel_bias.astype(F32)[:, idx]
    tab = jnp.tile(u, (1, nq))[:, :nq * (span - 1)].reshape(-1, nq, span - 1)[:, :, :nk]
    qc = (q0 + np.arange(nq))[:, None] // CHUNK
    kc = np.arange(nk)[None, :] // CHUNK
    mask = (kc <= qc) & (kc >= qc - BAND_PREV)
    return jnp.where(mask[None], tab, NEG_INF)


def _norm_mm_kernel(*refs, nparts, has_res):
    xs, gs = refs[:nparts], refs[nparts:2 * nparts]
    w_ref = refs[2 * nparts]
    r_ref = refs[2 * nparts + 1] if has_res else None
    o_ref, hs_ref = refs[-2:]

    @pl.when(pl.program_id(1) == 0)
    def _():
        off = 0
        for x_ref, g_ref in zip(xs, gs):
            hp = _rmsnorm_rows(x_ref[...], g_ref[...]).astype(BF16)
            hs_ref[:, off:off + hp.shape[1]] = hp
            off += hp.shape[1]

    acc = jnp.dot(hs_ref[...], w_ref[...], preferred_element_type=F32)
    o_ref[...] = (r_ref[...] + acc) if has_res else acc.astype(o_ref.dtype)


def norm_matmul(parts, gains, w, *, layer=0, res=None, out_dtype=F32, name):
    m = parts[0].shape[0]
    widths = [p.shape[1] for p in parts]
    k, n = sum(widths), w.shape[-1]
    tm, tn = _pick(m, (512, 256, 128)), _pick(n, (1024, 512, 256, 128))
    in_specs = ([pl.BlockSpec((tm, dp), lambda i, j: (i, 0)) for dp in widths]
                + [pl.BlockSpec((1, dp), lambda i, j: (0, 0)) for dp in widths]
                + [_wspec(w, layer, (k, tn), lambda i, j: (0, j))])
    args = list(parts) + [g.reshape(1, -1).astype(F32) for g in gains] + [w]
    alias = {}
    if res is not None:
        alias = {len(args): 0}
        in_specs.append(pl.BlockSpec((tm, tn), lambda i, j: (i, j)))
        args.append(res)
    return pl.pallas_call(
        functools.partial(_norm_mm_kernel, nparts=len(parts), has_res=res is not None),
        grid=(m // tm, n // tn),
        in_specs=in_specs,
        out_specs=pl.BlockSpec((tm, tn), lambda i, j: (i, j)),
        out_shape=jax.ShapeDtypeStruct((m, n), F32 if res is not None else out_dtype),
        scratch_shapes=[pltpu.VMEM((tm, k), BF16)],
        input_output_aliases=alias,
        compiler_params=_cp("parallel", "arbitrary"),
        name=name,
    )(*args)


def _mem_kernel(q_ref, k_ref, v_ref, *rest, heads, dim, scale):
    o_ref = rest[-1]
    for h in range(heads):
        sl = slice(h * dim, (h + 1) * dim)
        k = k_ref[:, sl] if len(k_ref.shape) == 2 else k_ref[:, h, :]
        v = v_ref[:, sl] if len(v_ref.shape) == 2 else v_ref[:, h, :]
        s = _qk(q_ref[:, sl], k.astype(BF16)) * scale
        m = jnp.max(s, axis=-1, keepdims=True)
        p = jnp.exp(s - m)
        l = jnp.sum(p, axis=-1, keepdims=True)
        o = jnp.dot(p.astype(BF16), v.astype(BF16), preferred_element_type=F32)
        o_ref[:, sl] = (o / l).astype(BF16)


def mem_attend(q, mk, mv, kcol, vcol, o_buf, *, nbatch, rows_per_batch, row0, heads, dim, name):
    t, w = q.shape
    tq = _pick(rows_per_batch, (512, 256, 128, 64))
    nq = rows_per_batch // tq
    r0 = row0 // tq
    if mk.ndim == 2:
        mtok = mk.shape[0] // nbatch
        kspec = pl.BlockSpec((mtok, w), lambda b, i: (b, kcol))
        vspec = pl.BlockSpec((mtok, w), lambda b, i: (b, vcol))
    else:
        kspec = vspec = pl.BlockSpec((None, None) + mk.shape[2:], lambda b, i: (kcol, b, 0, 0, 0))
    in_specs = [pl.BlockSpec((tq, w), lambda b, i: (r0 + b * nq + i, 0)), kspec, vspec]
    args = [q, mk, mv]
    alias = {}
    if o_buf is not None:
        in_specs.append(pl.BlockSpec(memory_space=pl.ANY))
        args.append(o_buf)
        alias = {3: 0}
    return pl.pallas_call(
        functools.partial(_mem_kernel, heads=heads, dim=dim, scale=dim ** -0.5),
        grid=(nbatch, nq),
        in_specs=in_specs,
        out_specs=pl.BlockSpec((tq, w), lambda b, i: (r0 + b * nq + i, 0)),
        out_shape=jax.ShapeDtypeStruct((t, w), BF16),
        input_output_aliases=alias,
        compiler_params=_cp("parallel", "parallel"),
        name=name,
    )(*args)


def _ffn_up_kernel(x_ref, *refs, seg, nseg, nch, blocks_per_seq, use_state, has_buf):
    wa_refs, wg_refs = refs[:nch], refs[nch:2 * nch]
    wdw_ref, bdw_ref, st_ref = refs[2 * nch:2 * nch + 3]
    rest = refs[2 * nch + 3 + (1 if has_buf else 0):]
    if use_state:
        u_ref, tail_ref = rest
        carry_ref = None
    else:
        u_ref, tail_ref, carry_ref = rest
    i, j = pl.program_id(0), pl.program_id(1)
    tc = FFN_CHUNK
    row = lax.broadcasted_iota(jnp.int32, (8, tc), 0)
    if not use_state:
        @pl.when(i % blocks_per_seq == 0)
        def _():
            carry_ref[j] = jnp.zeros(carry_ref.shape[1:], F32)
    for c in range(nch):
        cs = slice(c * tc, (c + 1) * tc)
        a = jnp.dot(x_ref[...], wa_refs[c][...], preferred_element_type=F32)
        g = jnp.dot(x_ref[...], wg_refs[c][...], preferred_element_type=F32)
        w0, w1, w2 = wdw_ref[0:1, cs], wdw_ref[1:2, cs], wdw_ref[2:3, cs]
        for s in range(nseg):
            gs = g[s * seg:(s + 1) * seg]
            if use_state:
                p2, p1 = st_ref[s, 0:1, cs], st_ref[s, 1:2, cs]
            else:
                p2, p1 = carry_ref[j, 6:7, cs], carry_ref[j, 7:8, cs]
            prev8 = jnp.where(row == 6, p2, jnp.where(row == 7, p1, 0.0))
            g3 = gs.reshape(seg // 8, 8, tc)
            shifted = []
            for sh in (1, 2):
                cur = pltpu.roll(g3, sh, 1)
                before = jnp.concatenate([pltpu.roll(prev8, sh, 0)[None], cur[:-1]], axis=0)
                shifted.append(jnp.where(row[None] >= sh, cur, before).reshape(seg, tc))
            gm1, gm2 = shifted
            gc = ((bdw_ref[:, cs] + w0 * gm2) + w1 * gm1) + w2 * gs
            u_ref[s * seg:(s + 1) * seg, cs] = (
                a[s * seg:(s + 1) * seg] * (gc * jax.nn.sigmoid(gc))).astype(BF16)
            tail = gs[seg - 8:seg]
            tail_ref[s, :, cs] = tail
            if not use_state:
                carry_ref[j, :, cs] = tail


def ffn_up(h, w_up, w_dw, b_dw, state, u_buf, layer, *, row0, rows, seg, blocks_per_seq, use_state,
           name):
    t, d = h.shape
    ff = w_dw.shape[2]
    nblk = ff // FFN_CHUNK
    assert nblk * FFN_CHUNK == ff
    nch = FFN_NCHUNK if (u_buf is None and ff >= FFN_NCHUNK * FFN_CHUNK) else 1
    tn = nch * FFN_CHUNK
    if use_state:
        tm, nseg = rows, rows // seg
    else:
        tm, nseg = seg, 1
    r0 = row0 // tm
    ni, nj = rows // tm, pl.cdiv(ff, tn)
    scratch = [] if use_state else [pltpu.VMEM((nj, 8, tn), F32)]

    def wspec(half, c):
        return pl.BlockSpec(
            (None, d, FFN_CHUNK),
            lambda i, j: (layer, 0, jnp.minimum(half * nblk + j * nch + c, 2 * nblk - 1)))

    in_specs = ([pl.BlockSpec((tm, d), lambda i, j: (r0 + i, 0))]
                + [wspec(0, c) for c in range(nch)] + [wspec(1, c) for c in range(nch)]
                + [pl.BlockSpec((None, CONV_W, tn), lambda i, j: (layer, 0, j)),
                   pl.BlockSpec((None, 1, tn), lambda i, j: (layer, 0, j)),
                   pl.BlockSpec((None, state.shape[1], CONV_W - 1, tn),
                                lambda i, j: (layer, 0, 0, j))])
    args = [h] + [w_up] * (2 * nch) + [w_dw, b_dw.reshape(b_dw.shape[0], 1, ff), state]
    alias = {}
    if u_buf is not None:
        alias = {len(args): 0}
        in_specs.append(pl.BlockSpec(memory_space=pl.ANY))
        args.append(u_buf)
    return pl.pallas_call(
        functools.partial(_ffn_up_kernel, seg=seg, nseg=nseg, nch=nch,
                          blocks_per_seq=blocks_per_seq, use_state=use_state,
                          has_buf=u_buf is not None),
        grid=(ni, nj),
        in_specs=in_specs,
        out_specs=[pl.BlockSpec((tm, tn), lambda i, j: (r0 + i, j)),
                   pl.BlockSpec((nseg, 8, tn), lambda i, j: (i, 0, j))],
        out_shape=[jax.ShapeDtypeStruct((t, ff), BF16),
                   jax.ShapeDtypeStruct((ni * nseg, 8, ff), F32)],
        scratch_shapes=scratch,
        input_output_aliases=alias,
        compiler_params=_cp("arbitrary", "arbitrary"),
        name=name,
    )(*args)


def _rope_tables(pos):
    half = 32
    inv = ROPE_THETA ** (-jnp.arange(half, dtype=F32) / half)
    ang = pos.astype(F32)[:, None] * inv[None, :]
    c, s, z = jnp.cos(ang), jnp.sin(ang), jnp.zeros_like(ang)
    return jnp.concatenate([c, z, c, z], axis=1), jnp.concatenate([-s, z, s, z], axis=1)


def _slot_cols(w, half):
    z = jnp.zeros(w.shape[:-1] + (LANES // 2 - half,), w.dtype)
    return jnp.concatenate([w[..., :half], z, w[..., half:], z], axis=-1)


def kernel(x_prompt, x_sample, mem_prompt, cache_mla_ckv, cache_mla_krope, cache_band_k, cache_band_v, cache_mem_k, cache_mem_v, state_conv, norm_mix, w_in, norm_cq, norm_ckv, w_uq, w_uk, w_uv, rel_bias, g_out_a, g_out_b, w_o, norm_mem, norm_memtok, w_mq, w_mkv, w_mo, norm_ffn, w_up, w_dw, b_dw, w_down, norm_final):
    batch, seq, d = x_prompt.shape
    dbatch, dseq, _ = x_sample.shape
    depth = norm_mix.shape[0]
    past = cache_mla_ckv.shape[2]
    nband = cache_band_k.shape[2]
    ql, kvl = norm_cq.shape[1], norm_ckv.shape[1]
    rope = cache_mla_krope.shape[3]
    half = rope // 2
    a_heads, a_nope = w_uk.shape[2], w_uk.shape[3]
    a_vdim = w_uv.shape[3]
    b_heads, b_dim = cache_band_k.shape[3], cache_band_k.shape[4]
    mtok, m_heads, m_dim = cache_mem_k.shape[2], cache_mem_k.shape[3], cache_mem_k.shape[4]
    mem_w = m_heads * m_dim
    ff = b_dw.shape[1]
    assert a_nope == LANES and a_vdim == LANES and b_dim == LANES and rope == LANES // 2
    assert seq % CHUNK == 0 and dseq == CHUNK and past % CHUNK == 0

    tp, ts = batch * seq, dbatch * dseq
    t = tp + ts
    band_keep = min(BAND_PREV * CHUNK, seq)
    mla_scale = (a_nope + rope) ** -0.5
    b_scale = b_dim ** -0.5

    pos = jnp.concatenate([jnp.tile(jnp.arange(seq, dtype=jnp.int32), batch),
                           jnp.tile(past + jnp.arange(dseq, dtype=jnp.int32), dbatch)])
    cos_t, sin_t = _rope_tables(pos)
    band_tq = _pick(seq, (256, 128, 64))
    nprev = (BAND_PREV * CHUNK) // band_tq
    assert nprev * band_tq == BAND_PREV * CHUNK and nprev <= 2
    perm = _slot_cols(jnp.eye(rope, dtype=BF16), half)
    tk_down = _pick(ff, (5504, 2816, 2048, 1024, 512, 256))
    ffn_tm = _pick(math.gcd(seq, 1024), (1024, 512, 256, 128))
    mla_t = _pick(seq, (512, 256, 128))
    mla_nh = 2 if a_heads % 2 == 0 else 1
    cos_tt, sin_tt = cos_t[:tp].T, sin_t[:tp].T

    w_up_b, w_down_b, w_o_b = w_up.astype(BF16), w_down.astype(BF16), w_o.astype(BF16)
    w_mq_b, w_mkv_b, w_mo_b = w_mq.astype(BF16), w_mkv.astype(BF16), w_mo.astype(BF16)
    c_ckv = cache_mla_ckv.reshape(depth, dbatch * past, kvl)
    c_kr = cache_mla_krope.reshape(depth, dbatch * past, rope)

    x = jnp.concatenate([x_prompt.reshape(tp, d), x_sample.reshape(ts, d)], axis=0)
    outs = {k: [] for k in ("p_ckv", "p_kr", "p_bk", "p_bv", "p_mk", "p_mv", "p_conv",
                            "s_ckv", "s_kr", "s_bk", "s_bv", "s_conv")}

    def unslot(kr):
        return jnp.concatenate([kr[:, :half], kr[:, LANES // 2:LANES // 2 + half]], axis=1)

    for l in range(depth):
        wi = w_in[l]
        w_lat = jnp.concatenate([wi[:, :ql + kvl], _slot_cols(wi[:, ql + kvl:ql + kvl + rope], half)],
                                axis=1).astype(BF16)
        w_qkvb = wi[:, ql + kvl + rope:].astype(BF16)
        wq = w_uq[l].reshape(ql, a_heads, a_nope + rope)
        wq = jnp.concatenate([wq[..., :a_nope], _slot_cols(wq[..., a_nope:], half)], axis=-1)
        wq = wq.reshape(ql, a_heads * HEAD_SLOT).astype(BF16)
        wq_t = wq.T
        wuk = w_uk[l].reshape(kvl, a_heads * a_nope).astype(BF16)
        wuv = w_uv[l].reshape(kvl, a_heads * a_vdim).astype(BF16)
        wuv_t = wuv.T

        h = rmsnorm(x, norm_mix[l], BF16, name="norm_mix")
        cq, ckv, ckv_b, kr, kr_b = latent_project(h, w_lat, norm_cq[l], norm_ckv[l], cos_t, sin_t)
        qkvb = matmul(h, w_qkvb, BF16, name="qkv_band")
        sel = jnp.concatenate([h[b * seq + seq - band_keep:(b + 1) * seq] for b in range(batch)]
                              + [h[tp:]], axis=0)
        kv_keep = matmul(sel, w_qkvb[:, b_heads * b_dim:], F32, name="kv_band_keep")
        q_t = q_project_t(cq, wq_t, cos_tt, sin_tt, mla_scale * LOG2E, rows=tp, tq=mla_t)
        q_s = q_project(cq, wq, cos_t, sin_t, mla_scale * LOG2E, row0=tp, rows=ts)
        k, v, v_t = kv_decompress(ckv_b, kr_b, wuk, wuv, perm, permute=False, tm=mla_t, w_uv_t=wuv_t)
        kc, vc = kv_decompress(c_ckv, c_kr, wuk, wuv, perm, permute=True, layer=l,
                               tm=_pick(dbatch * past, (512, 256, 128)))
        oa = mla_prompt(q_t, k, v_t, batch, seq, a_heads, mla_t, mla_nh)
        oa = mla_sample(q_s, kc, vc, k, v, oa, dbatch, dseq, past, a_heads, tp)
        bias_p = _band_bias_table(rel_bias[l], band_tq, (nprev + 1) * band_tq, nprev * band_tq)
        bias_s = _band_bias_table(rel_bias[l], dseq, nband + dseq, nband)
        ob = band_prompt(qkvb, bias_p, batch, seq, b_heads, band_tq, nprev, b_scale)
        ob = band_sample(qkvb, cache_band_k, cache_band_v, bias_s, ob, l, dbatch, dseq, nband,
                         b_heads, tp, b_scale)
        x = norm_matmul([oa, ob], [g_out_a[l], g_out_b[l]], w_o_b, layer=l, res=x, name="out_proj")

        memn = rmsnorm(mem_prompt.reshape(batch * mtok, d), norm_memtok[l], BF16, name="norm_memtok")
        mkv = matmul(memn, w_mkv_b, F32, layer=l, name="mem_kv")
        qm = norm_matmul([x], [norm_mem[l]], w_mq_b, layer=l, out_dtype=BF16, name="mem_q")
        om = mem_attend(qm, mkv, mkv, 0, 1, None, nbatch=batch, rows_per_batch=seq, row0=0,
                        heads=m_heads, dim=m_dim, name="mem_prompt")
        om = mem_attend(qm, cache_mem_k, cache_mem_v, l, l, om, nbatch=dbatch, rows_per_batch=dseq,
                        row0=tp, heads=m_heads, dim=m_dim, name="mem_sample")
        x = matmul_residual(om, w_mo_b, x, layer=l, name="mem_out")

        h = rmsnorm(x, norm_ffn[l], BF16, name="norm_ffn")
        u, tail_p = ffn_up(h, w_up_b, w_dw, b_dw, state_conv, None, l, row0=0, rows=tp, seg=ffn_tm,
                           blocks_per_seq=seq // ffn_tm, use_state=False, name="ffn_up_prompt")
        u, tail_s = ffn_up(h, w_up_b, w_dw, b_dw, state_conv, u, l, row0=tp, rows=ts, seg=dseq,
                           blocks_per_seq=1, use_state=True, name="ffn_up_sample")
        x = matmul_residual_ktiled(u, w_down_b, x, tk=tk_down, layer=l, name="ffn_down")

        outs["p_ckv"].append(ckv[:tp].reshape(batch, seq, kvl))
        outs["s_ckv"].append(ckv[tp:].reshape(dbatch, dseq, kvl))
        kr64 = unslot(kr)
        outs["p_kr"].append(kr64[:tp].reshape(batch, seq, rope))
        outs["s_kr"].append(kr64[tp:].reshape(dbatch, dseq, rope))
        hw = b_heads * b_dim
        nkp = batch * band_keep
        outs["p_bk"].append(kv_keep[:nkp, :hw].reshape(batch, band_keep, b_heads, b_dim))
        outs["p_bv"].append(kv_keep[:nkp, hw:].reshape(batch, band_keep, b_heads, b_dim))
        outs["s_bk"].append(kv_keep[nkp:, :hw].reshape(dbatch, dseq, b_heads, b_dim))
        outs["s_bv"].append(kv_keep[nkp:, hw:].reshape(dbatch, dseq, b_heads, b_dim))
        outs["p_mk"].append(mkv[:, :mem_w].reshape(batch, mtok, m_heads, m_dim))
        outs["p_mv"].append(mkv[:, mem_w:].reshape(batch, mtok, m_heads, m_dim))
        nblk = seq // ffn_tm
        tail_p = tail_p.reshape(batch, nblk, 8, ff)[:, nblk - 1, 8 - (CONV_W - 1):]
        outs["p_conv"].append(tail_p)
        outs["s_conv"].append(tail_s[:, 8 - (CONV_W - 1):])

    y_prompt = rmsnorm(x, norm_final, F32, row0=0, rows=tp, name="norm_final_p").reshape(batch, seq, d)
    y_sample = rmsnorm(x, norm_final, F32, row0=tp, rows=ts, name="norm_final_s").reshape(dbatch, dseq, d)
    st = {k_: jnp.stack(v_) for k_, v_ in outs.items()}
    return (y_prompt, y_sample, st["p_ckv"], st["p_kr"], st["p_bk"], st["p_bv"], st["p_mk"],
            st["p_mv"], st["p_conv"], st["s_ckv"], st["s_kr"], st["s_bk"], st["s_bv"], st["s_conv"])
```

```python
import functools
import math

import jax
import jax.numpy as jnp
import numpy as np
from jax import lax
from jax.experimental import pallas as pl
from jax.experimental.pallas import tpu as pltpu

CHUNK = 64
BAND_PREV = 8
MAX_REL = 128
CONV_W = 3
ROPE_THETA = 10000.0
EPS = 1e-6
NEG_INF = -1e30
LOG2E = math.log2(math.e)

LANES = 128
HEAD_SLOT = 256
VMEM_LIMIT = 56 * 1024 * 1024
FFN_CHUNK = 256
FFN_NCHUNK = 2

F32 = jnp.float32
BF16 = jnp.bfloat16


def _cp(*sem):
    return pltpu.CompilerParams(dimension_semantics=sem, vmem_limit_bytes=VMEM_LIMIT)


def _pick(n, prefs):
    for p in prefs:
        if n % p == 0:
            return p
    return n


def _round_up(n, m):
    return (n + m - 1) // m * m


def _rmsnorm_rows(x, g):
    ms = jnp.mean(x * x, axis=-1, keepdims=True)
    return (x * lax.rsqrt(ms + EPS)) * g


def _rmsnorm_kernel(x_ref, g_ref, o_ref):
    o_ref[...] = _rmsnorm_rows(x_ref[...].astype(F32), g_ref[...]).astype(o_ref.dtype)


def rmsnorm(x, g, out_dtype, *, row0=0, rows=None, name="rmsnorm"):
    m, d = x.shape
    rows = m if rows is None else rows
    tm = _pick(math.gcd(rows, row0) if row0 else rows, (256, 128, 64, 32, 16, 8))
    off = row0 // tm
    return pl.pallas_call(
        _rmsnorm_kernel,
        grid=(rows // tm,),
        in_specs=[pl.BlockSpec((tm, d), lambda i: (i + off, 0)),
                  pl.BlockSpec((1, d), lambda i: (0, 0))],
        out_specs=pl.BlockSpec((tm, d), lambda i: (i, 0)),
        out_shape=jax.ShapeDtypeStruct((rows, d), out_dtype),
        compiler_params=_cp("parallel"),
        name=name,
    )(x, g.reshape(1, d).astype(F32))


def _part_specs(parts, block, col):
    specs, off = [], 0
    for p in parts:
        nb = p.shape[0] // block[0]
        specs.append(pl.BlockSpec(
            block, lambda i, *r, off=off, nb=nb: (jnp.clip(i - off, 0, nb - 1), col(*r))))
        off += nb
    return specs


def _part_rows(refs, parts_blocks, i):
    x, bound = refs[0][...], 0
    for r, nb in zip(refs[1:], parts_blocks[:-1]):
        bound += nb
        x = jnp.where(i >= bound, r[...], x)
    return x


def _rmsnorm_parts_kernel(*refs, blocks):
    g_ref, o_ref = refs[-2:]
    x = _part_rows(refs[:-2], blocks, pl.program_id(0))
    o_ref[...] = _rmsnorm_rows(x, g_ref[...]).astype(o_ref.dtype)


def rmsnorm_parts(parts, g, out_dtype, *, name):
    d = parts[0].shape[1]
    tm = _pick(math.gcd(*[p.shape[0] for p in parts]), (256, 128, 64, 32, 16, 8))
    blocks = tuple(p.shape[0] // tm for p in parts)
    return pl.pallas_call(
        functools.partial(_rmsnorm_parts_kernel, blocks=blocks),
        grid=(sum(blocks),),
        in_specs=_part_specs(parts, (tm, d), lambda: 0) + [pl.BlockSpec((1, d), lambda i: (0, 0))],
        out_specs=pl.BlockSpec((tm, d), lambda i: (i, 0)),
        out_shape=jax.ShapeDtypeStruct((sum(blocks) * tm, d), out_dtype),
        compiler_params=_cp("parallel"),
        name=name,
    )(*parts, g.reshape(1, d).astype(F32))


def _mm_kernel(x_ref, w_ref, o_ref):
    o_ref[...] = jnp.dot(x_ref[...].astype(BF16), w_ref[...],
                         preferred_element_type=F32).astype(o_ref.dtype)


def _wspec(w, layer, block, index_map):
    if w.ndim == 2:
        return pl.BlockSpec(block, index_map)
    return pl.BlockSpec((None,) + block, lambda *a: (layer,) + index_map(*a))


def matmul(x, w, out_dtype, *, layer=0, tm_prefs=(1024, 512, 256, 128, 64, 32, 16, 8),
           tn_prefs=(512, 256, 128), name="matmul"):
    m, k = x.shape
    n = w.shape[-1]
    tm, tn = _pick(m, tm_prefs), _pick(n, tn_prefs)
    return pl.pallas_call(
        _mm_kernel,
        grid=(m // tm, n // tn),
        in_specs=[pl.BlockSpec((tm, k), lambda i, j: (i, 0)),
                  _wspec(w, layer, (k, tn), lambda i, j: (0, j))],
        out_specs=pl.BlockSpec((tm, tn), lambda i, j: (i, j)),
        out_shape=jax.ShapeDtypeStruct((m, n), out_dtype),
        compiler_params=_cp("parallel", "parallel"),
        name=name,
    )(x, w)


def _mm_res_kernel(x_ref, w_ref, r_ref, o_ref):
    o_ref[...] = r_ref[...] + jnp.dot(x_ref[...], w_ref[...], preferred_element_type=F32)


def matmul_residual(x, w, res, *, layer=0, name="matmul_res"):
    m, k = x.shape
    n = w.shape[-1]
    tm, tn = _pick(m, (1024, 512, 256, 128)), _pick(n, (512, 256, 128))
    return pl.pallas_call(
        _mm_res_kernel,
        grid=(m // tm, n // tn),
        in_specs=[pl.BlockSpec((tm, k), lambda i, j: (i, 0)),
                  _wspec(w, layer, (k, tn), lambda i, j: (0, j)),
                  pl.BlockSpec((tm, tn), lambda i, j: (i, j))],
        out_specs=pl.BlockSpec((tm, tn), lambda i, j: (i, j)),
        out_shape=jax.ShapeDtypeStruct((m, n), F32),
        input_output_aliases={2: 0},
        compiler_params=_cp("parallel", "parallel"),
        name=name,
    )(x, w, res)


def _mm_res_ktiled_kernel(x_ref, w_ref, r_ref, o_ref, acc_ref):
    kk = pl.program_id(2)

    @pl.when(kk == 0)
    def _():
        acc_ref[...] = jnp.zeros_like(acc_ref)

    acc_ref[...] += jnp.dot(x_ref[...], w_ref[...], preferred_element_type=F32)

    @pl.when(kk == pl.num_programs(2) - 1)
    def _():
        o_ref[...] = r_ref[...] + acc_ref[...]


def matmul_residual_ktiled(x, w, res, *, tk, layer=0, name="matmul_res_k"):
    m, k = x.shape
    n = w.shape[-1]
    tm, tn = _pick(m, (1024, 512, 256, 128)), _pick(n, (512, 256, 128))
    return pl.pallas_call(
        _mm_res_ktiled_kernel,
        grid=(m // tm, n // tn, k // tk),
        in_specs=[pl.BlockSpec((tm, tk), lambda i, j, kk: (i, kk)),
                  _wspec(w, layer, (tk, tn), lambda i, j, kk: (kk, j)),
                  pl.BlockSpec((tm, tn), lambda i, j, kk: (i, j))],
        out_specs=pl.BlockSpec((tm, tn), lambda i, j, kk: (i, j)),
        out_shape=jax.ShapeDtypeStruct((m, n), F32),
        scratch_shapes=[pltpu.VMEM((tm, tn), F32)],
        input_output_aliases={2: 0},
        compiler_params=_cp("parallel", "parallel", "arbitrary"),
        name=name,
    )(x, w, res)


def _rope_slot(r, cos, sin):
    return r * cos + pltpu.roll(r, 64, 1) * sin


def _lat_kernel(h_ref, w_ref, gq_ref, gkv_ref, cos_ref, sin_ref,
                cq_ref, ckv_ref, ckvb_ref, kr_ref, krb_ref, *, ql, kvl):
    acc = jnp.dot(h_ref[...], w_ref[...], preferred_element_type=F32)
    cq_ref[...] = _rmsnorm_rows(acc[:, :ql], gq_ref[...]).astype(BF16)
    ckv = _rmsnorm_rows(acc[:, ql:ql + kvl], gkv_ref[...])
    ckv_ref[...] = ckv
    ckvb_ref[...] = ckv.astype(BF16)
    kr = _rope_slot(acc[:, ql + kvl:], cos_ref[...], sin_ref[...])
    kr_ref[...] = kr
    krb_ref[...] = kr.astype(BF16)


def latent_project(h, w_lat, g_cq, g_ckv, cos_t, sin_t):
    m, d = h.shape
    ql, kvl = g_cq.shape[0], g_ckv.shape[0]
    n = w_lat.shape[1]
    tm = _pick(m, (512, 256, 128))
    row = lambda i: (i, 0)
    fix = lambda i: (0, 0)
    return pl.pallas_call(
        functools.partial(_lat_kernel, ql=ql, kvl=kvl),
        grid=(m // tm,),
        in_specs=[pl.BlockSpec((tm, d), row), pl.BlockSpec((d, n), fix),
                  pl.BlockSpec((1, ql), fix), pl.BlockSpec((1, kvl), fix),
                  pl.BlockSpec((tm, LANES), row), pl.BlockSpec((tm, LANES), row)],
        out_specs=[pl.BlockSpec((tm, ql), row), pl.BlockSpec((tm, kvl), row),
                   pl.BlockSpec((tm, kvl), row), pl.BlockSpec((tm, LANES), row),
                   pl.BlockSpec((tm, LANES), row)],
        out_shape=[jax.ShapeDtypeStruct((m, ql), BF16), jax.ShapeDtypeStruct((m, kvl), F32),
                   jax.ShapeDtypeStruct((m, kvl), BF16), jax.ShapeDtypeStruct((m, LANES), F32),
                   jax.ShapeDtypeStruct((m, LANES), BF16)],
        compiler_params=_cp("parallel"),
        name="latent_project",
    )(h, w_lat, g_cq.reshape(1, ql), g_ckv.reshape(1, kvl), cos_t, sin_t)


def _q_kernel(c_ref, w_ref, cos_ref, sin_ref, o_ref, *, heads, scale):
    acc = jnp.dot(c_ref[...], w_ref[...], preferred_element_type=F32)
    cos, sin = cos_ref[...], sin_ref[...]
    for h in range(heads):
        lo = h * HEAD_SLOT
        o_ref[:, lo:lo + LANES] = (acc[:, lo:lo + LANES] * scale).astype(BF16)
        r = _rope_slot(acc[:, lo + LANES:lo + HEAD_SLOT], cos, sin)
        o_ref[:, lo + LANES:lo + HEAD_SLOT] = (r * scale).astype(BF16)


def q_project(cq, w_uq, cos_t, sin_t, scale, *, row0, rows):
    ql = cq.shape[1]
    n = w_uq.shape[1]
    tm = _pick(math.gcd(rows, row0), (512, 256, 128, 64))
    tn = _pick(n, (1024, 512, 256))
    r0 = row0 // tm
    return pl.pallas_call(
        functools.partial(_q_kernel, heads=tn // HEAD_SLOT, scale=scale),
        grid=(rows // tm, n // tn),
        in_specs=[pl.BlockSpec((tm, ql), lambda i, j: (r0 + i, 0)),
                  pl.BlockSpec((ql, tn), lambda i, j: (0, j)),
                  pl.BlockSpec((tm, LANES), lambda i, j: (r0 + i, 0)),
                  pl.BlockSpec((tm, LANES), lambda i, j: (r0 + i, 0))],
        out_specs=pl.BlockSpec((tm, tn), lambda i, j: (i, j)),
        out_shape=jax.ShapeDtypeStruct((rows, n), BF16),
        compiler_params=_cp("parallel", "parallel"),
        name="q_project",
    )(cq, w_uq, cos_t, sin_t)


def _qt_kernel(c_ref, wt_ref, cos_ref, sin_ref, o_ref, *, heads, scale):
    acc = lax.dot_general(wt_ref[...], c_ref[...], (((1,), (1,)), ((), ())),
                          preferred_element_type=F32)
    cos, sin = cos_ref[...], sin_ref[...]
    for h in range(heads):
        lo = h * HEAD_SLOT
        o_ref[0, lo:lo + LANES, :] = (acc[lo:lo + LANES] * scale).astype(BF16)
        r = acc[lo + LANES:lo + HEAD_SLOT]
        r = r * cos + pltpu.roll(r, 64, 0) * sin
        o_ref[0, lo + LANES:lo + HEAD_SLOT, :] = (r * scale).astype(BF16)


def q_project_t(cq, w_uq_t, cos_tt, sin_tt, scale, *, rows, tq):
    ql = cq.shape[1]
    n = w_uq_t.shape[0]
    tn = _pick(n, (1024, 512, 256))
    return pl.pallas_call(
        functools.partial(_qt_kernel, heads=tn // HEAD_SLOT, scale=scale),
        grid=(rows // tq, n // tn),
        in_specs=[pl.BlockSpec((tq, ql), lambda i, j: (i, 0)),
                  pl.BlockSpec((tn, ql), lambda i, j: (j, 0)),
                  pl.BlockSpec((LANES, tq), lambda i, j: (0, i)),
                  pl.BlockSpec((LANES, tq), lambda i, j: (0, i))],
        out_specs=pl.BlockSpec((1, tn, tq), lambda i, j: (i, j, 0)),
        out_shape=jax.ShapeDtypeStruct((rows // tq, n, tq), BF16),
        compiler_params=_cp("parallel", "parallel"),
        name="q_project_t",
    )(cq, w_uq_t, cos_tt, sin_tt)


def _kv_kernel(c_ref, kr_ref, perm_ref, wk_ref, wv_ref, *rest, heads, permute, with_vt):
    if with_vt:
        wvt_ref, k_ref, v_ref, vt_ref = rest
    else:
        k_ref, v_ref = rest
    c = c_ref[...].astype(BF16)
    kn = jnp.dot(c, wk_ref[...], preferred_element_type=F32)
    v_ref[...] = jnp.dot(c, wv_ref[...], preferred_element_type=F32).astype(BF16)
    if with_vt:
        vt_ref[0] = lax.dot_general(wvt_ref[...], c, (((1,), (1,)), ((), ())),
                                    preferred_element_type=F32).astype(BF16)
    kr = kr_ref[...].astype(BF16)
    if permute:
        kr = jnp.dot(kr, perm_ref[...], preferred_element_type=F32).astype(BF16)
    for h in range(heads):
        lo = h * HEAD_SLOT
        k_ref[:, lo:lo + LANES] = kn[:, h * LANES:(h + 1) * LANES].astype(BF16)
        k_ref[:, lo + LANES:lo + HEAD_SLOT] = kr


def kv_decompress(ckv, kr, w_uk, w_uv, perm, *, permute, tm, w_uv_t=None, layer=0):
    m, kvl = ckv.shape[-2:]
    heads = w_uk.shape[1] // LANES
    with_vt = w_uv_t is not None
    row = lambda i: (i, 0)
    fix = lambda i: (0, 0)
    in_specs = [_wspec(ckv, layer, (tm, kvl), row), _wspec(kr, layer, (tm, kr.shape[-1]), row),
                pl.BlockSpec(perm.shape, fix),
                pl.BlockSpec(w_uk.shape, fix), pl.BlockSpec(w_uv.shape, fix)]
    out_specs = [pl.BlockSpec((tm, heads * HEAD_SLOT), row), pl.BlockSpec((tm, heads * LANES), row)]
    out_shape = [jax.ShapeDtypeStruct((m, heads * HEAD_SLOT), BF16),
                 jax.ShapeDtypeStruct((m, heads * LANES), BF16)]
    args = [ckv, kr, perm, w_uk, w_uv]
    if with_vt:
        in_specs.append(pl.BlockSpec(w_uv_t.shape, fix))
        out_specs.append(pl.BlockSpec((1, heads * LANES, tm), lambda i: (i, 0, 0)))
        out_shape.append(jax.ShapeDtypeStruct((pl.cdiv(m, tm), heads * LANES, tm), BF16))
        args.append(w_uv_t)
    return pl.pallas_call(
        functools.partial(_kv_kernel, heads=heads, permute=permute, with_vt=with_vt),
        grid=(pl.cdiv(m, tm),),
        in_specs=in_specs,
        out_specs=out_specs,
        out_shape=out_shape,
        compiler_params=_cp("parallel"),
        name="kv_decompress",
    )(*args)


def _qk(q, k):
    return lax.dot_general(q, k, (((1,), (1,)), ((), ())), preferred_element_type=F32)


def _mla_prompt_kernel(qt_ref, k_ref, vt_ref, o_ref, m_ref, l_ref, acc_ref, s0_ref, s1_ref, *, t, nh):
    qi = pl.program_id(2)
    m_ref[...] = jnp.full_like(m_ref, NEG_INF)
    l_ref[...] = jnp.zeros_like(l_ref)
    acc_ref[...] = jnp.zeros_like(acc_ref)

    def scores(ki, s_ref):
        start = pl.multiple_of(ki * t, t)
        for h in range(nh):
            s_ref[h] = jnp.dot(k_ref[pl.ds(start, t), h * HEAD_SLOT:(h + 1) * HEAD_SLOT],
                               qt_ref[0, h * HEAD_SLOT:(h + 1) * HEAD_SLOT, :],
                               preferred_element_type=F32)

    def consume(ki, s_ref, mask):
        for h in range(nh):
            s = s_ref[h]
            if mask is not None:
                s = jnp.where(mask, s, NEG_INF)
            m_old = m_ref[h]
            m_new = jnp.maximum(m_old, jnp.max(s, axis=0, keepdims=True))
            alpha = jnp.exp2(m_old - m_new)
            p = jnp.exp2(s - m_new)
            l_ref[h] = alpha * l_ref[h] + jnp.sum(p, axis=0, keepdims=True)
            acc_ref[h] = alpha * acc_ref[h] + jnp.dot(
                vt_ref[ki, h * LANES:(h + 1) * LANES, :], p.astype(BF16),
                preferred_element_type=F32)
            m_ref[h] = m_new

    scores(0, s0_ref)

    def body(j, carry):
        scores(2 * j + 1, s1_ref)
        consume(2 * j, s0_ref, None)
        scores(2 * j + 2, s0_ref)
        consume(2 * j + 1, s1_ref, None)
        return carry

    lax.fori_loop(0, qi // 2, body, 0)
    kc = lax.broadcasted_iota(jnp.int32, (t, t), 0) // CHUNK
    qc = lax.broadcasted_iota(jnp.int32, (t, t), 1) // CHUNK
    mask = kc <= qc

    @pl.when(qi % 2 == 0)
    def _():
        consume(qi, s0_ref, mask)

    @pl.when(qi % 2 == 1)
    def _():
        scores(qi, s1_ref)
        consume(qi - 1, s0_ref, None)
        consume(qi, s1_ref, mask)

    for h in range(nh):
        o_ref[:, h * LANES:(h + 1) * LANES] = (acc_ref[h] / l_ref[h]).T


def mla_prompt(qt, k, vt, batch, seq, heads, t, nh):
    rows = k.shape[0]
    nq = seq // t
    return pl.pallas_call(
        functools.partial(_mla_prompt_kernel, t=t, nh=nh),
        grid=(batch, heads // nh, nq),
        in_specs=[pl.BlockSpec((1, nh * HEAD_SLOT, t), lambda b, g, i: (b * nq + i, g, 0)),
                  pl.BlockSpec((seq, nh * HEAD_SLOT), lambda b, g, i: (b, g)),
                  pl.BlockSpec((nq, nh * LANES, t), lambda b, g, i: (b, g, 0))],
        out_specs=pl.BlockSpec((t, nh * LANES), lambda b, g, i: (b * nq + i, g)),
        out_shape=jax.ShapeDtypeStruct((rows, heads * LANES), F32),
        scratch_shapes=[pltpu.VMEM((nh, 1, t), F32), pltpu.VMEM((nh, 1, t), F32),
                        pltpu.VMEM((nh, LANES, t), F32),
                        pltpu.VMEM((nh, t, t), F32), pltpu.VMEM((nh, t, t), F32)],
        compiler_params=_cp("parallel", "parallel", "arbitrary"),
        name="mla_prompt",
    )(qt, k, vt)


def _mla_sample_kernel(q_ref, kc_ref, vc_ref, kn_ref, vn_ref, oin_ref, o_ref, *, nh):
    del oin_ref
    for h in range(nh):
        qs = slice(h * HEAD_SLOT, (h + 1) * HEAD_SLOT)
        vs = slice(h * LANES, (h + 1) * LANES)
        q = q_ref[:, qs]
        s1 = _qk(q, kc_ref[:, qs])
        s2 = _qk(q, kn_ref[:, qs])
        m = jnp.maximum(jnp.max(s1, axis=-1, keepdims=True), jnp.max(s2, axis=-1, keepdims=True))
        p1 = jnp.exp2(s1 - m)
        p2 = jnp.exp2(s2 - m)
        l = jnp.sum(p1, axis=-1, keepdims=True) + jnp.sum(p2, axis=-1, keepdims=True)
        o = (jnp.dot(p1.astype(BF16), vc_ref[:, vs], preferred_element_type=F32)
             + jnp.dot(p2.astype(BF16), vn_ref[:, vs], preferred_element_type=F32))
        o_ref[:, vs] = o / l


def mla_sample(q, kc, vc, k, v, o_buf, dbatch, dseq, past, heads, row0):
    r0 = row0 // dseq
    nh = _pick(heads, (4, 2, 1))
    return pl.pallas_call(
        functools.partial(_mla_sample_kernel, nh=nh),
        grid=(dbatch, heads // nh),
        in_specs=[pl.BlockSpec((dseq, nh * HEAD_SLOT), lambda b, g: (b, g)),
                  pl.BlockSpec((past, nh * HEAD_SLOT), lambda b, g: (b, g)),
                  pl.BlockSpec((past, nh * LANES), lambda b, g: (b, g)),
                  pl.BlockSpec((dseq, nh * HEAD_SLOT), lambda b, g: (r0 + b, g)),
                  pl.BlockSpec((dseq, nh * LANES), lambda b, g: (r0 + b, g)),
                  pl.BlockSpec(memory_space=pl.ANY)],
        out_specs=pl.BlockSpec((dseq, nh * LANES), lambda b, g: (r0 + b, g)),
        out_shape=jax.ShapeDtypeStruct(o_buf.shape, F32),
        input_output_aliases={5: 0},
        compiler_params=_cp("parallel", "parallel"),
        name="mla_sample",
    )(q, kc, vc, k, v, o_buf)


def _band_prompt_kernel(q_ref, k0_ref, k1_ref, k2_ref, v0_ref, v1_ref, v2_ref, bias_ref, o_ref,
                        *, tq, nprev, scale, nh):
    qi = pl.program_id(2)
    ks = (k0_ref, k1_ref, k2_ref)[3 - nprev - 1:]
    vs = (v0_ref, v1_ref, v2_ref)[3 - nprev - 1:]
    raw = [[_qk(q_ref[:, h * LANES:(h + 1) * LANES], kr[:, h * LANES:(h + 1) * LANES]) for kr in ks]
           for h in range(nh)]
    for h in range(nh):
        hs = slice(h * LANES, (h + 1) * LANES)
        ss = []
        for d in range(len(ks)):
            s = raw[h][d] * scale + bias_ref[h, :, d * tq:(d + 1) * tq]
            if d < nprev:
                s = jnp.where(qi - (nprev - d) >= 0, s, NEG_INF)
            ss.append(s)
        m = functools.reduce(jnp.maximum, [jnp.max(s, axis=-1, keepdims=True) for s in ss])
        ps = [jnp.exp(s - m) for s in ss]
        l = functools.reduce(lambda a, b: a + b, [jnp.sum(p, axis=-1, keepdims=True) for p in ps])
        o = functools.reduce(lambda a, b: a + b,
                             [jnp.dot(p.astype(BF16), vr[:, hs], preferred_element_type=F32)
                              for p, vr in zip(ps, vs)])
        o_ref[:, hs] = o / l


def band_prompt(qkv, bias, batch, seq, heads, tq, nprev, scale):
    t = qkv.shape[0]
    nq = seq // tq
    nh = _pick(heads, (4, 2, 1))
    ng = heads // nh

    def kspec(back, col0):
        return pl.BlockSpec((tq, nh * LANES),
                            lambda b, g, i: (b * nq + jnp.maximum(i - back, 0), col0 + g))

    return pl.pallas_call(
        functools.partial(_band_prompt_kernel, tq=tq, nprev=nprev, scale=scale, nh=nh),
        grid=(batch, ng, nq),
        in_specs=[pl.BlockSpec((tq, nh * LANES), lambda b, g, i: (b * nq + i, g)),
                  kspec(2, ng), kspec(1, ng), kspec(0, ng),
                  kspec(2, 2 * ng), kspec(1, 2 * ng), kspec(0, 2 * ng),
                  pl.BlockSpec((nh, tq, (nprev + 1) * tq), lambda b, g, i: (g, 0, 0))],
        out_specs=pl.BlockSpec((tq, nh * LANES), lambda b, g, i: (b * nq + i, g)),
        out_shape=jax.ShapeDtypeStruct((t, heads * LANES), F32),
        compiler_params=_cp("parallel", "parallel", "arbitrary"),
        name="band_prompt",
    )(qkv, qkv, qkv, qkv, qkv, qkv, qkv, bias)


def _band_sample_kernel(q_ref, kc_ref, vc_ref, kn_ref, vn_ref, bias_ref, oin_ref, o_ref,
                        *, nb, scale, nh):
    del oin_ref
    for h in range(nh):
        hs = slice(h * LANES, (h + 1) * LANES)
        q = q_ref[:, hs]
        s1 = _qk(q, kc_ref[:, h, :].astype(BF16)) * scale + bias_ref[h, :, :nb]
        s2 = _qk(q, kn_ref[:, hs]) * scale + bias_ref[h, :, nb:]
        m = jnp.maximum(jnp.max(s1, axis=-1, keepdims=True), jnp.max(s2, axis=-1, keepdims=True))
        p1 = jnp.exp(s1 - m)
        p2 = jnp.exp(s2 - m)
        l = jnp.sum(p1, axis=-1, keepdims=True) + jnp.sum(p2, axis=-1, keepdims=True)
        o = (jnp.dot(p1.astype(BF16), vc_ref[:, h, :].astype(BF16), preferred_element_type=F32)
             + jnp.dot(p2.astype(BF16), vn_ref[:, hs], preferred_element_type=F32))
        o_ref[:, hs] = o / l


def band_sample(qkv, kc, vc, bias, o_buf, layer, dbatch, dseq, nb, heads, row0, scale):
    r0 = row0 // dseq
    nh, ng = heads, 1
    w = nh * LANES
    cspec = pl.BlockSpec((None, None, nb, heads, LANES), lambda b, g: (layer, b, 0, 0, 0))
    return pl.pallas_call(
        functools.partial(_band_sample_kernel, nb=nb, scale=scale, nh=nh),
        grid=(dbatch, ng),
        in_specs=[pl.BlockSpec((dseq, w), lambda b, g: (r0 + b, g)),
                  cspec, cspec,
                  pl.BlockSpec((dseq, w), lambda b, g: (r0 + b, ng + g)),
                  pl.BlockSpec((dseq, w), lambda b, g: (r0 + b, 2 * ng + g)),
                  pl.BlockSpec((nh, dseq, nb + dseq), lambda b, g: (g, 0, 0)),
                  pl.BlockSpec(memory_space=pl.ANY)],
        out_specs=pl.BlockSpec((dseq, w), lambda b, g: (r0 + b, g)),
        out_shape=jax.ShapeDtypeStruct(o_buf.shape, F32),
        input_output_aliases={6: 0},
        compiler_params=_cp("parallel", "parallel"),
        name="band_sample",
    )(qkv, kc, vc, qkv, qkv, bias, o_buf)


def _band_bias_table(rel_bias, nq, nk, q0):
    span = nq + nk - 1
    k = np.concatenate([np.arange(0, nk), np.arange(-(nq - 1), 0)])
    idx = np.clip(q0 - k, -MAX_REL, MAX_REL) + MAX_REL
    u = rel_bias.astype(F32)[:, idx]
    tab = jnp.tile(u, (1, nq))[:, :nq * (span - 1)].reshape(-1, nq, span - 1)[:, :, :nk]
    qc = (q0 + np.arange(nq))[:, None] // CHUNK
    kc = np.arange(nk)[None, :] // CHUNK
    mask = (kc <= qc) & (kc >= qc - BAND_PREV)
    return jnp.where(mask[None], tab, NEG_INF)


def _norm_mm_kernel(*refs, nparts, res_blocks):
    xs, gs = refs[:nparts], refs[nparts:2 * nparts]
    w_ref = refs[2 * nparts]
    r_refs = refs[2 * nparts + 1:-2]
    o_ref, hs_ref = refs[-2:]

    @pl.when(pl.program_id(1) == 0)
    def _():
        off = 0
        for x_ref, g_ref in zip(xs, gs):
            hp = _rmsnorm_rows(x_ref[...], g_ref[...]).astype(BF16)
            hs_ref[:, off:off + hp.shape[1]] = hp
            off += hp.shape[1]

    acc = jnp.dot(hs_ref[...], w_ref[...], preferred_element_type=F32)
    if r_refs:
        o_ref[...] = _part_rows(r_refs, res_blocks, pl.program_id(0)) + acc
    else:
        o_ref[...] = acc.astype(o_ref.dtype)


def norm_matmul(parts, gains, w, *, layer=0, res=(), out_dtype=F32, name):
    m = parts[0].shape[0]
    widths = [p.shape[1] for p in parts]
    k, n = sum(widths), w.shape[-1]
    tm, tn = _pick(math.gcd(m, *[r.shape[0] for r in res]), (512, 256, 128)), _pick(n, (1024, 512, 256, 128))
    in_specs = ([pl.BlockSpec((tm, dp), lambda i, j: (i, 0)) for dp in widths]
                + [pl.BlockSpec((1, dp), lambda i, j: (0, 0)) for dp in widths]
                + [_wspec(w, layer, (k, tn), lambda i, j: (0, j))]
                + _part_specs(res, (tm, tn), lambda j: j))
    args = list(parts) + [g.reshape(1, -1).astype(F32) for g in gains] + [w] + list(res)
    alias = {len(args) - 1: 0} if len(res) == 1 else {}
    return pl.pallas_call(
        functools.partial(_norm_mm_kernel, nparts=len(parts),
                          res_blocks=tuple(r.shape[0] // tm for r in res)),
        grid=(m // tm, n // tn),
        in_specs=in_specs,
        out_specs=pl.BlockSpec((tm, tn), lambda i, j: (i, j)),
        out_shape=jax.ShapeDtypeStruct((m, n), F32 if res else out_dtype),
        scratch_shapes=[pltpu.VMEM((tm, k), BF16)],
        input_output_aliases=alias,
        compiler_params=_cp("parallel", "arbitrary"),
        name=name,
    )(*args)


def _mem_kernel(q_ref, k_ref, v_ref, *rest, heads, dim, scale):
    o_ref = rest[-1]
    for h in range(heads):
        sl = slice(h * dim, (h + 1) * dim)
        k = k_ref[:, sl] if len(k_ref.shape) == 2 else k_ref[:, h, :]
        v = v_ref[:, sl] if len(v_ref.shape) == 2 else v_ref[:, h, :]
        s = _qk(q_ref[:, sl], k.astype(BF16)) * scale
        m = jnp.max(s, axis=-1, keepdims=True)
        p = jnp.exp(s - m)
        l = jnp.sum(p, axis=-1, keepdims=True)
        o = jnp.dot(p.astype(BF16), v.astype(BF16), preferred_element_type=F32)
        o_ref[:, sl] = (o / l).astype(BF16)


def mem_attend(q, mk, mv, kcol, vcol, o_buf, *, nbatch, rows_per_batch, row0, heads, dim, name):
    t, w = q.shape
    tq = _pick(rows_per_batch, (512, 256, 128, 64))
    nq = rows_per_batch // tq
    r0 = row0 // tq
    if mk.ndim == 2:
        mtok = mk.shape[0] // nbatch
        kspec = pl.BlockSpec((mtok, w), lambda b, i: (b, kcol))
        vspec = pl.BlockSpec((mtok, w), lambda b, i: (b, vcol))
    else:
        kspec = vspec = pl.BlockSpec((None, None) + mk.shape[2:], lambda b, i: (kcol, b, 0, 0, 0))
    in_specs = [pl.BlockSpec((tq, w), lambda b, i: (r0 + b * nq + i, 0)), kspec, vspec]
    args = [q, mk, mv]
    alias = {}
    if o_buf is not None:
        in_specs.append(pl.BlockSpec(memory_space=pl.ANY))
        args.append(o_buf)
        alias = {3: 0}
    return pl.pallas_call(
        functools.partial(_mem_kernel, heads=heads, dim=dim, scale=dim ** -0.5),
        grid=(nbatch, nq),
        in_specs=in_specs,
        out_specs=pl.BlockSpec((tq, w), lambda b, i: (r0 + b * nq + i, 0)),
        out_shape=jax.ShapeDtypeStruct((t, w), BF16),
        input_output_aliases=alias,
        compiler_params=_cp("parallel", "parallel"),
        name=name,
    )(*args)


def _ffn_up_kernel(x_ref, *refs, seg, nseg, nch, blocks_per_seq, use_state, has_buf):
    wa_refs, wg_refs = refs[:nch], refs[nch:2 * nch]
    wdw_ref, bdw_ref, st_ref = refs[2 * nch:2 * nch + 3]
    rest = refs[2 * nch + 3 + (1 if has_buf else 0):]
    if use_state:
        u_ref, tail_ref = rest
        carry_ref = None
    else:
        u_ref, tail_ref, carry_ref = rest
    i, j = pl.program_id(0), pl.program_id(1)
    tc = FFN_CHUNK
    row = lax.broadcasted_iota(jnp.int32, (8, tc), 0)
    if not use_state:
        @pl.when(i % blocks_per_seq == 0)
        def _():
            carry_ref[j] = jnp.zeros(carry_ref.shape[1:], F32)
    for c in range(nch):
        cs = slice(c * tc, (c + 1) * tc)
        a = jnp.dot(x_ref[...], wa_refs[c][...], preferred_element_type=F32)
        g = jnp.dot(x_ref[...], wg_refs[c][...], preferred_element_type=F32)
        w0, w1, w2 = wdw_ref[0:1, cs], wdw_ref[1:2, cs], wdw_ref[2:3, cs]
        for s in range(nseg):
            gs = g[s * seg:(s + 1) * seg]
            if use_state:
                p2, p1 = st_ref[s, 0:1, cs], st_ref[s, 1:2, cs]
            else:
                p2, p1 = carry_ref[j, 6:7, cs], carry_ref[j, 7:8, cs]
            prev8 = jnp.where(row == 6, p2, jnp.where(row == 7, p1, 0.0))
            g3 = gs.reshape(seg // 8, 8, tc)
            shifted = []
            for sh in (1, 2):
                cur = pltpu.roll(g3, sh, 1)
                before = jnp.concatenate([pltpu.roll(prev8, sh, 0)[None], cur[:-1]], axis=0)
                shifted.append(jnp.where(row[None] >= sh, cur, before).reshape(seg, tc))
            gm1, gm2 = shifted
            gc = ((bdw_ref[:, cs] + w0 * gm2) + w1 * gm1) + w2 * gs
            u_ref[s * seg:(s + 1) * seg, cs] = (
                a[s * seg:(s + 1) * seg] * (gc * jax.nn.sigmoid(gc))).astype(BF16)
            tail = gs[seg - 8:seg]
            tail_ref[s, :, cs] = tail
            if not use_state:
                carry_ref[j, :, cs] = tail


def ffn_up(h, w_up, w_dw, b_dw, state, u_buf, layer, *, row0, rows, seg, blocks_per_seq, use_state,
           name):
    t, d = h.shape
    ff = w_dw.shape[2]
    nblk = ff // FFN_CHUNK
    assert nblk * FFN_CHUNK == ff
    nch = FFN_NCHUNK if (u_buf is None and ff >= FFN_NCHUNK * FFN_CHUNK) else 1
    tn = nch * FFN_CHUNK
    if use_state:
        tm, nseg = rows, rows // seg
    else:
        tm, nseg = seg, 1
    r0 = row0 // tm
    ni, nj = rows // tm, pl.cdiv(ff, tn)
    scratch = [] if use_state else [pltpu.VMEM((nj, 8, tn), F32)]

    def wspec(half, c):
        return pl.BlockSpec(
            (None, d, FFN_CHUNK),
            lambda i, j: (layer, 0, jnp.minimum(half * nblk + j * nch + c, 2 * nblk - 1)))

    in_specs = ([pl.BlockSpec((tm, d), lambda i, j: (r0 + i, 0))]
                + [wspec(0, c) for c in range(nch)] + [wspec(1, c) for c in range(nch)]
                + [pl.BlockSpec((None, CONV_W, tn), lambda i, j: (layer, 0, j)),
                   pl.BlockSpec((None, 1, tn), lambda i, j: (layer, 0, j)),
                   pl.BlockSpec((None, state.shape[1], CONV_W - 1, tn),
                                lambda i, j: (layer, 0, 0, j))])
    args = [h] + [w_up] * (2 * nch) + [w_dw, b_dw.reshape(b_dw.shape[0], 1, ff), state]
    alias = {}
    if u_buf is not None:
        alias = {len(args): 0}
        in_specs.append(pl.BlockSpec(memory_space=pl.ANY))
        args.append(u_buf)
    return pl.pallas_call(
        functools.partial(_ffn_up_kernel, seg=seg, nseg=nseg, nch=nch,
                          blocks_per_seq=blocks_per_seq, use_state=use_state,
                          has_buf=u_buf is not None),
        grid=(ni, nj),
        in_specs=in_specs,
        out_specs=[pl.BlockSpec((tm, tn), lambda i, j: (r0 + i, j)),
                   pl.BlockSpec((nseg, 8, tn), lambda i, j: (i, 0, j))],
        out_shape=[jax.ShapeDtypeStruct((t, ff), BF16),
                   jax.ShapeDtypeStruct((ni * nseg, 8, ff), F32)],
        scratch_shapes=scratch,
        input_output_aliases=alias,
        compiler_params=_cp("arbitrary", "arbitrary"),
        name=name,
    )(*args)


def _rope_tables(pos):
    half = 32
    inv = ROPE_THETA ** (-jnp.arange(half, dtype=F32) / half)
    ang = pos.astype(F32)[:, None] * inv[None, :]
    c, s, z = jnp.cos(ang), jnp.sin(ang), jnp.zeros_like(ang)
    return jnp.concatenate([c, z, c, z], axis=1), jnp.concatenate([-s, z, s, z], axis=1)


def _slot_cols(w, half):
    z = jnp.zeros(w.shape[:-1] + (LANES // 2 - half,), w.dtype)
    return jnp.concatenate([w[..., :half], z, w[..., half:], z], axis=-1)


def kernel(x_prompt, x_sample, mem_prompt, cache_mla_ckv, cache_mla_krope, cache_band_k, cache_band_v, cache_mem_k, cache_mem_v, state_conv, norm_mix, w_in, norm_cq, norm_ckv, w_uq, w_uk, w_uv, rel_bias, g_out_a, g_out_b, w_o, norm_mem, norm_memtok, w_mq, w_mkv, w_mo, norm_ffn, w_up, w_dw, b_dw, w_down, norm_final):
    batch, seq, d = x_prompt.shape
    dbatch, dseq, _ = x_sample.shape
    depth = norm_mix.shape[0]
    past = cache_mla_ckv.shape[2]
    nband = cache_band_k.shape[2]
    ql, kvl = norm_cq.shape[1], norm_ckv.shape[1]
    rope = cache_mla_krope.shape[3]
    half = rope // 2
    a_heads, a_nope = w_uk.shape[2], w_uk.shape[3]
    a_vdim = w_uv.shape[3]
    b_heads, b_dim = cache_band_k.shape[3], cache_band_k.shape[4]
    mtok, m_heads, m_dim = cache_mem_k.shape[2], cache_mem_k.shape[3], cache_mem_k.shape[4]
    mem_w = m_heads * m_dim
    ff = b_dw.shape[1]
    assert a_nope == LANES and a_vdim == LANES and b_dim == LANES and rope == LANES // 2
    assert seq % CHUNK == 0 and dseq == CHUNK and past % CHUNK == 0

    tp, ts = batch * seq, dbatch * dseq
    t = tp + ts
    band_keep = min(BAND_PREV * CHUNK, seq)
    mla_scale = (a_nope + rope) ** -0.5
    b_scale = b_dim ** -0.5

    pos = jnp.concatenate([jnp.tile(jnp.arange(seq, dtype=jnp.int32), batch),
                           jnp.tile(past + jnp.arange(dseq, dtype=jnp.int32), dbatch)])
    cos_t, sin_t = _rope_tables(pos)
    band_tq = _pick(seq, (256, 128, 64))
    nprev = (BAND_PREV * CHUNK) // band_tq
    assert nprev * band_tq == BAND_PREV * CHUNK and nprev <= 2
    perm = _slot_cols(jnp.eye(rope, dtype=BF16), half)
    tk_down = _pick(ff, (5504, 2816, 2048, 1024, 512, 256))
    ffn_tm = _pick(math.gcd(seq, 1024), (1024, 512, 256, 128))
    mla_t = _pick(seq, (512, 256, 128))
    mla_nh = 2 if a_heads % 2 == 0 else 1
    cos_tt, sin_tt = cos_t[:tp].T, sin_t[:tp].T

    w_up_b, w_down_b, w_o_b = w_up.astype(BF16), w_down.astype(BF16), w_o.astype(BF16)
    w_mq_b, w_mkv_b, w_mo_b = w_mq.astype(BF16), w_mkv.astype(BF16), w_mo.astype(BF16)
    c_ckv = cache_mla_ckv.reshape(depth, dbatch * past, kvl)
    c_kr = cache_mla_krope.reshape(depth, dbatch * past, rope)

    x = [x_prompt.reshape(tp, d), x_sample.reshape(ts, d)]
    outs = {k: [] for k in ("p_ckv", "p_kr", "p_bk", "p_bv", "p_mk", "p_mv", "p_conv",
                            "s_ckv", "s_kr", "s_bk", "s_bv", "s_conv")}

    def unslot(kr):
        return jnp.concatenate([kr[:, :half], kr[:, LANES // 2:LANES // 2 + half]], axis=1)

    for l in range(depth):
        wi = w_in[l]
        w_lat = jnp.concatenate([wi[:, :ql + kvl], _slot_cols(wi[:, ql + kvl:ql + kvl + rope], half)],
                                axis=1).astype(BF16)
        w_qkvb = wi[:, ql + kvl + rope:].astype(BF16)
        wq = w_uq[l].reshape(ql, a_heads, a_nope + rope)
        wq = jnp.concatenate([wq[..., :a_nope], _slot_cols(wq[..., a_nope:], half)], axis=-1)
        wq = wq.reshape(ql, a_heads * HEAD_SLOT).astype(BF16)
        wq_t = wq.T
        wuk = w_uk[l].reshape(kvl, a_heads * a_nope).astype(BF16)
        wuv = w_uv[l].reshape(kvl, a_heads * a_vdim).astype(BF16)
        wuv_t = wuv.T

        h = rmsnorm_parts(x, norm_mix[l], BF16, name="norm_mix")
        cq, ckv, ckv_b, kr, kr_b = latent_project(h, w_lat, norm_cq[l], norm_ckv[l], cos_t, sin_t)
        qkvb = matmul(h, w_qkvb, BF16, name="qkv_band")
        sel = jnp.concatenate([h[b * seq + seq - band_keep:(b + 1) * seq] for b in range(batch)]
                              + [h[tp:]], axis=0)
        kv_keep = matmul(sel, w_qkvb[:, b_heads * b_dim:], F32, name="kv_band_keep")
        q_t = q_project_t(cq, wq_t, cos_tt, sin_tt, mla_scale * LOG2E, rows=tp, tq=mla_t)
        q_s = q_project(cq, wq, cos_t, sin_t, mla_scale * LOG2E, row0=tp, rows=ts)
        k, v, v_t = kv_decompress(ckv_b, kr_b, wuk, wuv, perm, permute=False, tm=mla_t, w_uv_t=wuv_t)
        kc, vc = kv_decompress(c_ckv, c_kr, wuk, wuv, perm, permute=True, layer=l,
                               tm=_pick(dbatch * past, (512, 256, 128)))
        oa = mla_prompt(q_t, k, v_t, batch, seq, a_heads, mla_t, mla_nh)
        oa = mla_sample(q_s, kc, vc, k, v, oa, dbatch, dseq, past, a_heads, tp)
        bias_p = _band_bias_table(rel_bias[l], band_tq, (nprev + 1) * band_tq, nprev * band_tq)
        bias_s = _band_bias_table(rel_bias[l], dseq, nband + dseq, nband)
        ob = band_prompt(qkvb, bias_p, batch, seq, b_heads, band_tq, nprev, b_scale)
        ob = band_sample(qkvb, cache_band_k, cache_band_v, bias_s, ob, l, dbatch, dseq, nband,
                         b_heads, tp, b_scale)
        x = norm_matmul([oa, ob], [g_out_a[l], g_out_b[l]], w_o_b, layer=l, res=x, name="out_proj")

        memn = rmsnorm(mem_prompt.reshape(batch * mtok, d), norm_memtok[l], BF16, name="norm_memtok")
        mkv = matmul(memn, w_mkv_b, F32, layer=l, name="mem_kv")
        qm = norm_matmul([x], [norm_mem[l]], w_mq_b, layer=l, out_dtype=BF16, name="mem_q")
        om = mem_attend(qm, mkv, mkv, 0, 1, None, nbatch=batch, rows_per_batch=seq, row0=0,
                        heads=m_heads, dim=m_dim, name="mem_prompt")
        om = mem_attend(qm, cache_mem_k, cache_mem_v, l, l, om, nbatch=dbatch, rows_per_batch=dseq,
                        row0=tp, heads=m_heads, dim=m_dim, name="mem_sample")
        x = matmul_residual(om, w_mo_b, x, layer=l, name="mem_out")

        h = rmsnorm(x, norm_ffn[l], BF16, name="norm_ffn")
        u, tail_p = ffn_up(h, w_up_b, w_dw, b_dw, state_conv, None, l, row0=0, rows=tp, seg=ffn_tm,
                           blocks_per_seq=seq // ffn_tm, use_state=False, name="ffn_up_prompt")
        u, tail_s = ffn_up(h, w_up_b, w_dw, b_dw, state_conv, u, l, row0=tp, rows=ts, seg=dseq,
                           blocks_per_seq=1, use_state=True, name="ffn_up_sample")
        y = matmul_residual_ktiled(u, w_down_b, x, tk=tk_down, layer=l, name="ffn_down")
        x = [y]

        outs["p_ckv"].append(ckv[:tp].reshape(batch, seq, kvl))
        outs["s_ckv"].append(ckv[tp:].reshape(dbatch, dseq, kvl))
        kr64 = unslot(kr)
        outs["p_kr"].append(kr64[:tp].reshape(batch, seq, rope))
        outs["s_kr"].append(kr64[tp:].reshape(dbatch, dseq, rope))
        hw = b_heads * b_dim
        nkp = batch * band_keep
        outs["p_bk"].append(kv_keep[:nkp, :hw].reshape(batch, band_keep, b_heads, b_dim))
        outs["p_bv"].append(kv_keep[:nkp, hw:].reshape(batch, band_keep, b_heads, b_dim))
        outs["s_bk"].append(kv_keep[nkp:, :hw].reshape(dbatch, dseq, b_heads, b_dim))
        outs["s_bv"].append(kv_keep[nkp:, hw:].reshape(dbatch, dseq, b_heads, b_dim))
        outs["p_mk"].append(mkv[:, :mem_w].reshape(batch, mtok, m_heads, m_dim))
        outs["p_mv"].append(mkv[:, mem_w:].reshape(batch, mtok, m_heads, m_dim))
        nblk = seq // ffn_tm
        tail_p = tail_p.reshape(batch, nblk, 8, ff)[:, nblk - 1, 8 - (CONV_W - 1):]
        outs["p_conv"].append(tail_p)
        outs["s_conv"].append(tail_s[:, 8 - (CONV_W - 1):])

    y_prompt = rmsnorm(y, norm_final, F32, row0=0, rows=tp, name="norm_final_p").reshape(batch, seq, d)
    y_sample = rmsnorm(y, norm_final, F32, row0=tp, rows=ts, name="norm_final_s").reshape(dbatch, dseq, d)
    st = {k_: jnp.stack(v_) for k_, v_ in outs.items()}
    return (y_prompt, y_sample, st["p_ckv"], st["p_kr"], st["p_bk"], st["p_bv"], st["p_mk"],
            st["p_mv"], st["p_conv"], st["s_ckv"], st["s_kr"], st["s_bk"], st["s_bv"], st["s_conv"])
```

```python
import functools
import math

import jax
import jax.numpy as jnp
import numpy as np
from jax import lax
from jax.experimental import pallas as pl
from jax.experimental.pallas import tpu as pltpu

CHUNK = 64
BAND_PREV = 8
MAX_REL = 128
CONV_W = 3
ROPE_THETA = 10000.0
EPS = 1e-6
NEG_INF = -1e30
LOG2E = math.log2(math.e)

LANES = 128
HEAD_SLOT = 256
VMEM_LIMIT = 56 * 1024 * 1024
FFN_CHUNK = 256
FFN_NCHUNK = 2

F32 = jnp.float32
BF16 = jnp.bfloat16


def _cp(*sem):
    return pltpu.CompilerParams(dimension_semantics=sem, vmem_limit_bytes=VMEM_LIMIT)


def _pick(n, prefs):
    for p in prefs:
        if n % p == 0:
            return p
    return n


def _rmsnorm_rows(x, g):
    ms = jnp.mean(x * x, axis=-1, keepdims=True)
    return (x * lax.rsqrt(ms + EPS)) * g


def _rmsnorm_kernel(x_ref, g_ref, o_ref):
    o_ref[...] = _rmsnorm_rows(x_ref[...].astype(F32), g_ref[...]).astype(o_ref.dtype)


def rmsnorm(x, g, out_dtype, *, row0=0, rows=None, name="rmsnorm"):
    m, d = x.shape
    rows = m if rows is None else rows
    tm = _pick(math.gcd(rows, row0) if row0 else rows, (256, 128, 64, 32, 16, 8))
    off = row0 // tm
    return pl.pallas_call(
        _rmsnorm_kernel,
        grid=(rows // tm,),
        in_specs=[pl.BlockSpec((tm, d), lambda i: (i + off, 0)),
                  pl.BlockSpec((1, d), lambda i: (0, 0))],
        out_specs=pl.BlockSpec((tm, d), lambda i: (i, 0)),
        out_shape=jax.ShapeDtypeStruct((rows, d), out_dtype),
        compiler_params=_cp("parallel"),
        name=name,
    )(x, g.reshape(1, d).astype(F32))


def _part_specs(parts, block, col):
    specs, off = [], 0
    for p in parts:
        nb = p.shape[0] // block[0]
        specs.append(pl.BlockSpec(
            block, lambda i, *r, off=off, nb=nb: (jnp.clip(i - off, 0, nb - 1), col(*r))))
        off += nb
    return specs


def _part_rows(refs, parts_blocks, i):
    x, bound = refs[0][...], 0
    for r, nb in zip(refs[1:], parts_blocks[:-1]):
        bound += nb
        x = jnp.where(i >= bound, r[...], x)
    return x


def _rmsnorm_parts_kernel(*refs, blocks):
    g_ref, o_ref = refs[-2:]
    x = _part_rows(refs[:-2], blocks, pl.program_id(0))
    o_ref[...] = _rmsnorm_rows(x, g_ref[...]).astype(o_ref.dtype)


def rmsnorm_parts(parts, g, out_dtype, *, name):
    d = parts[0].shape[1]
    tm = _pick(math.gcd(*[p.shape[0] for p in parts]), (256, 128, 64, 32, 16, 8))
    blocks = tuple(p.shape[0] // tm for p in parts)
    return pl.pallas_call(
        functools.partial(_rmsnorm_parts_kernel, blocks=blocks),
        grid=(sum(blocks),),
        in_specs=_part_specs(parts, (tm, d), lambda: 0) + [pl.BlockSpec((1, d), lambda i: (0, 0))],
        out_specs=pl.BlockSpec((tm, d), lambda i: (i, 0)),
        out_shape=jax.ShapeDtypeStruct((sum(blocks) * tm, d), out_dtype),
        compiler_params=_cp("parallel"),
        name=name,
    )(*parts, g.reshape(1, d).astype(F32))


def _mm_kernel(x_ref, w_ref, o_ref):
    o_ref[...] = jnp.dot(x_ref[...].astype(BF16), w_ref[...],
                         preferred_element_type=F32).astype(o_ref.dtype)


def _wspec(w, layer, block, index_map):
    if w.ndim == 2:
        return pl.BlockSpec(block, index_map)
    return pl.BlockSpec((None,) + block, lambda *a: (layer,) + index_map(*a))


def matmul(x, w, out_dtype, *, layer=0, tm_prefs=(1024, 512, 256, 128, 64, 32, 16, 8),
           tn_prefs=(512, 256, 128), name="matmul"):
    m, k = x.shape
    n = w.shape[-1]
    tm, tn = _pick(m, tm_prefs), _pick(n, tn_prefs)
    return pl.pallas_call(
        _mm_kernel,
        grid=(m // tm, n // tn),
        in_specs=[pl.BlockSpec((tm, k), lambda i, j: (i, 0)),
                  _wspec(w, layer, (k, tn), lambda i, j: (0, j))],
        out_specs=pl.BlockSpec((tm, tn), lambda i, j: (i, j)),
        out_shape=jax.ShapeDtypeStruct((m, n), out_dtype),
        compiler_params=_cp("parallel", "parallel"),
        name=name,
    )(x, w)


def _mm_res_kernel(x_ref, w_ref, r_ref, o_ref):
    o_ref[...] = r_ref[...] + jnp.dot(x_ref[...], w_ref[...], preferred_element_type=F32)


def matmul_residual(x, w, res, *, layer=0, name="matmul_res"):
    m, k = x.shape
    n = w.shape[-1]
    tm, tn = _pick(m, (1024, 512, 256, 128)), _pick(n, (512, 256, 128))
    return pl.pallas_call(
        _mm_res_kernel,
        grid=(m // tm, n // tn),
        in_specs=[pl.BlockSpec((tm, k), lambda i, j: (i, 0)),
                  _wspec(w, layer, (k, tn), lambda i, j: (0, j)),
                  pl.BlockSpec((tm, tn), lambda i, j: (i, j))],
        out_specs=pl.BlockSpec((tm, tn), lambda i, j: (i, j)),
        out_shape=jax.ShapeDtypeStruct((m, n), F32),
        input_output_aliases={2: 0},
        compiler_params=_cp("parallel", "parallel"),
        name=name,
    )(x, w, res)


def _mm_res_ktiled_kernel(x_ref, w_ref, r_ref, o_ref, acc_ref):
    kk = pl.program_id(2)

    @pl.when(kk == 0)
    def _():
        acc_ref[...] = jnp.zeros_like(acc_ref)

    acc_ref[...] += jnp.dot(x_ref[...], w_ref[...], preferred_element_type=F32)

    @pl.when(kk == pl.num_programs(2) - 1)
    def _():
        o_ref[...] = r_ref[...] + acc_ref[...]


def matmul_residual_ktiled(x, w, res, *, tk, layer=0, name="matmul_res_k"):
    m, k = x.shape
    n = w.shape[-1]
    tm, tn = _pick(m, (1024, 512, 256, 128)), _pick(n, (512, 256, 128))
    return pl.pallas_call(
        _mm_res_ktiled_kernel,
        grid=(m // tm, n // tn, k // tk),
        in_specs=[pl.BlockSpec((tm, tk), lambda i, j, kk: (i, kk)),
                  _wspec(w, layer, (tk, tn), lambda i, j, kk: (kk, j)),
                  pl.BlockSpec((tm, tn), lambda i, j, kk: (i, j))],
        out_specs=pl.BlockSpec((tm, tn), lambda i, j, kk: (i, j)),
        out_shape=jax.ShapeDtypeStruct((m, n), F32),
        scratch_shapes=[pltpu.VMEM((tm, tn), F32)],
        input_output_aliases={2: 0},
        compiler_params=_cp("parallel", "parallel", "arbitrary"),
        name=name,
    )(x, w, res)


def _rope_slot(r, cos, sin):
    return r * cos + pltpu.roll(r, 64, 1) * sin


def _lat_kernel(h_ref, w_ref, gq_ref, gkv_ref, cos_ref, sin_ref,
                cq_ref, ckv_ref, ckvb_ref, kr_ref, krb_ref, *, ql, kvl):
    acc = jnp.dot(h_ref[...], w_ref[...], preferred_element_type=F32)
    cq_ref[...] = _rmsnorm_rows(acc[:, :ql], gq_ref[...]).astype(BF16)
    ckv = _rmsnorm_rows(acc[:, ql:ql + kvl], gkv_ref[...])
    ckv_ref[...] = ckv
    ckvb_ref[...] = ckv.astype(BF16)
    kr = _rope_slot(acc[:, ql + kvl:], cos_ref[...], sin_ref[...])
    kr_ref[...] = kr
    krb_ref[...] = kr.astype(BF16)


def latent_project(h, w_lat, g_cq, g_ckv, cos_t, sin_t):
    m, d = h.shape
    ql, kvl = g_cq.shape[0], g_ckv.shape[0]
    n = w_lat.shape[1]
    tm = _pick(m, (512, 256, 128))
    row = lambda i: (i, 0)
    fix = lambda i: (0, 0)
    return pl.pallas_call(
        functools.partial(_lat_kernel, ql=ql, kvl=kvl),
        grid=(m // tm,),
        in_specs=[pl.BlockSpec((tm, d), row), pl.BlockSpec((d, n), fix),
                  pl.BlockSpec((1, ql), fix), pl.BlockSpec((1, kvl), fix),
                  pl.BlockSpec((tm, LANES), row), pl.BlockSpec((tm, LANES), row)],
        out_specs=[pl.BlockSpec((tm, ql), row), pl.BlockSpec((tm, kvl), row),
                   pl.BlockSpec((tm, kvl), row), pl.BlockSpec((tm, LANES), row),
                   pl.BlockSpec((tm, LANES), row)],
        out_shape=[jax.ShapeDtypeStruct((m, ql), BF16), jax.ShapeDtypeStruct((m, kvl), F32),
                   jax.ShapeDtypeStruct((m, kvl), BF16), jax.ShapeDtypeStruct((m, LANES), F32),
                   jax.ShapeDtypeStruct((m, LANES), BF16)],
        compiler_params=_cp("parallel"),
        name="latent_project",
    )(h, w_lat, g_cq.reshape(1, ql), g_ckv.reshape(1, kvl), cos_t, sin_t)


def _q_kernel(c_ref, w_ref, cos_ref, sin_ref, o_ref, *, heads, scale):
    acc = jnp.dot(c_ref[...], w_ref[...], preferred_element_type=F32)
    cos, sin = cos_ref[...], sin_ref[...]
    for h in range(heads):
        lo = h * HEAD_SLOT
        o_ref[:, lo:lo + LANES] = (acc[:, lo:lo + LANES] * scale).astype(BF16)
        r = _rope_slot(acc[:, lo + LANES:lo + HEAD_SLOT], cos, sin)
        o_ref[:, lo + LANES:lo + HEAD_SLOT] = (r * scale).astype(BF16)


def q_project(cq, w_uq, cos_t, sin_t, scale, *, row0, rows):
    ql = cq.shape[1]
    n = w_uq.shape[1]
    tm = _pick(math.gcd(rows, row0), (512, 256, 128, 64))
    tn = _pick(n, (1024, 512, 256))
    r0 = row0 // tm
    return pl.pallas_call(
        functools.partial(_q_kernel, heads=tn // HEAD_SLOT, scale=scale),
        grid=(rows // tm, n // tn),
        in_specs=[pl.BlockSpec((tm, ql), lambda i, j: (r0 + i, 0)),
                  pl.BlockSpec((ql, tn), lambda i, j: (0, j)),
                  pl.BlockSpec((tm, LANES), lambda i, j: (r0 + i, 0)),
                  pl.BlockSpec((tm, LANES), lambda i, j: (r0 + i, 0))],
        out_specs=pl.BlockSpec((tm, tn), lambda i, j: (i, j)),
        out_shape=jax.ShapeDtypeStruct((rows, n), BF16),
        compiler_params=_cp("parallel", "parallel"),
        name="q_project",
    )(cq, w_uq, cos_t, sin_t)


def _qt_kernel(c_ref, wt_ref, cos_ref, sin_ref, o_ref, *, heads, scale):
    acc = lax.dot_general(wt_ref[...], c_ref[...], (((1,), (1,)), ((), ())),
                          preferred_element_type=F32)
    cos, sin = cos_ref[...], sin_ref[...]
    for h in range(heads):
        lo = h * HEAD_SLOT
        o_ref[0, lo:lo + LANES, :] = (acc[lo:lo + LANES] * scale).astype(BF16)
        r = acc[lo + LANES:lo + HEAD_SLOT]
        r = r * cos + pltpu.roll(r, 64, 0) * sin
        o_ref[0, lo + LANES:lo + HEAD_SLOT, :] = (r * scale).astype(BF16)


def q_project_t(cq, w_uq_t, cos_tt, sin_tt, scale, *, rows, tq):
    ql = cq.shape[1]
    n = w_uq_t.shape[0]
    tn = _pick(n, (1024, 512, 256))
    return pl.pallas_call(
        functools.partial(_qt_kernel, heads=tn // HEAD_SLOT, scale=scale),
        grid=(rows // tq, n // tn),
        in_specs=[pl.BlockSpec((tq, ql), lambda i, j: (i, 0)),
                  pl.BlockSpec((tn, ql), lambda i, j: (j, 0)),
                  pl.BlockSpec((LANES, tq), lambda i, j: (0, i)),
                  pl.BlockSpec((LANES, tq), lambda i, j: (0, i))],
        out_specs=pl.BlockSpec((1, tn, tq), lambda i, j: (i, j, 0)),
        out_shape=jax.ShapeDtypeStruct((rows // tq, n, tq), BF16),
        compiler_params=_cp("parallel", "parallel"),
        name="q_project_t",
    )(cq, w_uq_t, cos_tt, sin_tt)


def _kv_kernel(c_ref, kr_ref, wk_ref, wvt_ref, k_ref, vt_ref, *, heads):
    c = c_ref[...]
    kn = jnp.dot(c, wk_ref[...], preferred_element_type=F32)
    vt_ref[0] = lax.dot_general(wvt_ref[...], c, (((1,), (1,)), ((), ())),
                                preferred_element_type=F32).astype(BF16)
    kr = kr_ref[...]
    for h in range(heads):
        lo = h * HEAD_SLOT
        k_ref[:, lo:lo + LANES] = kn[:, h * LANES:(h + 1) * LANES].astype(BF16)
        k_ref[:, lo + LANES:lo + HEAD_SLOT] = kr


def kv_decompress(ckv, kr, w_uk, w_uv_t, *, rows, tm):
    kvl = ckv.shape[1]
    heads = w_uk.shape[1] // LANES
    row = lambda i: (i, 0)
    fix = lambda i: (0, 0)
    return pl.pallas_call(
        functools.partial(_kv_kernel, heads=heads),
        grid=(rows // tm,),
        in_specs=[pl.BlockSpec((tm, kvl), row), pl.BlockSpec((tm, LANES), row),
                  pl.BlockSpec(w_uk.shape, fix), pl.BlockSpec(w_uv_t.shape, fix)],
        out_specs=[pl.BlockSpec((tm, heads * HEAD_SLOT), row),
                   pl.BlockSpec((1, heads * LANES, tm), lambda i: (i, 0, 0))],
        out_shape=[jax.ShapeDtypeStruct((rows, heads * HEAD_SLOT), BF16),
                   jax.ShapeDtypeStruct((rows // tm, heads * LANES, tm), BF16)],
        compiler_params=_cp("parallel"),
        name="kv_decompress",
    )(ckv, kr, w_uk, w_uv_t)


def _qk(q, k):
    return lax.dot_general(q, k, (((1,), (1,)), ((), ())), preferred_element_type=F32)


def _mla_prompt_kernel(qt_ref, k_ref, vt_ref, o_ref, m_ref, l_ref, acc_ref, s0_ref, s1_ref, *, t, nh):
    qi = pl.program_id(2)
    m_ref[...] = jnp.full_like(m_ref, NEG_INF)
    l_ref[...] = jnp.zeros_like(l_ref)
    acc_ref[...] = jnp.zeros_like(acc_ref)

    def scores(ki, s_ref):
        start = pl.multiple_of(ki * t, t)
        for h in range(nh):
            s_ref[h] = jnp.dot(k_ref[pl.ds(start, t), h * HEAD_SLOT:(h + 1) * HEAD_SLOT],
                               qt_ref[0, h * HEAD_SLOT:(h + 1) * HEAD_SLOT, :],
                               preferred_element_type=F32)

    def consume(ki, s_ref, mask):
        for h in range(nh):
            s = s_ref[h]
            if mask is not None:
                s = jnp.where(mask, s, NEG_INF)
            m_old = m_ref[h]
            m_new = jnp.maximum(m_old, jnp.max(s, axis=0, keepdims=True))
            alpha = jnp.exp2(m_old - m_new)
            p = jnp.exp2(s - m_new)
            l_ref[h] = alpha * l_ref[h] + jnp.sum(p, axis=0, keepdims=True)
            acc_ref[h] = alpha * acc_ref[h] + jnp.dot(
                vt_ref[ki, h * LANES:(h + 1) * LANES, :], p.astype(BF16),
                preferred_element_type=F32)
            m_ref[h] = m_new

    scores(0, s0_ref)

    def body(j, carry):
        scores(2 * j + 1, s1_ref)
        consume(2 * j, s0_ref, None)
        scores(2 * j + 2, s0_ref)
        consume(2 * j + 1, s1_ref, None)
        return carry

    lax.fori_loop(0, qi // 2, body, 0)
    kc = lax.broadcasted_iota(jnp.int32, (t, t), 0) // CHUNK
    qc = lax.broadcasted_iota(jnp.int32, (t, t), 1) // CHUNK
    mask = kc <= qc

    @pl.when(qi % 2 == 0)
    def _():
        consume(qi, s0_ref, mask)

    @pl.when(qi % 2 == 1)
    def _():
        scores(qi, s1_ref)
        consume(qi - 1, s0_ref, None)
        consume(qi, s1_ref, mask)

    for h in range(nh):
        o_ref[:, h * LANES:(h + 1) * LANES] = (acc_ref[h] / l_ref[h]).T


def mla_prompt(qt, k, vt, rows, batch, seq, heads, t, nh):
    nq = seq // t
    return pl.pallas_call(
        functools.partial(_mla_prompt_kernel, t=t, nh=nh),
        grid=(batch, heads // nh, nq),
        in_specs=[pl.BlockSpec((1, nh * HEAD_SLOT, t), lambda b, g, i: (b * nq + i, g, 0)),
                  pl.BlockSpec((seq, nh * HEAD_SLOT), lambda b, g, i: (b, g)),
                  pl.BlockSpec((nq, nh * LANES, t), lambda b, g, i: (b, g, 0))],
        out_specs=pl.BlockSpec((t, nh * LANES), lambda b, g, i: (b * nq + i, g)),
        out_shape=jax.ShapeDtypeStruct((rows, heads * LANES), F32),
        scratch_shapes=[pltpu.VMEM((nh, 1, t), F32), pltpu.VMEM((nh, 1, t), F32),
                        pltpu.VMEM((nh, LANES, t), F32),
                        pltpu.VMEM((nh, t, t), F32), pltpu.VMEM((nh, t, t), F32)],
        compiler_params=_cp("parallel", "parallel", "arbitrary"),
        name="mla_prompt",
    )(qt, k, vt)


def _q_absorb_kernel(q_ref, wk_ref, o_ref):
    kvl = wk_ref.shape[0]
    o_ref[0, :, :kvl] = lax.dot_general(q_ref[:, :LANES], wk_ref[...], (((1,), (1,)), ((), ())),
                                        preferred_element_type=F32).astype(BF16)
    o_ref[0, :, kvl:] = q_ref[:, LANES:]


def _mla_latent_kernel(q_ref, cc_ref, kc_ref, perm_ref, cn_ref, kn_ref, o_ref):
    heads, dseq, width = q_ref.shape
    q = q_ref[...].reshape(heads * dseq, width)
    cc = cc_ref[...].astype(BF16)
    krc = jnp.dot(kc_ref[...].astype(BF16), perm_ref[...], preferred_element_type=F32).astype(BF16)
    cn = cn_ref[...]
    s1 = _qk(q, jnp.concatenate([cc, krc], axis=1))
    s2 = _qk(q, jnp.concatenate([cn, kn_ref[...]], axis=1))
    m = jnp.maximum(jnp.max(s1, axis=-1, keepdims=True), jnp.max(s2, axis=-1, keepdims=True))
    p1 = jnp.exp2(s1 - m)
    p2 = jnp.exp2(s2 - m)
    l = jnp.sum(p1, axis=-1, keepdims=True) + jnp.sum(p2, axis=-1, keepdims=True)
    o = (jnp.dot(p1.astype(BF16), cc, preferred_element_type=F32)
         + jnp.dot(p2.astype(BF16), cn, preferred_element_type=F32)) / l
    o_ref[...] = o.astype(BF16).reshape(heads, dseq, o.shape[1])


def _o_absorb_kernel(ol_ref, wv_ref, oin_ref, o_ref):
    del oin_ref
    o_ref[...] = jnp.dot(ol_ref[0], wv_ref[...], preferred_element_type=F32)


def mla_sample_latent(q_s, c_ckv, c_kr, ckv_b, kr_b, w_uk, w_uv, perm, o_buf, layer,
                      dbatch, dseq, past, heads, row0):
    ts, kvl = q_s.shape[0], w_uk.shape[0]
    width = kvl + LANES
    q_abs = pl.pallas_call(
        _q_absorb_kernel,
        grid=(heads,),
        in_specs=[pl.BlockSpec((ts, HEAD_SLOT), lambda h: (0, h)),
                  pl.BlockSpec((kvl, LANES), lambda h: (0, h))],
        out_specs=pl.BlockSpec((1, ts, width), lambda h: (h, 0, 0)),
        out_shape=jax.ShapeDtypeStruct((heads, ts, width), BF16),
        compiler_params=_cp("parallel"),
        name="mla_q_absorb",
    )(q_s, w_uk)
    r0 = row0 // dseq
    o_lat = pl.pallas_call(
        _mla_latent_kernel,
        grid=(dbatch,),
        in_specs=[pl.BlockSpec((heads, dseq, width), lambda b: (0, b, 0)),
                  pl.BlockSpec((None, past, kvl), lambda b: (layer, b, 0)),
                  pl.BlockSpec((None, past, c_kr.shape[2]), lambda b: (layer, b, 0)),
                  pl.BlockSpec(perm.shape, lambda b: (0, 0)),
                  pl.BlockSpec((dseq, kvl), lambda b: (r0 + b, 0)),
                  pl.BlockSpec((dseq, LANES), lambda b: (r0 + b, 0))],
        out_specs=pl.BlockSpec((heads, dseq, kvl), lambda b: (0, b, 0)),
        out_shape=jax.ShapeDtypeStruct((heads, ts, kvl), BF16),
        compiler_params=_cp("parallel"),
        name="mla_sample_latent",
    )(q_abs, c_ckv, c_kr, perm, ckv_b, kr_b)
    return pl.pallas_call(
        _o_absorb_kernel,
        grid=(heads,),
        in_specs=[pl.BlockSpec((1, ts, kvl), lambda h: (h, 0, 0)),
                  pl.BlockSpec((kvl, LANES), lambda h: (0, h)),
                  pl.BlockSpec(memory_space=pl.ANY)],
        out_specs=pl.BlockSpec((ts, LANES), lambda h: (row0 // ts, h)),
        out_shape=jax.ShapeDtypeStruct(o_buf.shape, F32),
        input_output_aliases={2: 0},
        compiler_params=_cp("parallel"),
        name="mla_o_absorb",
    )(o_lat, w_uv, o_buf)


def _band_prompt_kernel(q_ref, k0_ref, k1_ref, k2_ref, v0_ref, v1_ref, v2_ref, bias_ref, o_ref,
                        *, tq, nprev, scale, nh):
    qi = pl.program_id(2)
    ks = (k0_ref, k1_ref, k2_ref)[3 - nprev - 1:]
    vs = (v0_ref, v1_ref, v2_ref)[3 - nprev - 1:]
    raw = [[_qk(q_ref[:, h * LANES:(h + 1) * LANES], kr[:, h * LANES:(h + 1) * LANES]) for kr in ks]
           for h in range(nh)]
    for h in range(nh):
        hs = slice(h * LANES, (h + 1) * LANES)
        ss = []
        for d in range(len(ks)):
            s = raw[h][d] * scale + bias_ref[h, :, d * tq:(d + 1) * tq]
            if d < nprev:
                s = jnp.where(qi - (nprev - d) >= 0, s, NEG_INF)
            ss.append(s)
        m = functools.reduce(jnp.maximum, [jnp.max(s, axis=-1, keepdims=True) for s in ss])
        ps = [jnp.exp(s - m) for s in ss]
        l = functools.reduce(lambda a, b: a + b, [jnp.sum(p, axis=-1, keepdims=True) for p in ps])
        o = functools.reduce(lambda a, b: a + b,
                             [jnp.dot(p.astype(BF16), vr[:, hs], preferred_element_type=F32)
                              for p, vr in zip(ps, vs)])
        o_ref[:, hs] = o / l


def band_prompt(qkv, bias, batch, seq, heads, tq, nprev, scale):
    t = qkv.shape[0]
    nq = seq // tq
    nh = _pick(heads, (4, 2, 1))
    ng = heads // nh

    def kspec(back, col0):
        return pl.BlockSpec((tq, nh * LANES),
                            lambda b, g, i: (b * nq + jnp.maximum(i - back, 0), col0 + g))

    return pl.pallas_call(
        functools.partial(_band_prompt_kernel, tq=tq, nprev=nprev, scale=scale, nh=nh),
        grid=(batch, ng, nq),
        in_specs=[pl.BlockSpec((tq, nh * LANES), lambda b, g, i: (b * nq + i, g)),
                  kspec(2, ng), kspec(1, ng), kspec(0, ng),
                  kspec(2, 2 * ng), kspec(1, 2 * ng), kspec(0, 2 * ng),
                  pl.BlockSpec((nh, tq, (nprev + 1) * tq), lambda b, g, i: (g, 0, 0))],
        out_specs=pl.BlockSpec((tq, nh * LANES), lambda b, g, i: (b * nq + i, g)),
        out_shape=jax.ShapeDtypeStruct((t, heads * LANES), F32),
        compiler_params=_cp("parallel", "parallel", "arbitrary"),
        name="band_prompt",
    )(qkv, qkv, qkv, qkv, qkv, qkv, qkv, bias)


def _band_sample_kernel(q_ref, kc_ref, vc_ref, kn_ref, vn_ref, bias_ref, oin_ref, o_ref,
                        *, nb, scale, nh):
    del oin_ref
    for h in range(nh):
        hs = slice(h * LANES, (h + 1) * LANES)
        q = q_ref[:, hs]
        s1 = _qk(q, kc_ref[:, h, :].astype(BF16)) * scale + bias_ref[h, :, :nb]
        s2 = _qk(q, kn_ref[:, hs]) * scale + bias_ref[h, :, nb:]
        m = jnp.maximum(jnp.max(s1, axis=-1, keepdims=True), jnp.max(s2, axis=-1, keepdims=True))
        p1 = jnp.exp(s1 - m)
        p2 = jnp.exp(s2 - m)
        l = jnp.sum(p1, axis=-1, keepdims=True) + jnp.sum(p2, axis=-1, keepdims=True)
        o = (jnp.dot(p1.astype(BF16), vc_ref[:, h, :].astype(BF16), preferred_element_type=F32)
             + jnp.dot(p2.astype(BF16), vn_ref[:, hs], preferred_element_type=F32))
        o_ref[:, hs] = o / l


def band_sample(qkv, kc, vc, bias, o_buf, layer, dbatch, dseq, nb, heads, row0, scale):
    r0 = row0 // dseq
    nh, ng = heads, 1
    w = nh * LANES
    cspec = pl.BlockSpec((None, None, nb, heads, LANES), lambda b, g: (layer, b, 0, 0, 0))
    return pl.pallas_call(
        functools.partial(_band_sample_kernel, nb=nb, scale=scale, nh=nh),
        grid=(dbatch, ng),
        in_specs=[pl.BlockSpec((dseq, w), lambda b, g: (r0 + b, g)),
                  cspec, cspec,
                  pl.BlockSpec((dseq, w), lambda b, g: (r0 + b, ng + g)),
                  pl.BlockSpec((dseq, w), lambda b, g: (r0 + b, 2 * ng + g)),
                  pl.BlockSpec((nh, dseq, nb + dseq), lambda b, g: (g, 0, 0)),
                  pl.BlockSpec(memory_space=pl.ANY)],
        out_specs=pl.BlockSpec((dseq, w), lambda b, g: (r0 + b, g)),
        out_shape=jax.ShapeDtypeStruct(o_buf.shape, F32),
        input_output_aliases={6: 0},
        compiler_params=_cp("parallel", "parallel"),
        name="band_sample",
    )(qkv, kc, vc, qkv, qkv, bias, o_buf)


def _band_bias_table(rel_bias, nq, nk, q0):
    span = nq + nk - 1
    k = np.concatenate([np.arange(0, nk), np.arange(-(nq - 1), 0)])
    idx = np.clip(q0 - k, -MAX_REL, MAX_REL) + MAX_REL
    u = rel_bias.astype(F32)[:, idx]
    tab = jnp.tile(u, (1, nq))[:, :nq * (span - 1)].reshape(-1, nq, span - 1)[:, :, :nk]
    qc = (q0 + np.arange(nq))[:, None] // CHUNK
    kc = np.arange(nk)[None, :] // CHUNK
    mask = (kc <= qc) & (kc >= qc - BAND_PREV)
    return jnp.where(mask[None], tab, NEG_INF)


def _norm_mm_kernel(*refs, nparts, res_blocks):
    xs, gs = refs[:nparts], refs[nparts:2 * nparts]
    w_ref = refs[2 * nparts]
    r_refs = refs[2 * nparts + 1:-2]
    o_ref, hs_ref = refs[-2:]

    @pl.when(pl.program_id(1) == 0)
    def _():
        off = 0
        for x_ref, g_ref in zip(xs, gs):
            hp = _rmsnorm_rows(x_ref[...], g_ref[...]).astype(BF16)
            hs_ref[:, off:off + hp.shape[1]] = hp
            off += hp.shape[1]

    acc = jnp.dot(hs_ref[...], w_ref[...], preferred_element_type=F32)
    if r_refs:
        o_ref[...] = _part_rows(r_refs, res_blocks, pl.program_id(0)) + acc
    else:
        o_ref[...] = acc.astype(o_ref.dtype)


def norm_matmul(parts, gains, w, *, layer=0, res=(), out_dtype=F32, name):
    m = parts[0].shape[0]
    widths = [p.shape[1] for p in parts]
    k, n = sum(widths), w.shape[-1]
    tm, tn = _pick(math.gcd(m, *[r.shape[0] for r in res]), (512, 256, 128)), _pick(n, (1024, 512, 256, 128))
    in_specs = ([pl.BlockSpec((tm, dp), lambda i, j: (i, 0)) for dp in widths]
                + [pl.BlockSpec((1, dp), lambda i, j: (0, 0)) for dp in widths]
                + [_wspec(w, layer, (k, tn), lambda i, j: (0, j))]
                + _part_specs(res, (tm, tn), lambda j: j))
    args = list(parts) + [g.reshape(1, -1).astype(F32) for g in gains] + [w] + list(res)
    alias = {len(args) - 1: 0} if len(res) == 1 else {}
    return pl.pallas_call(
        functools.partial(_norm_mm_kernel, nparts=len(parts),
                          res_blocks=tuple(r.shape[0] // tm for r in res)),
        grid=(m // tm, n // tn),
        in_specs=in_specs,
        out_specs=pl.BlockSpec((tm, tn), lambda i, j: (i, j)),
        out_shape=jax.ShapeDtypeStruct((m, n), F32 if res else out_dtype),
        scratch_shapes=[pltpu.VMEM((tm, k), BF16)],
        input_output_aliases=alias,
        compiler_params=_cp("parallel", "arbitrary"),
        name=name,
    )(*args)


def _mem_kernel(q_ref, k_ref, v_ref, *rest, heads, dim, scale):
    o_ref = rest[-1]
    for h in range(heads):
        sl = slice(h * dim, (h + 1) * dim)
        k = k_ref[:, sl] if len(k_ref.shape) == 2 else k_ref[:, h, :]
        v = v_ref[:, sl] if len(v_ref.shape) == 2 else v_ref[:, h, :]
        s = _qk(q_ref[:, sl], k.astype(BF16)) * scale
        m = jnp.max(s, axis=-1, keepdims=True)
        p = jnp.exp(s - m)
        l = jnp.sum(p, axis=-1, keepdims=True)
        o = jnp.dot(p.astype(BF16), v.astype(BF16), preferred_element_type=F32)
        o_ref[:, sl] = (o / l).astype(BF16)


def mem_attend(q, mk, mv, kcol, vcol, o_buf, *, nbatch, rows_per_batch, row0, heads, dim, name):
    t, w = q.shape
    tq = _pick(rows_per_batch, (512, 256, 128, 64))
    nq = rows_per_batch // tq
    r0 = row0 // tq
    if mk.ndim == 2:
        mtok = mk.shape[0] // nbatch
        kspec = pl.BlockSpec((mtok, w), lambda b, i: (b, kcol))
        vspec = pl.BlockSpec((mtok, w), lambda b, i: (b, vcol))
    else:
        kspec = vspec = pl.BlockSpec((None, None) + mk.shape[2:], lambda b, i: (kcol, b, 0, 0, 0))
    in_specs = [pl.BlockSpec((tq, w), lambda b, i: (r0 + b * nq + i, 0)), kspec, vspec]
    args = [q, mk, mv]
    alias = {}
    if o_buf is not None:
        in_specs.append(pl.BlockSpec(memory_space=pl.ANY))
        args.append(o_buf)
        alias = {3: 0}
    return pl.pallas_call(
        functools.partial(_mem_kernel, heads=heads, dim=dim, scale=dim ** -0.5),
        grid=(nbatch, nq),
        in_specs=in_specs,
        out_specs=pl.BlockSpec((tq, w), lambda b, i: (r0 + b * nq + i, 0)),
        out_shape=jax.ShapeDtypeStruct((t, w), BF16),
        input_output_aliases=alias,
        compiler_params=_cp("parallel", "parallel"),
        name=name,
    )(*args)


def _ffn_up_kernel(x_ref, *refs, seg, nseg, nch, blocks_per_seq, use_state, has_buf):
    wa_refs, wg_refs = refs[:nch], refs[nch:2 * nch]
    wdw_ref, bdw_ref, st_ref = refs[2 * nch:2 * nch + 3]
    rest = refs[2 * nch + 3 + (1 if has_buf else 0):]
    if use_state:
        u_ref, tail_ref = rest
        carry_ref = None
    else:
        u_ref, tail_ref, carry_ref = rest
    i, j = pl.program_id(0), pl.program_id(1)
    tc = FFN_CHUNK
    row = lax.broadcasted_iota(jnp.int32, (8, tc), 0)
    if not use_state:
        @pl.when(i % blocks_per_seq == 0)
        def _():
            carry_ref[j] = jnp.zeros(carry_ref.shape[1:], F32)
    for c in range(nch):
        cs = slice(c * tc, (c + 1) * tc)
        a = jnp.dot(x_ref[...], wa_refs[c][...], preferred_element_type=F32)
        g = jnp.dot(x_ref[...], wg_refs[c][...], preferred_element_type=F32)
        w0, w1, w2 = wdw_ref[0:1, cs], wdw_ref[1:2, cs], wdw_ref[2:3, cs]
        for s in range(nseg):
            gs = g[s * seg:(s + 1) * seg]
            if use_state:
                p2, p1 = st_ref[s, 0:1, cs], st_ref[s, 1:2, cs]
            else:
                p2, p1 = carry_ref[j, 6:7, cs], carry_ref[j, 7:8, cs]
            prev8 = jnp.where(row == 6, p2, jnp.where(row == 7, p1, 0.0))
            g3 = gs.reshape(seg // 8, 8, tc)
            shifted = []
            for sh in (1, 2):
                cur = pltpu.roll(g3, sh, 1)
                before = jnp.concatenate([pltpu.roll(prev8, sh, 0)[None], cur[:-1]], axis=0)
                shifted.append(jnp.where(row[None] >= sh, cur, before).reshape(seg, tc))
            gm1, gm2 = shifted
            gc = ((bdw_ref[:, cs] + w0 * gm2) + w1 * gm1) + w2 * gs
            u_ref[s * seg:(s + 1) * seg, cs] = (
                a[s * seg:(s + 1) * seg] * (gc * jax.nn.sigmoid(gc))).astype(BF16)
            tail = gs[seg - 8:seg]
            tail_ref[s, :, cs] = tail
            if not use_state:
                carry_ref[j, :, cs] = tail


def ffn_up(h, w_up, w_dw, b_dw, state, u_buf, layer, *, row0, rows, seg, blocks_per_seq, use_state,
           name):
    t, d = h.shape
    ff = w_dw.shape[2]
    nblk = ff // FFN_CHUNK
    assert nblk * FFN_CHUNK == ff
    nch = FFN_NCHUNK if (u_buf is None and ff >= FFN_NCHUNK * FFN_CHUNK) else 1
    tn = nch * FFN_CHUNK
    if use_state:
        tm, nseg = rows, rows // seg
    else:
        tm, nseg = seg, 1
    r0 = row0 // tm
    ni, nj = rows // tm, pl.cdiv(ff, tn)
    scratch = [] if use_state else [pltpu.VMEM((nj, 8, tn), F32)]

    def wspec(half, c):
        return pl.BlockSpec(
            (None, d, FFN_CHUNK),
            lambda i, j: (layer, 0, jnp.minimum(half * nblk + j * nch + c, 2 * nblk - 1)))

    in_specs = ([pl.BlockSpec((tm, d), lambda i, j: (r0 + i, 0))]
                + [wspec(0, c) for c in range(nch)] + [wspec(1, c) for c in range(nch)]
                + [pl.BlockSpec((None, CONV_W, tn), lambda i, j: (layer, 0, j)),
                   pl.BlockSpec((None, 1, tn), lambda i, j: (layer, 0, j)),
                   pl.BlockSpec((None, state.shape[1], CONV_W - 1, tn),
                                lambda i, j: (layer, 0, 0, j))])
    args = [h] + [w_up] * (2 * nch) + [w_dw, b_dw.reshape(b_dw.shape[0], 1, ff), state]
    alias = {}
    if u_buf is not None:
        alias = {len(args): 0}
        in_specs.append(pl.BlockSpec(memory_space=pl.ANY))
        args.append(u_buf)
    return pl.pallas_call(
        functools.partial(_ffn_up_kernel, seg=seg, nseg=nseg, nch=nch,
                          blocks_per_seq=blocks_per_seq, use_state=use_state,
                          has_buf=u_buf is not None),
        grid=(ni, nj),
        in_specs=in_specs,
        out_specs=[pl.BlockSpec((tm, tn), lambda i, j: (r0 + i, j)),
                   pl.BlockSpec((nseg, 8, tn), lambda i, j: (i, 0, j))],
        out_shape=[jax.ShapeDtypeStruct((t, ff), BF16),
                   jax.ShapeDtypeStruct((ni * nseg, 8, ff), F32)],
        scratch_shapes=scratch,
        input_output_aliases=alias,
        compiler_params=_cp("arbitrary", "arbitrary"),
        name=name,
    )(*args)


def _rope_tables(pos):
    half = 32
    inv = ROPE_THETA ** (-jnp.arange(half, dtype=F32) / half)
    ang = pos.astype(F32)[:, None] * inv[None, :]
    c, s, z = jnp.cos(ang), jnp.sin(ang), jnp.zeros_like(ang)
    return jnp.concatenate([c, z, c, z], axis=1), jnp.concatenate([-s, z, s, z], axis=1)


def _slot_cols(w, half):
    z = jnp.zeros(w.shape[:-1] + (LANES // 2 - half,), w.dtype)
    return jnp.concatenate([w[..., :half], z, w[..., half:], z], axis=-1)


def kernel(x_prompt, x_sample, mem_prompt, cache_mla_ckv, cache_mla_krope, cache_band_k, cache_band_v, cache_mem_k, cache_mem_v, state_conv, norm_mix, w_in, norm_cq, norm_ckv, w_uq, w_uk, w_uv, rel_bias, g_out_a, g_out_b, w_o, norm_mem, norm_memtok, w_mq, w_mkv, w_mo, norm_ffn, w_up, w_dw, b_dw, w_down, norm_final):
    batch, seq, d = x_prompt.shape
    dbatch, dseq, _ = x_sample.shape
    depth = norm_mix.shape[0]
    past = cache_mla_ckv.shape[2]
    nband = cache_band_k.shape[2]
    ql, kvl = norm_cq.shape[1], norm_ckv.shape[1]
    rope = cache_mla_krope.shape[3]
    half = rope // 2
    a_heads, a_nope = w_uk.shape[2], w_uk.shape[3]
    a_vdim = w_uv.shape[3]
    b_heads, b_dim = cache_band_k.shape[3], cache_band_k.shape[4]
    mtok, m_heads, m_dim = cache_mem_k.shape[2], cache_mem_k.shape[3], cache_mem_k.shape[4]
    mem_w = m_heads * m_dim
    ff = b_dw.shape[1]
    assert a_nope == LANES and a_vdim == LANES and b_dim == LANES and rope == LANES // 2
    assert seq % CHUNK == 0 and dseq == CHUNK and past % CHUNK == 0

    tp, ts = batch * seq, dbatch * dseq
    t = tp + ts
    band_keep = min(BAND_PREV * CHUNK, seq)
    mla_scale = (a_nope + rope) ** -0.5
    b_scale = b_dim ** -0.5

    pos = jnp.concatenate([jnp.tile(jnp.arange(seq, dtype=jnp.int32), batch),
                           jnp.tile(past + jnp.arange(dseq, dtype=jnp.int32), dbatch)])
    cos_t, sin_t = _rope_tables(pos)
    band_tq = _pick(seq, (256, 128, 64))
    nprev = (BAND_PREV * CHUNK) // band_tq
    assert nprev * band_tq == BAND_PREV * CHUNK and nprev <= 2
    perm = _slot_cols(jnp.eye(rope, dtype=BF16), half)
    tk_down = _pick(ff, (5504, 2816, 2048, 1024, 512, 256))
    ffn_tm = _pick(math.gcd(seq, 1024), (1024, 512, 256, 128))
    mla_t = _pick(seq, (512, 256, 128))
    mla_nh = 2 if a_heads % 2 == 0 else 1
    cos_tt, sin_tt = cos_t[:tp].T, sin_t[:tp].T

    w_up_b, w_down_b, w_o_b = w_up.astype(BF16), w_down.astype(BF16), w_o.astype(BF16)
    w_mq_b, w_mkv_b, w_mo_b = w_mq.astype(BF16), w_mkv.astype(BF16), w_mo.astype(BF16)
    c_ckv = cache_mla_ckv.reshape(depth, dbatch * past, kvl)
    c_kr = cache_mla_krope.reshape(depth, dbatch * past, rope)

    x = [x_prompt.reshape(tp, d), x_sample.reshape(ts, d)]
    outs = {k: [] for k in ("p_ckv", "p_kr", "p_bk", "p_bv", "p_mk", "p_mv", "p_conv",
                            "s_ckv", "s_kr", "s_bk", "s_bv", "s_conv")}

    def unslot(kr):
        return jnp.concatenate([kr[:, :half], kr[:, LANES // 2:LANES // 2 + half]], axis=1)

    for l in range(depth):
        wi = w_in[l]
        w_lat = jnp.concatenate([wi[:, :ql + kvl], _slot_cols(wi[:, ql + kvl:ql + kvl + rope], half)],
                                axis=1).astype(BF16)
        w_qkvb = wi[:, ql + kvl + rope:].astype(BF16)
        wq = w_uq[l].reshape(ql, a_heads, a_nope + rope)
        wq = jnp.concatenate([wq[..., :a_nope], _slot_cols(wq[..., a_nope:], half)], axis=-1)
        wq = wq.reshape(ql, a_heads * HEAD_SLOT).astype(BF16)
        wq_t = wq.T
        wuk = w_uk[l].reshape(kvl, a_heads * a_nope).astype(BF16)
        wuv = w_uv[l].reshape(kvl, a_heads * a_vdim).astype(BF16)
        wuv_t = wuv.T

        h = rmsnorm_parts(x, norm_mix[l], BF16, name="norm_mix")
        cq, ckv, ckv_b, kr, kr_b = latent_project(h, w_lat, norm_cq[l], norm_ckv[l], cos_t, sin_t)
        qkvb = matmul(h, w_qkvb, BF16, name="qkv_band")
        sel = jnp.concatenate([h[b * seq + seq - band_keep:(b + 1) * seq] for b in range(batch)]
                              + [h[tp:]], axis=0)
        kv_keep = matmul(sel, w_qkvb[:, b_heads * b_dim:], F32, name="kv_band_keep")
        q_t = q_project_t(cq, wq_t, cos_tt, sin_tt, mla_scale * LOG2E, rows=tp, tq=mla_t)
        q_s = q_project(cq, wq, cos_t, sin_t, mla_scale * LOG2E, row0=tp, rows=ts)
        k, v_t = kv_decompress(ckv_b, kr_b, wuk, wuv_t, rows=tp, tm=mla_t)
        oa = mla_prompt(q_t, k, v_t, t, batch, seq, a_heads, mla_t, mla_nh)
        oa = mla_sample_latent(q_s, c_ckv, c_kr, ckv_b, kr_b, wuk, wuv, perm, oa, l,
                               dbatch, dseq, past, a_heads, tp)
        bias_p = _band_bias_table(rel_bias[l], band_tq, (nprev + 1) * band_tq, nprev * band_tq)
        bias_s = _band_bias_table(rel_bias[l], dseq, nband + dseq, nband)
        ob = band_prompt(qkvb, bias_p, batch, seq, b_heads, band_tq, nprev, b_scale)
        ob = band_sample(qkvb, cache_band_k, cache_band_v, bias_s, ob, l, dbatch, dseq, nband,
                         b_heads, tp, b_scale)
        x = norm_matmul([oa, ob], [g_out_a[l], g_out_b[l]], w_o_b, layer=l, res=x, name="out_proj")

        memn = rmsnorm(mem_prompt.reshape(batch * mtok, d), norm_memtok[l], BF16, name="norm_memtok")
        mkv = matmul(memn, w_mkv_b, F32, layer=l, name="mem_kv")
        qm = norm_matmul([x], [norm_mem[l]], w_mq_b, layer=l, out_dtype=BF16, name="mem_q")
        om = mem_attend(qm, mkv, mkv, 0, 1, None, nbatch=batch, rows_per_batch=seq, row0=0,
                        heads=m_heads, dim=m_dim, name="mem_prompt")
        om = mem_attend(qm, cache_mem_k, cache_mem_v, l, l, om, nbatch=dbatch, rows_per_batch=dseq,
                        row0=tp, heads=m_heads, dim=m_dim, name="mem_sample")
        x = matmul_residual(om, w_mo_b, x, layer=l, name="mem_out")

        h = rmsnorm(x, norm_ffn[l], BF16, name="norm_ffn")
        u, tail_p = ffn_up(h, w_up_b, w_dw, b_dw, state_conv, None, l, row0=0, rows=tp, seg=ffn_tm,
                           blocks_per_seq=seq // ffn_tm, use_state=False, name="ffn_up_prompt")
        u, tail_s = ffn_up(h, w_up_b, w_dw, b_dw, state_conv, u, l, row0=tp, rows=ts, seg=dseq,
                           blocks_per_seq=1, use_state=True, name="ffn_up_sample")
        y = matmul_residual_ktiled(u, w_down_b, x, tk=tk_down, layer=l, name="ffn_down")
        x = [y]

        outs["p_ckv"].append(ckv[:tp].reshape(batch, seq, kvl))
        outs["s_ckv"].append(ckv[tp:].reshape(dbatch, dseq, kvl))
        kr64 = unslot(kr)
        outs["p_kr"].append(kr64[:tp].reshape(batch, seq, rope))
        outs["s_kr"].append(kr64[tp:].reshape(dbatch, dseq, rope))
        hw = b_heads * b_dim
        nkp = batch * band_keep
        outs["p_bk"].append(kv_keep[:nkp, :hw].reshape(batch, band_keep, b_heads, b_dim))
        outs["p_bv"].append(kv_keep[:nkp, hw:].reshape(batch, band_keep, b_heads, b_dim))
        outs["s_bk"].append(kv_keep[nkp:, :hw].reshape(dbatch, dseq, b_heads, b_dim))
        outs["s_bv"].append(kv_keep[nkp:, hw:].reshape(dbatch, dseq, b_heads, b_dim))
        outs["p_mk"].append(mkv[:, :mem_w].reshape(batch, mtok, m_heads, m_dim))
        outs["p_mv"].append(mkv[:, mem_w:].reshape(batch, mtok, m_heads, m_dim))
        nblk = seq // ffn_tm
        tail_p = tail_p.reshape(batch, nblk, 8, ff)[:, nblk - 1, 8 - (CONV_W - 1):]
        outs["p_conv"].append(tail_p)
        outs["s_conv"].append(tail_s[:, 8 - (CONV_W - 1):])

    y_prompt = rmsnorm(y, norm_final, F32, row0=0, rows=tp, name="norm_final_p").reshape(batch, seq, d)
    y_sample = rmsnorm(y, norm_final, F32, row0=tp, rows=ts, name="norm_final_s").reshape(dbatch, dseq, d)
    st = {k_: jnp.stack(v_) for k_, v_ in outs.items()}
    return (y_prompt, y_sample, st["p_ckv"], st["p_kr"], st["p_bk"], st["p_bv"], st["p_mk"],
            st["p_mv"], st["p_conv"], st["s_ckv"], st["s_kr"], st["s_bk"], st["s_bv"], st["s_conv"])
```

```python
import functools
import math

import jax
import jax.numpy as jnp
import numpy as np
from jax import lax
from jax.experimental import pallas as pl
from jax.experimental.pallas import tpu as pltpu

CHUNK = 64
BAND_PREV = 8
MAX_REL = 128
CONV_W = 3
ROPE_THETA = 10000.0
EPS = 1e-6
NEG_INF = -1e30
LOG2E = math.log2(math.e)

LANES = 128
HEAD_SLOT = 256
VMEM_LIMIT = 56 * 1024 * 1024
FFN_CHUNK = 256
FFN_NCHUNK = 2

F32 = jnp.float32
BF16 = jnp.bfloat16


def _cp(*sem):
    return pltpu.CompilerParams(dimension_semantics=sem, vmem_limit_bytes=VMEM_LIMIT)


def _pick(n, prefs):
    for p in prefs:
        if n % p == 0:
            return p
    return n


def _rmsnorm_rows(x, g):
    ms = jnp.mean(x * x, axis=-1, keepdims=True)
    return (x * lax.rsqrt(ms + EPS)) * g


def _rmsnorm_kernel(x_ref, g_ref, o_ref):
    o_ref[...] = _rmsnorm_rows(x_ref[...].astype(F32), g_ref[...]).astype(o_ref.dtype)


def rmsnorm(x, g, out_dtype, *, row0=0, rows=None, name="rmsnorm"):
    m, d = x.shape
    rows = m if rows is None else rows
    tm = _pick(math.gcd(rows, row0) if row0 else rows, (256, 128, 64, 32, 16, 8))
    off = row0 // tm
    return pl.pallas_call(
        _rmsnorm_kernel,
        grid=(rows // tm,),
        in_specs=[pl.BlockSpec((tm, d), lambda i: (i + off, 0)),
                  pl.BlockSpec((1, d), lambda i: (0, 0))],
        out_specs=pl.BlockSpec((tm, d), lambda i: (i, 0)),
        out_shape=jax.ShapeDtypeStruct((rows, d), out_dtype),
        compiler_params=_cp("parallel"),
        name=name,
    )(x, g.reshape(1, d).astype(F32))


def _part_specs(parts, block, col):
    specs, off = [], 0
    for p in parts:
        nb = p.shape[0] // block[0]
        specs.append(pl.BlockSpec(
            block, lambda i, *r, off=off, nb=nb: (jnp.clip(i - off, 0, nb - 1), col(*r))))
        off += nb
    return specs


def _part_rows(refs, parts_blocks, i):
    x, bound = refs[0][...], 0
    for r, nb in zip(refs[1:], parts_blocks[:-1]):
        bound += nb
        x = jnp.where(i >= bound, r[...], x)
    return x


def _rmsnorm_parts_kernel(*refs, blocks):
    g_ref, o_ref = refs[-2:]
    x = _part_rows(refs[:-2], blocks, pl.program_id(0))
    o_ref[...] = _rmsnorm_rows(x, g_ref[...]).astype(o_ref.dtype)


def rmsnorm_parts(parts, g, out_dtype, *, name):
    d = parts[0].shape[1]
    tm = _pick(math.gcd(*[p.shape[0] for p in parts]), (256, 128, 64, 32, 16, 8))
    blocks = tuple(p.shape[0] // tm for p in parts)
    return pl.pallas_call(
        functools.partial(_rmsnorm_parts_kernel, blocks=blocks),
        grid=(sum(blocks),),
        in_specs=_part_specs(parts, (tm, d), lambda: 0) + [pl.BlockSpec((1, d), lambda i: (0, 0))],
        out_specs=pl.BlockSpec((tm, d), lambda i: (i, 0)),
        out_shape=jax.ShapeDtypeStruct((sum(blocks) * tm, d), out_dtype),
        compiler_params=_cp("parallel"),
        name=name,
    )(*parts, g.reshape(1, d).astype(F32))


def _mm_kernel(x_ref, w_ref, o_ref):
    o_ref[...] = jnp.dot(x_ref[...].astype(BF16), w_ref[...],
                         preferred_element_type=F32).astype(o_ref.dtype)


def _wspec(w, layer, block, index_map):
    if w.ndim == 2:
        return pl.BlockSpec(block, index_map)
    return pl.BlockSpec((None,) + block, lambda *a: (layer,) + index_map(*a))


def matmul(x, w, out_dtype, *, layer=0, tm_prefs=(1024, 512, 256, 128, 64, 32, 16, 8),
           tn_prefs=(512, 256, 128), name="matmul"):
    m, k = x.shape
    n = w.shape[-1]
    tm, tn = _pick(m, tm_prefs), _pick(n, tn_prefs)
    return pl.pallas_call(
        _mm_kernel,
        grid=(m // tm, n // tn),
        in_specs=[pl.BlockSpec((tm, k), lambda i, j: (i, 0)),
                  _wspec(w, layer, (k, tn), lambda i, j: (0, j))],
        out_specs=pl.BlockSpec((tm, tn), lambda i, j: (i, j)),
        out_shape=jax.ShapeDtypeStruct((m, n), out_dtype),
        compiler_params=_cp("parallel", "parallel"),
        name=name,
    )(x, w)


def _mm_res_kernel(x_ref, w_ref, r_ref, o_ref):
    o_ref[...] = r_ref[...] + jnp.dot(x_ref[...], w_ref[...], preferred_element_type=F32)


def matmul_residual(x, w, res, *, layer=0, name="matmul_res"):
    m, k = x.shape
    n = w.shape[-1]
    tm, tn = _pick(m, (1024, 512, 256, 128)), _pick(n, (1024, 512, 256, 128) if k <= 1024 else (512, 256, 128))
    return pl.pallas_call(
        _mm_res_kernel,
        grid=(m // tm, n // tn),
        in_specs=[pl.BlockSpec((tm, k), lambda i, j: (i, 0)),
                  _wspec(w, layer, (k, tn), lambda i, j: (0, j)),
                  pl.BlockSpec((tm, tn), lambda i, j: (i, j))],
        out_specs=pl.BlockSpec((tm, tn), lambda i, j: (i, j)),
        out_shape=jax.ShapeDtypeStruct((m, n), F32),
        input_output_aliases={2: 0},
        compiler_params=_cp("parallel", "parallel"),
        name=name,
    )(x, w, res)


def _mm_res_ktiled_kernel(x_ref, w_ref, r_ref, o_ref, acc_ref):
    kk = pl.program_id(2)

    @pl.when(kk == 0)
    def _():
        acc_ref[...] = jnp.zeros_like(acc_ref)

    acc_ref[...] += jnp.dot(x_ref[...], w_ref[...], preferred_element_type=F32)

    @pl.when(kk == pl.num_programs(2) - 1)
    def _():
        o_ref[...] = r_ref[...] + acc_ref[...]


def matmul_residual_ktiled(x, w, res, *, tk, layer=0, name="matmul_res_k"):
    m, k = x.shape
    n = w.shape[-1]
    tm, tn = _pick(m, (1024, 512, 256, 128)), _pick(n, (512, 256, 128))
    return pl.pallas_call(
        _mm_res_ktiled_kernel,
        grid=(m // tm, n // tn, k // tk),
        in_specs=[pl.BlockSpec((tm, tk), lambda i, j, kk: (i, kk)),
                  _wspec(w, layer, (tk, tn), lambda i, j, kk: (kk, j)),
                  pl.BlockSpec((tm, tn), lambda i, j, kk: (i, j))],
        out_specs=pl.BlockSpec((tm, tn), lambda i, j, kk: (i, j)),
        out_shape=jax.ShapeDtypeStruct((m, n), F32),
        scratch_shapes=[pltpu.VMEM((tm, tn), F32)],
        input_output_aliases={2: 0},
        compiler_params=_cp("parallel", "parallel", "arbitrary"),
        name=name,
    )(x, w, res)


def _rope_slot(r, cos, sin):
    return r * cos + pltpu.roll(r, 64, 1) * sin


def _lat_kernel(h_ref, w_ref, gq_ref, gkv_ref, cos_ref, sin_ref,
                cq_ref, ckv_ref, ckvb_ref, kr_ref, krb_ref, *, ql, kvl):
    acc = jnp.dot(h_ref[...], w_ref[...], preferred_element_type=F32)
    cq_ref[...] = _rmsnorm_rows(acc[:, :ql], gq_ref[...]).astype(BF16)
    ckv = _rmsnorm_rows(acc[:, ql:ql + kvl], gkv_ref[...])
    ckv_ref[...] = ckv
    ckvb_ref[...] = ckv.astype(BF16)
    kr = _rope_slot(acc[:, ql + kvl:], cos_ref[...], sin_ref[...])
    kr_ref[...] = kr
    krb_ref[...] = kr.astype(BF16)


def latent_project(h, w_lat, g_cq, g_ckv, cos_t, sin_t):
    m, d = h.shape
    ql, kvl = g_cq.shape[0], g_ckv.shape[0]
    n = w_lat.shape[1]
    tm = _pick(m, (512, 256, 128))
    row = lambda i: (i, 0)
    fix = lambda i: (0, 0)
    return pl.pallas_call(
        functools.partial(_lat_kernel, ql=ql, kvl=kvl),
        grid=(m // tm,),
        in_specs=[pl.BlockSpec((tm, d), row), pl.BlockSpec((d, n), fix),
                  pl.BlockSpec((1, ql), fix), pl.BlockSpec((1, kvl), fix),
                  pl.BlockSpec((tm, LANES), row), pl.BlockSpec((tm, LANES), row)],
        out_specs=[pl.BlockSpec((tm, ql), row), pl.BlockSpec((tm, kvl), row),
                   pl.BlockSpec((tm, kvl), row), pl.BlockSpec((tm, LANES), row),
                   pl.BlockSpec((tm, LANES), row)],
        out_shape=[jax.ShapeDtypeStruct((m, ql), BF16), jax.ShapeDtypeStruct((m, kvl), F32),
                   jax.ShapeDtypeStruct((m, kvl), BF16), jax.ShapeDtypeStruct((m, LANES), F32),
                   jax.ShapeDtypeStruct((m, LANES), BF16)],
        compiler_params=_cp("parallel"),
        name="latent_project",
    )(h, w_lat, g_cq.reshape(1, ql), g_ckv.reshape(1, kvl), cos_t, sin_t)


def _q_kernel(c_ref, w_ref, cos_ref, sin_ref, o_ref, *, heads, scale):
    acc = jnp.dot(c_ref[...], w_ref[...], preferred_element_type=F32)
    cos, sin = cos_ref[...], sin_ref[...]
    for h in range(heads):
        lo = h * HEAD_SLOT
        o_ref[:, lo:lo + LANES] = (acc[:, lo:lo + LANES] * scale).astype(BF16)
        r = _rope_slot(acc[:, lo + LANES:lo + HEAD_SLOT], cos, sin)
        o_ref[:, lo + LANES:lo + HEAD_SLOT] = (r * scale).astype(BF16)


def q_project(cq, w_uq, cos_t, sin_t, scale, *, row0, rows):
    ql = cq.shape[1]
    n = w_uq.shape[1]
    tm = _pick(math.gcd(rows, row0), (512, 256, 128, 64))
    tn = _pick(n, (1024, 512, 256))
    r0 = row0 // tm
    return pl.pallas_call(
        functools.partial(_q_kernel, heads=tn // HEAD_SLOT, scale=scale),
        grid=(rows // tm, n // tn),
        in_specs=[pl.BlockSpec((tm, ql), lambda i, j: (r0 + i, 0)),
                  pl.BlockSpec((ql, tn), lambda i, j: (0, j)),
                  pl.BlockSpec((tm, LANES), lambda i, j: (r0 + i, 0)),
                  pl.BlockSpec((tm, LANES), lambda i, j: (r0 + i, 0))],
        out_specs=pl.BlockSpec((tm, tn), lambda i, j: (i, j)),
        out_shape=jax.ShapeDtypeStruct((rows, n), BF16),
        compiler_params=_cp("parallel", "parallel"),
        name="q_project",
    )(cq, w_uq, cos_t, sin_t)


def _qt_kernel(c_ref, wt_ref, cos_ref, sin_ref, o_ref, *, heads, scale):
    acc = lax.dot_general(wt_ref[...], c_ref[...], (((1,), (1,)), ((), ())),
                          preferred_element_type=F32)
    cos, sin = cos_ref[...], sin_ref[...]
    for h in range(heads):
        lo = h * HEAD_SLOT
        o_ref[0, lo:lo + LANES, :] = (acc[lo:lo + LANES] * scale).astype(BF16)
        r = acc[lo + LANES:lo + HEAD_SLOT]
        r = r * cos + pltpu.roll(r, 64, 0) * sin
        o_ref[0, lo + LANES:lo + HEAD_SLOT, :] = (r * scale).astype(BF16)


def q_project_t(cq, w_uq_t, cos_tt, sin_tt, scale, *, rows, tq):
    ql = cq.shape[1]
    n = w_uq_t.shape[0]
    tn = _pick(n, (1024, 512, 256))
    return pl.pallas_call(
        functools.partial(_qt_kernel, heads=tn // HEAD_SLOT, scale=scale),
        grid=(rows // tq, n // tn),
        in_specs=[pl.BlockSpec((tq, ql), lambda i, j: (i, 0)),
                  pl.BlockSpec((tn, ql), lambda i, j: (j, 0)),
                  pl.BlockSpec((LANES, tq), lambda i, j: (0, i)),
                  pl.BlockSpec((LANES, tq), lambda i, j: (0, i))],
        out_specs=pl.BlockSpec((1, tn, tq), lambda i, j: (i, j, 0)),
        out_shape=jax.ShapeDtypeStruct((rows // tq, n, tq), BF16),
        compiler_params=_cp("parallel", "parallel"),
        name="q_project_t",
    )(cq, w_uq_t, cos_tt, sin_tt)


def _kv_kernel(c_ref, kr_ref, wk_ref, wvt_ref, k_ref, vt_ref, *, heads):
    c = c_ref[...]
    kn = jnp.dot(c, wk_ref[...], preferred_element_type=F32)
    vt_ref[0] = lax.dot_general(wvt_ref[...], c, (((1,), (1,)), ((), ())),
                                preferred_element_type=F32).astype(BF16)
    kr = kr_ref[...]
    for h in range(heads):
        lo = h * HEAD_SLOT
        k_ref[:, lo:lo + LANES] = kn[:, h * LANES:(h + 1) * LANES].astype(BF16)
        k_ref[:, lo + LANES:lo + HEAD_SLOT] = kr


def kv_decompress(ckv, kr, w_uk, w_uv_t, *, rows, tm):
    kvl = ckv.shape[1]
    heads = w_uk.shape[1] // LANES
    row = lambda i: (i, 0)
    fix = lambda i: (0, 0)
    return pl.pallas_call(
        functools.partial(_kv_kernel, heads=heads),
        grid=(rows // tm,),
        in_specs=[pl.BlockSpec((tm, kvl), row), pl.BlockSpec((tm, LANES), row),
                  pl.BlockSpec(w_uk.shape, fix), pl.BlockSpec(w_uv_t.shape, fix)],
        out_specs=[pl.BlockSpec((tm, heads * HEAD_SLOT), row),
                   pl.BlockSpec((1, heads * LANES, tm), lambda i: (i, 0, 0))],
        out_shape=[jax.ShapeDtypeStruct((rows, heads * HEAD_SLOT), BF16),
                   jax.ShapeDtypeStruct((rows // tm, heads * LANES, tm), BF16)],
        compiler_params=_cp("parallel"),
        name="kv_decompress",
    )(ckv, kr, w_uk, w_uv_t)


def _qk(q, k):
    return lax.dot_general(q, k, (((1,), (1,)), ((), ())), preferred_element_type=F32)


def _mla_prompt_kernel(qt_ref, k_ref, vt_ref, o_ref, m_ref, l_ref, acc_ref, s0_ref, s1_ref, *, t, nh):
    qi = pl.program_id(2)
    m_ref[...] = jnp.full_like(m_ref, NEG_INF)
    l_ref[...] = jnp.zeros_like(l_ref)
    acc_ref[...] = jnp.zeros_like(acc_ref)

    def scores(ki, s_ref):
        start = pl.multiple_of(ki * t, t)
        for h in range(nh):
            s_ref[h] = jnp.dot(k_ref[pl.ds(start, t), h * HEAD_SLOT:(h + 1) * HEAD_SLOT],
                               qt_ref[0, h * HEAD_SLOT:(h + 1) * HEAD_SLOT, :],
                               preferred_element_type=F32)

    def consume(ki, s_ref, mask):
        for h in range(nh):
            s = s_ref[h]
            if mask is not None:
                s = jnp.where(mask, s, NEG_INF)
            m_old = m_ref[h]
            m_new = jnp.maximum(m_old, jnp.max(s, axis=0, keepdims=True))
            alpha = jnp.exp2(m_old - m_new)
            p = jnp.exp2(s - m_new)
            l_ref[h] = alpha * l_ref[h] + jnp.sum(p, axis=0, keepdims=True)
            acc_ref[h] = alpha * acc_ref[h] + jnp.dot(
                vt_ref[ki, h * LANES:(h + 1) * LANES, :], p.astype(BF16),
                preferred_element_type=F32)
            m_ref[h] = m_new

    scores(0, s0_ref)

    def body(j, carry):
        scores(2 * j + 1, s1_ref)
        consume(2 * j, s0_ref, None)
        scores(2 * j + 2, s0_ref)
        consume(2 * j + 1, s1_ref, None)
        return carry

    lax.fori_loop(0, qi // 2, body, 0)
    kc = lax.broadcasted_iota(jnp.int32, (t, t), 0) // CHUNK
    qc = lax.broadcasted_iota(jnp.int32, (t, t), 1) // CHUNK
    mask = kc <= qc

    @pl.when(qi % 2 == 0)
    def _():
        consume(qi, s0_ref, mask)

    @pl.when(qi % 2 == 1)
    def _():
        scores(qi, s1_ref)
        consume(qi - 1, s0_ref, None)
        consume(qi, s1_ref, mask)

    for h in range(nh):
        o_ref[:, h * LANES:(h + 1) * LANES] = (acc_ref[h] / l_ref[h]).T


def mla_prompt(qt, k, vt, rows, batch, seq, heads, t, nh):
    nq = seq // t
    return pl.pallas_call(
        functools.partial(_mla_prompt_kernel, t=t, nh=nh),
        grid=(batch, heads // nh, nq),
        in_specs=[pl.BlockSpec((1, nh * HEAD_SLOT, t), lambda b, g, i: (b * nq + i, g, 0)),
                  pl.BlockSpec((seq, nh * HEAD_SLOT), lambda b, g, i: (b, g),
                               pipeline_mode=pl.Buffered(1)),
                  pl.BlockSpec((nq, nh * LANES, t), lambda b, g, i: (b, g, 0),
                               pipeline_mode=pl.Buffered(1))],
        out_specs=pl.BlockSpec((t, nh * LANES), lambda b, g, i: (b * nq + i, g)),
        out_shape=jax.ShapeDtypeStruct((rows, heads * LANES), F32),
        scratch_shapes=[pltpu.VMEM((nh, 1, t), F32), pltpu.VMEM((nh, 1, t), F32),
                        pltpu.VMEM((nh, LANES, t), F32),
                        pltpu.VMEM((nh, t, t), F32), pltpu.VMEM((nh, t, t), F32)],
        compiler_params=_cp("parallel", "parallel", "arbitrary"),
        name="mla_prompt",
    )(qt, k, vt)


def _q_absorb_kernel(q_ref, wk_ref, o_ref):
    kvl = wk_ref.shape[0]
    o_ref[0, :, :kvl] = lax.dot_general(q_ref[:, :LANES], wk_ref[...], (((1,), (1,)), ((), ())),
                                        preferred_element_type=F32).astype(BF16)
    o_ref[0, :, kvl:] = q_ref[:, LANES:]


def _mla_latent_kernel(q_ref, cc_ref, kc_ref, perm_ref, cn_ref, kn_ref, o_ref):
    heads, dseq, width = q_ref.shape
    q = q_ref[...].reshape(heads * dseq, width)
    cc = cc_ref[...].astype(BF16)
    krc = jnp.dot(kc_ref[...].astype(BF16), perm_ref[...], preferred_element_type=F32).astype(BF16)
    cn = cn_ref[...]
    s1 = _qk(q, jnp.concatenate([cc, krc], axis=1))
    s2 = _qk(q, jnp.concatenate([cn, kn_ref[...]], axis=1))
    m = jnp.maximum(jnp.max(s1, axis=-1, keepdims=True), jnp.max(s2, axis=-1, keepdims=True))
    p1 = jnp.exp2(s1 - m)
    p2 = jnp.exp2(s2 - m)
    l = jnp.sum(p1, axis=-1, keepdims=True) + jnp.sum(p2, axis=-1, keepdims=True)
    o = (jnp.dot(p1.astype(BF16), cc, preferred_element_type=F32)
         + jnp.dot(p2.astype(BF16), cn, preferred_element_type=F32)) / l
    o_ref[...] = o.astype(BF16).reshape(heads, dseq, o.shape[1])


def _o_absorb_kernel(ol_ref, wv_ref, oin_ref, o_ref):
    del oin_ref
    o_ref[...] = jnp.dot(ol_ref[0], wv_ref[...], preferred_element_type=F32)


def mla_sample_latent(q_s, c_ckv, c_kr, ckv_b, kr_b, w_uk, w_uv, perm, o_buf, layer,
                      dbatch, dseq, past, heads, row0):
    ts, kvl = q_s.shape[0], w_uk.shape[0]
    width = kvl + LANES
    q_abs = pl.pallas_call(
        _q_absorb_kernel,
        grid=(heads,),
        in_specs=[pl.BlockSpec((ts, HEAD_SLOT), lambda h: (0, h)),
                  pl.BlockSpec((kvl, LANES), lambda h: (0, h))],
        out_specs=pl.BlockSpec((1, ts, width), lambda h: (h, 0, 0)),
        out_shape=jax.ShapeDtypeStruct((heads, ts, width), BF16),
        compiler_params=_cp("parallel"),
        name="mla_q_absorb",
    )(q_s, w_uk)
    r0 = row0 // dseq
    o_lat = pl.pallas_call(
        _mla_latent_kernel,
        grid=(dbatch,),
        in_specs=[pl.BlockSpec((heads, dseq, width), lambda b: (0, b, 0)),
                  pl.BlockSpec((None, past, kvl), lambda b: (layer, b, 0)),
                  pl.BlockSpec((None, past, c_kr.shape[2]), lambda b: (layer, b, 0)),
                  pl.BlockSpec(perm.shape, lambda b: (0, 0)),
                  pl.BlockSpec((dseq, kvl), lambda b: (r0 + b, 0)),
                  pl.BlockSpec((dseq, LANES), lambda b: (r0 + b, 0))],
        out_specs=pl.BlockSpec((heads, dseq, kvl), lambda b: (0, b, 0)),
        out_shape=jax.ShapeDtypeStruct((heads, ts, kvl), BF16),
        compiler_params=_cp("parallel"),
        name="mla_sample_latent",
    )(q_abs, c_ckv, c_kr, perm, ckv_b, kr_b)
    return pl.pallas_call(
        _o_absorb_kernel,
        grid=(heads,),
        in_specs=[pl.BlockSpec((1, ts, kvl), lambda h: (h, 0, 0)),
                  pl.BlockSpec((kvl, LANES), lambda h: (0, h)),
                  pl.BlockSpec(memory_space=pl.ANY)],
        out_specs=pl.BlockSpec((ts, LANES), lambda h: (row0 // ts, h)),
        out_shape=jax.ShapeDtypeStruct(o_buf.shape, F32),
        input_output_aliases={2: 0},
        compiler_params=_cp("parallel"),
        name="mla_o_absorb",
    )(o_lat, w_uv, o_buf)


def _band_prompt_kernel(q_ref, k0_ref, k1_ref, k2_ref, v0_ref, v1_ref, v2_ref, bias_ref, o_ref,
                        *, tq, nprev, scale, nh):
    qi = pl.program_id(2)
    ks = (k0_ref, k1_ref, k2_ref)[3 - nprev - 1:]
    vs = (v0_ref, v1_ref, v2_ref)[3 - nprev - 1:]
    raw = [[_qk(q_ref[:, h * LANES:(h + 1) * LANES], kr[:, h * LANES:(h + 1) * LANES]) for kr in ks]
           for h in range(nh)]
    for h in range(nh):
        hs = slice(h * LANES, (h + 1) * LANES)
        ss = []
        for d in range(len(ks)):
            s = raw[h][d] * scale + bias_ref[h, :, d * tq:(d + 1) * tq]
            if d < nprev:
                s = jnp.where(qi - (nprev - d) >= 0, s, NEG_INF)
            ss.append(s)
        m = functools.reduce(jnp.maximum, [jnp.max(s, axis=-1, keepdims=True) for s in ss])
        ps = [jnp.exp(s - m) for s in ss]
        l = functools.reduce(lambda a, b: a + b, [jnp.sum(p, axis=-1, keepdims=True) for p in ps])
        o = functools.reduce(lambda a, b: a + b,
                             [jnp.dot(p.astype(BF16), vr[:, hs], preferred_element_type=F32)
                              for p, vr in zip(ps, vs)])
        o_ref[:, hs] = o / l


def band_prompt(qkv, bias, batch, seq, heads, tq, nprev, scale):
    t = qkv.shape[0]
    nq = seq // tq
    nh = _pick(heads, (4, 2, 1))
    ng = heads // nh

    def kspec(back, col0):
        return pl.BlockSpec((tq, nh * LANES),
                            lambda b, g, i: (b * nq + jnp.maximum(i - back, 0), col0 + g))

    return pl.pallas_call(
        functools.partial(_band_prompt_kernel, tq=tq, nprev=nprev, scale=scale, nh=nh),
        grid=(batch, ng, nq),
        in_specs=[pl.BlockSpec((tq, nh * LANES), lambda b, g, i: (b * nq + i, g)),
                  kspec(2, ng), kspec(1, ng), kspec(0, ng),
                  kspec(2, 2 * ng), kspec(1, 2 * ng), kspec(0, 2 * ng),
                  pl.BlockSpec((nh, tq, (nprev + 1) * tq), lambda b, g, i: (g, 0, 0))],
        out_specs=pl.BlockSpec((tq, nh * LANES), lambda b, g, i: (b * nq + i, g)),
        out_shape=jax.ShapeDtypeStruct((t, heads * LANES), F32),
        compiler_params=_cp("parallel", "parallel", "arbitrary"),
        name="band_prompt",
    )(qkv, qkv, qkv, qkv, qkv, qkv, qkv, bias)


def _band_sample_kernel(q_ref, kc_ref, vc_ref, kn_ref, vn_ref, bias_ref, oin_ref, o_ref,
                        *, nb, scale, nh):
    del oin_ref
    for h in range(nh):
        hs = slice(h * LANES, (h + 1) * LANES)
        q = q_ref[:, hs]
        s1 = _qk(q, kc_ref[:, h, :].astype(BF16)) * scale + bias_ref[h, :, :nb]
        s2 = _qk(q, kn_ref[:, hs]) * scale + bias_ref[h, :, nb:]
        m = jnp.maximum(jnp.max(s1, axis=-1, keepdims=True), jnp.max(s2, axis=-1, keepdims=True))
        p1 = jnp.exp(s1 - m)
        p2 = jnp.exp(s2 - m)
        l = jnp.sum(p1, axis=-1, keepdims=True) + jnp.sum(p2, axis=-1, keepdims=True)
        o = (jnp.dot(p1.astype(BF16), vc_ref[:, h, :].astype(BF16), preferred_element_type=F32)
             + jnp.dot(p2.astype(BF16), vn_ref[:, hs], preferred_element_type=F32))
        o_ref[:, hs] = o / l


def band_sample(qkv, kc, vc, bias, o_buf, layer, dbatch, dseq, nb, heads, row0, scale):
    r0 = row0 // dseq
    nh, ng = heads, 1
    w = nh * LANES
    cspec = pl.BlockSpec((None, None, nb, heads, LANES), lambda b, g: (layer, b, 0, 0, 0))
    return pl.pallas_call(
        functools.partial(_band_sample_kernel, nb=nb, scale=scale, nh=nh),
        grid=(dbatch, ng),
        in_specs=[pl.BlockSpec((dseq, w), lambda b, g: (r0 + b, g)),
                  cspec, cspec,
                  pl.BlockSpec((dseq, w), lambda b, g: (r0 + b, ng + g)),
                  pl.BlockSpec((dseq, w), lambda b, g: (r0 + b, 2 * ng + g)),
                  pl.BlockSpec((nh, dseq, nb + dseq), lambda b, g: (g, 0, 0)),
                  pl.BlockSpec(memory_space=pl.ANY)],
        out_specs=pl.BlockSpec((dseq, w), lambda b, g: (r0 + b, g)),
        out_shape=jax.ShapeDtypeStruct(o_buf.shape, F32),
        input_output_aliases={6: 0},
        compiler_params=_cp("parallel", "parallel"),
        name="band_sample",
    )(qkv, kc, vc, qkv, qkv, bias, o_buf)


def _band_bias_table(rel_bias, nq, nk, q0):
    span = nq + nk - 1
    k = np.concatenate([np.arange(0, nk), np.arange(-(nq - 1), 0)])
    idx = np.clip(q0 - k, -MAX_REL, MAX_REL) + MAX_REL
    u = rel_bias.astype(F32)[:, idx]
    tab = jnp.tile(u, (1, nq))[:, :nq * (span - 1)].reshape(-1, nq, span - 1)[:, :, :nk]
    qc = (q0 + np.arange(nq))[:, None] // CHUNK
    kc = np.arange(nk)[None, :] // CHUNK
    mask = (kc <= qc) & (kc >= qc - BAND_PREV)
    return jnp.where(mask[None], tab, NEG_INF)


def _norm_mm_kernel(*refs, nparts, res_blocks):
    xs, gs = refs[:nparts], refs[nparts:2 * nparts]
    w_ref = refs[2 * nparts]
    r_refs = refs[2 * nparts + 1:-2]
    o_ref, hs_ref = refs[-2:]

    @pl.when(pl.program_id(1) == 0)
    def _():
        off = 0
        for x_ref, g_ref in zip(xs, gs):
            hp = _rmsnorm_rows(x_ref[...], g_ref[...]).astype(BF16)
            hs_ref[:, off:off + hp.shape[1]] = hp
            off += hp.shape[1]

    acc = jnp.dot(hs_ref[...], w_ref[...], preferred_element_type=F32)
    if r_refs:
        o_ref[...] = _part_rows(r_refs, res_blocks, pl.program_id(0)) + acc
    else:
        o_ref[...] = acc.astype(o_ref.dtype)


def norm_matmul(parts, gains, w, *, layer=0, res=(), out_dtype=F32, name):
    m = parts[0].shape[0]
    widths = [p.shape[1] for p in parts]
    k, n = sum(widths), w.shape[-1]
    tm, tn = _pick(math.gcd(m, *[r.shape[0] for r in res]), (512, 256, 128)), _pick(n, (1024, 512, 256, 128))
    in_specs = ([pl.BlockSpec((tm, dp), lambda i, j: (i, 0)) for dp in widths]
                + [pl.BlockSpec((1, dp), lambda i, j: (0, 0)) for dp in widths]
                + [_wspec(w, layer, (k, tn), lambda i, j: (0, j))]
                + _part_specs(res, (tm, tn), lambda j: j))
    args = list(parts) + [g.reshape(1, -1).astype(F32) for g in gains] + [w] + list(res)
    alias = {len(args) - 1: 0} if len(res) == 1 else {}
    return pl.pallas_call(
        functools.partial(_norm_mm_kernel, nparts=len(parts),
                          res_blocks=tuple(r.shape[0] // tm for r in res)),
        grid=(m // tm, n // tn),
        in_specs=in_specs,
        out_specs=pl.BlockSpec((tm, tn), lambda i, j: (i, j)),
        out_shape=jax.ShapeDtypeStruct((m, n), F32 if res else out_dtype),
        scratch_shapes=[pltpu.VMEM((tm, k), BF16)],
        input_output_aliases=alias,
        compiler_params=_cp("parallel", "arbitrary"),
        name=name,
    )(*args)


def _mem_kernel(q_ref, k_ref, v_ref, *rest, heads, dim, scale):
    o_ref = rest[-1]
    for h in range(heads):
        sl = slice(h * dim, (h + 1) * dim)
        k = k_ref[:, sl] if len(k_ref.shape) == 2 else k_ref[:, h, :]
        v = v_ref[:, sl] if len(v_ref.shape) == 2 else v_ref[:, h, :]
        s = _qk(q_ref[:, sl], k.astype(BF16)) * scale
        m = jnp.max(s, axis=-1, keepdims=True)
        p = jnp.exp(s - m)
        l = jnp.sum(p, axis=-1, keepdims=True)
        o = jnp.dot(p.astype(BF16), v.astype(BF16), preferred_element_type=F32)
        o_ref[:, sl] = (o / l).astype(BF16)


def mem_attend(q, mk, mv, kcol, vcol, o_buf, *, nbatch, rows_per_batch, row0, heads, dim, name):
    t, w = q.shape
    tq = _pick(rows_per_batch, (512, 256, 128, 64))
    nq = rows_per_batch // tq
    r0 = row0 // tq
    if mk.ndim == 2:
        mtok = mk.shape[0] // nbatch
        kspec = pl.BlockSpec((mtok, w), lambda b, i: (b, kcol))
        vspec = pl.BlockSpec((mtok, w), lambda b, i: (b, vcol))
    else:
        kspec = vspec = pl.BlockSpec((None, None) + mk.shape[2:], lambda b, i: (kcol, b, 0, 0, 0))
    in_specs = [pl.BlockSpec((tq, w), lambda b, i: (r0 + b * nq + i, 0)), kspec, vspec]
    args = [q, mk, mv]
    alias = {}
    if o_buf is not None:
        in_specs.append(pl.BlockSpec(memory_space=pl.ANY))
        args.append(o_buf)
        alias = {3: 0}
    return pl.pallas_call(
        functools.partial(_mem_kernel, heads=heads, dim=dim, scale=dim ** -0.5),
        grid=(nbatch, nq),
        in_specs=in_specs,
        out_specs=pl.BlockSpec((tq, w), lambda b, i: (r0 + b * nq + i, 0)),
        out_shape=jax.ShapeDtypeStruct((t, w), BF16),
        input_output_aliases=alias,
        compiler_params=_cp("parallel", "parallel"),
        name=name,
    )(*args)


def _ffn_up_kernel(x_ref, *refs, seg, nseg, nch, blocks_per_seq, use_state, has_buf):
    wa_refs, wg_refs = refs[:nch], refs[nch:2 * nch]
    wdw_ref, bdw_ref, st_ref = refs[2 * nch:2 * nch + 3]
    rest = refs[2 * nch + 3 + (1 if has_buf else 0):]
    if use_state:
        u_ref, tail_ref = rest
        carry_ref = None
    else:
        u_ref, tail_ref, carry_ref = rest
    i, j = pl.program_id(0), pl.program_id(1)
    tc = FFN_CHUNK
    row = lax.broadcasted_iota(jnp.int32, (8, tc), 0)
    if not use_state:
        @pl.when(i % blocks_per_seq == 0)
        def _():
            carry_ref[j] = jnp.zeros(carry_ref.shape[1:], F32)
    for c in range(nch):
        cs = slice(c * tc, (c + 1) * tc)
        a = jnp.dot(x_ref[...], wa_refs[c][...], preferred_element_type=F32)
        g = jnp.dot(x_ref[...], wg_refs[c][...], preferred_element_type=F32)
        w0, w1, w2 = wdw_ref[0:1, cs], wdw_ref[1:2, cs], wdw_ref[2:3, cs]
        for s in range(nseg):
            gs = g[s * seg:(s + 1) * seg]
            if use_state:
                p2, p1 = st_ref[s, 0:1, cs], st_ref[s, 1:2, cs]
            else:
                p2, p1 = carry_ref[j, 6:7, cs], carry_ref[j, 7:8, cs]
            prev8 = jnp.where(row == 6, p2, jnp.where(row == 7, p1, 0.0))
            g3 = gs.reshape(seg // 8, 8, tc)
            shifted = []
            for sh in (1, 2):
                cur = pltpu.roll(g3, sh, 1)
                before = jnp.concatenate([pltpu.roll(prev8, sh, 0)[None], cur[:-1]], axis=0)
                shifted.append(jnp.where(row[None] >= sh, cur, before).reshape(seg, tc))
            gm1, gm2 = shifted
            gc = ((bdw_ref[:, cs] + w0 * gm2) + w1 * gm1) + w2 * gs
            u_ref[s * seg:(s + 1) * seg, cs] = (
                a[s * seg:(s + 1) * seg] * (gc * jax.nn.sigmoid(gc))).astype(BF16)
            tail = gs[seg - 8:seg]
            tail_ref[s, :, cs] = tail
            if not use_state:
                carry_ref[j, :, cs] = tail


def ffn_up(h, w_up, w_dw, b_dw, state, u_buf, layer, *, row0, rows, seg, blocks_per_seq, use_state,
           name):
    t, d = h.shape
    ff = w_dw.shape[2]
    nblk = ff // FFN_CHUNK
    assert nblk * FFN_CHUNK == ff
    nch = FFN_NCHUNK if (u_buf is None and ff >= FFN_NCHUNK * FFN_CHUNK) else 1
    tn = nch * FFN_CHUNK
    if use_state:
        tm, nseg = rows, rows // seg
    else:
        tm, nseg = seg, 1
    r0 = row0 // tm
    ni, nj = rows // tm, pl.cdiv(ff, tn)
    scratch = [] if use_state else [pltpu.VMEM((nj, 8, tn), F32)]

    def wspec(half, c):
        return pl.BlockSpec(
            (None, d, FFN_CHUNK),
            lambda i, j: (layer, 0, jnp.minimum(half * nblk + j * nch + c, 2 * nblk - 1)))

    in_specs = ([pl.BlockSpec((tm, d), lambda i, j: (r0 + i, 0))]
                + [wspec(0, c) for c in range(nch)] + [wspec(1, c) for c in range(nch)]
                + [pl.BlockSpec((None, CONV_W, tn), lambda i, j: (layer, 0, j)),
                   pl.BlockSpec((None, 1, tn), lambda i, j: (layer, 0, j)),
                   pl.BlockSpec((None, state.shape[1], CONV_W - 1, tn),
                                lambda i, j: (layer, 0, 0, j))])
    args = [h] + [w_up] * (2 * nch) + [w_dw, b_dw.reshape(b_dw.shape[0], 1, ff), state]
    alias = {}
    if u_buf is not None:
        alias = {len(args): 0}
        in_specs.append(pl.BlockSpec(memory_space=pl.ANY))
        args.append(u_buf)
    return pl.pallas_call(
        functools.partial(_ffn_up_kernel, seg=seg, nseg=nseg, nch=nch,
                          blocks_per_seq=blocks_per_seq, use_state=use_state,
                          has_buf=u_buf is not None),
        grid=(ni, nj),
        in_specs=in_specs,
        out_specs=[pl.BlockSpec((tm, tn), lambda i, j: (r0 + i, j)),
                   pl.BlockSpec((nseg, 8, tn), lambda i, j: (i, 0, j))],
        out_shape=[jax.ShapeDtypeStruct((t, ff), BF16),
                   jax.ShapeDtypeStruct((ni * nseg, 8, ff), F32)],
        scratch_shapes=scratch,
        input_output_aliases=alias,
        compiler_params=_cp("arbitrary", "arbitrary"),
        name=name,
    )(*args)


def _rope_tables(pos):
    half = 32
    inv = ROPE_THETA ** (-jnp.arange(half, dtype=F32) / half)
    ang = pos.astype(F32)[:, None] * inv[None, :]
    c, s, z = jnp.cos(ang), jnp.sin(ang), jnp.zeros_like(ang)
    return jnp.concatenate([c, z, c, z], axis=1), jnp.concatenate([-s, z, s, z], axis=1)


def _slot_cols(w, half):
    z = jnp.zeros(w.shape[:-1] + (LANES // 2 - half,), w.dtype)
    return jnp.concatenate([w[..., :half], z, w[..., half:], z], axis=-1)


def kernel(x_prompt, x_sample, mem_prompt, cache_mla_ckv, cache_mla_krope, cache_band_k, cache_band_v, cache_mem_k, cache_mem_v, state_conv, norm_mix, w_in, norm_cq, norm_ckv, w_uq, w_uk, w_uv, rel_bias, g_out_a, g_out_b, w_o, norm_mem, norm_memtok, w_mq, w_mkv, w_mo, norm_ffn, w_up, w_dw, b_dw, w_down, norm_final):
    batch, seq, d = x_prompt.shape
    dbatch, dseq, _ = x_sample.shape
    depth = norm_mix.shape[0]
    past = cache_mla_ckv.shape[2]
    nband = cache_band_k.shape[2]
    ql, kvl = norm_cq.shape[1], norm_ckv.shape[1]
    rope = cache_mla_krope.shape[3]
    half = rope // 2
    a_heads, a_nope = w_uk.shape[2], w_uk.shape[3]
    a_vdim = w_uv.shape[3]
    b_heads, b_dim = cache_band_k.shape[3], cache_band_k.shape[4]
    mtok, m_heads, m_dim = cache_mem_k.shape[2], cache_mem_k.shape[3], cache_mem_k.shape[4]
    mem_w = m_heads * m_dim
    ff = b_dw.shape[1]
    assert a_nope == LANES and a_vdim == LANES and b_dim == LANES and rope == LANES // 2
    assert seq % CHUNK == 0 and dseq == CHUNK and past % CHUNK == 0

    tp, ts = batch * seq, dbatch * dseq
    t = tp + ts
    band_keep = min(BAND_PREV * CHUNK, seq)
    mla_scale = (a_nope + rope) ** -0.5
    b_scale = b_dim ** -0.5

    pos = jnp.concatenate([jnp.tile(jnp.arange(seq, dtype=jnp.int32), batch),
                           jnp.tile(past + jnp.arange(dseq, dtype=jnp.int32), dbatch)])
    cos_t, sin_t = _rope_tables(pos)
    band_tq = _pick(seq, (256, 128, 64))
    nprev = (BAND_PREV * CHUNK) // band_tq
    assert nprev * band_tq == BAND_PREV * CHUNK and nprev <= 2
    perm = _slot_cols(jnp.eye(rope, dtype=BF16), half)
    tk_down = _pick(ff, (5504, 2816, 2048, 1024, 512, 256))
    ffn_tm = _pick(math.gcd(seq, 1024), (1024, 512, 256, 128))
    mla_t = _pick(seq, (512, 256, 128))
    mla_nh = _pick(a_heads, (4, 2, 1))
    cos_tt, sin_tt = cos_t[:tp].T, sin_t[:tp].T

    w_up_b, w_down_b, w_o_b = w_up.astype(BF16), w_down.astype(BF16), w_o.astype(BF16)
    w_mq_b, w_mkv_b, w_mo_b = w_mq.astype(BF16), w_mkv.astype(BF16), w_mo.astype(BF16)
    c_ckv = cache_mla_ckv.reshape(depth, dbatch * past, kvl)
    c_kr = cache_mla_krope.reshape(depth, dbatch * past, rope)

    x = [x_prompt.reshape(tp, d), x_sample.reshape(ts, d)]
    outs = {k: [] for k in ("p_ckv", "p_kr", "p_bk", "p_bv", "p_mk", "p_mv", "p_conv",
                            "s_ckv", "s_kr", "s_bk", "s_bv", "s_conv")}

    def unslot(kr):
        return jnp.concatenate([kr[:, :half], kr[:, LANES // 2:LANES // 2 + half]], axis=1)

    for l in range(depth):
        wi = w_in[l]
        w_lat = jnp.concatenate([wi[:, :ql + kvl], _slot_cols(wi[:, ql + kvl:ql + kvl + rope], half)],
                                axis=1).astype(BF16)
        w_qkvb = wi[:, ql + kvl + rope:].astype(BF16)
        wq = w_uq[l].reshape(ql, a_heads, a_nope + rope)
        wq = jnp.concatenate([wq[..., :a_nope], _slot_cols(wq[..., a_nope:], half)], axis=-1)
        wq = wq.reshape(ql, a_heads * HEAD_SLOT).astype(BF16)
        wq_t = wq.T
        wuk = w_uk[l].reshape(kvl, a_heads * a_nope).astype(BF16)
        wuv = w_uv[l].reshape(kvl, a_heads * a_vdim).astype(BF16)
        wuv_t = wuv.T

        h = rmsnorm_parts(x, norm_mix[l], BF16, name="norm_mix")
        cq, ckv, ckv_b, kr, kr_b = latent_project(h, w_lat, norm_cq[l], norm_ckv[l], cos_t, sin_t)
        qkvb = matmul(h, w_qkvb, BF16, name="qkv_band")
        sel = jnp.concatenate([h[b * seq + seq - band_keep:(b + 1) * seq] for b in range(batch)]
                              + [h[tp:]], axis=0)
        kv_keep = matmul(sel, w_qkvb[:, b_heads * b_dim:], F32, name="kv_band_keep")
        q_t = q_project_t(cq, wq_t, cos_tt, sin_tt, mla_scale * LOG2E, rows=tp, tq=mla_t)
        q_s = q_project(cq, wq, cos_t, sin_t, mla_scale * LOG2E, row0=tp, rows=ts)
        k, v_t = kv_decompress(ckv_b, kr_b, wuk, wuv_t, rows=tp, tm=mla_t)
        oa = mla_prompt(q_t, k, v_t, t, batch, seq, a_heads, mla_t, mla_nh)
        oa = mla_sample_latent(q_s, c_ckv, c_kr, ckv_b, kr_b, wuk, wuv, perm, oa, l,
                               dbatch, dseq, past, a_heads, tp)
        bias_p = _band_bias_table(rel_bias[l], band_tq, (nprev + 1) * band_tq, nprev * band_tq)
        bias_s = _band_bias_table(rel_bias[l], dseq, nband + dseq, nband)
        ob = band_prompt(qkvb, bias_p, batch, seq, b_heads, band_tq, nprev, b_scale)
        ob = band_sample(qkvb, cache_band_k, cache_band_v, bias_s, ob, l, dbatch, dseq, nband,
                         b_heads, tp, b_scale)
        x = norm_matmul([oa, ob], [g_out_a[l], g_out_b[l]], w_o_b, layer=l, res=x, name="out_proj")

        memn = rmsnorm(mem_prompt.reshape(batch * mtok, d), norm_memtok[l], BF16, name="norm_memtok")
        mkv = matmul(memn, w_mkv_b, F32, layer=l, name="mem_kv")
        qm = norm_matmul([x], [norm_mem[l]], w_mq_b, layer=l, out_dtype=BF16, name="mem_q")
        om = mem_attend(qm, mkv, mkv, 0, 1, None, nbatch=batch, rows_per_batch=seq, row0=0,
                        heads=m_heads, dim=m_dim, name="mem_prompt")
        om = mem_attend(qm, cache_mem_k, cache_mem_v, l, l, om, nbatch=dbatch, rows_per_batch=dseq,
                        row0=tp, heads=m_heads, dim=m_dim, name="mem_sample")
        x = matmul_residual(om, w_mo_b, x, layer=l, name="mem_out")

        h = rmsnorm(x, norm_ffn[l], BF16, name="norm_ffn")
        u, tail_p = ffn_up(h, w_up_b, w_dw, b_dw, state_conv, None, l, row0=0, rows=tp, seg=ffn_tm,
                           blocks_per_seq=seq // ffn_tm, use_state=False, name="ffn_up_prompt")
        u, tail_s = ffn_up(h, w_up_b, w_dw, b_dw, state_conv, u, l, row0=tp, rows=ts, seg=dseq,
                           blocks_per_seq=1, use_state=True, name="ffn_up_sample")
        y = matmul_residual_ktiled(u, w_down_b, x, tk=tk_down, layer=l, name="ffn_down")
        x = [y]

        outs["p_ckv"].append(ckv[:tp].reshape(batch, seq, kvl))
        outs["s_ckv"].append(ckv[tp:].reshape(dbatch, dseq, kvl))
        kr64 = unslot(kr)
        outs["p_kr"].append(kr64[:tp].reshape(batch, seq, rope))
        outs["s_kr"].append(kr64[tp:].reshape(dbatch, dseq, rope))
        hw = b_heads * b_dim
        nkp = batch * band_keep
        outs["p_bk"].append(kv_keep[:nkp, :hw].reshape(batch, band_keep, b_heads, b_dim))
        outs["p_bv"].append(kv_keep[:nkp, hw:].reshape(batch, band_keep, b_heads, b_dim))
        outs["s_bk"].append(kv_keep[nkp:, :hw].reshape(dbatch, dseq, b_heads, b_dim))
        outs["s_bv"].append(kv_keep[nkp:, hw:].reshape(dbatch, dseq, b_heads, b_dim))
        outs["p_mk"].append(mkv[:, :mem_w].reshape(batch, mtok, m_heads, m_dim))
        outs["p_mv"].append(mkv[:, mem_w:].reshape(batch, mtok, m_heads, m_dim))
        nblk = seq // ffn_tm
        tail_p = tail_p.reshape(batch, nblk, 8, ff)[:, nblk - 1, 8 - (CONV_W - 1):]
        outs["p_conv"].append(tail_p)
        outs["s_conv"].append(tail_s[:, 8 - (CONV_W - 1):])

    y_prompt = rmsnorm(y, norm_final, F32, row0=0, rows=tp, name="norm_final_p").reshape(batch, seq, d)
    y_sample = rmsnorm(y, norm_final, F32, row0=tp, rows=ts, name="norm_final_s").reshape(dbatch, dseq, d)
    st = {k_: jnp.stack(v_) for k_, v_ in outs.items()}
    return (y_prompt, y_sample, st["p_ckv"], st["p_kr"], st["p_bk"], st["p_bv"], st["p_mk"],
            st["p_mv"], st["p_conv"], st["s_ckv"], st["s_kr"], st["s_bk"], st["s_bv"], st["s_conv"])
```

```python
import functools
import math

import jax
import jax.numpy as jnp
import numpy as np
from jax import lax
from jax.experimental import pallas as pl
from jax.experimental.pallas import tpu as pltpu

CHUNK = 64
BAND_PREV = 8
MAX_REL = 128
CONV_W = 3
ROPE_THETA = 10000.0
EPS = 1e-6
NEG_INF = -1e30
LOG2E = math.log2(math.e)

LANES = 128
HEAD_SLOT = 256
VT_SLOT = 144
VMEM_LIMIT = 56 * 1024 * 1024
FFN_CHUNK = 256
FFN_NCHUNK = 2

F32 = jnp.float32
BF16 = jnp.bfloat16


def _cp(*sem):
    return pltpu.CompilerParams(dimension_semantics=sem, vmem_limit_bytes=VMEM_LIMIT)


def _pick(n, prefs):
    for p in prefs:
        if n % p == 0:
            return p
    return n


def _rmsnorm_rows(x, g):
    ms = jnp.mean(x * x, axis=-1, keepdims=True)
    return (x * lax.rsqrt(ms + EPS)) * g


def _rmsnorm_kernel(x_ref, g_ref, o_ref):
    o_ref[...] = _rmsnorm_rows(x_ref[...].astype(F32), g_ref[...]).astype(o_ref.dtype)


def rmsnorm(x, g, out_dtype, *, row0=0, rows=None, name="rmsnorm"):
    m, d = x.shape
    rows = m if rows is None else rows
    tm = _pick(math.gcd(rows, row0) if row0 else rows, (256, 128, 64, 32, 16, 8))
    off = row0 // tm
    return pl.pallas_call(
        _rmsnorm_kernel,
        grid=(rows // tm,),
        in_specs=[pl.BlockSpec((tm, d), lambda i: (i + off, 0)),
                  pl.BlockSpec((1, d), lambda i: (0, 0))],
        out_specs=pl.BlockSpec((tm, d), lambda i: (i, 0)),
        out_shape=jax.ShapeDtypeStruct((rows, d), out_dtype),
        compiler_params=_cp("parallel"),
        name=name,
    )(x, g.reshape(1, d).astype(F32))


def _part_specs(parts, block, col):
    specs, off = [], 0
    for p in parts:
        nb = p.shape[0] // block[0]
        specs.append(pl.BlockSpec(
            block, lambda i, *r, off=off, nb=nb: (jnp.clip(i - off, 0, nb - 1), col(*r))))
        off += nb
    return specs


def _part_rows(refs, parts_blocks, i):
    x, bound = refs[0][...], 0
    for r, nb in zip(refs[1:], parts_blocks[:-1]):
        bound += nb
        x = jnp.where(i >= bound, r[...], x)
    return x


def _rmsnorm_parts_kernel(*refs, blocks):
    g_ref, o_ref = refs[-2:]
    x = _part_rows(refs[:-2], blocks, pl.program_id(0))
    o_ref[...] = _rmsnorm_rows(x, g_ref[...]).astype(o_ref.dtype)


def rmsnorm_parts(parts, g, out_dtype, *, name):
    d = parts[0].shape[1]
    tm = _pick(math.gcd(*[p.shape[0] for p in parts]), (256, 128, 64, 32, 16, 8))
    blocks = tuple(p.shape[0] // tm for p in parts)
    return pl.pallas_call(
        functools.partial(_rmsnorm_parts_kernel, blocks=blocks),
        grid=(sum(blocks),),
        in_specs=_part_specs(parts, (tm, d), lambda: 0) + [pl.BlockSpec((1, d), lambda i: (0, 0))],
        out_specs=pl.BlockSpec((tm, d), lambda i: (i, 0)),
        out_shape=jax.ShapeDtypeStruct((sum(blocks) * tm, d), out_dtype),
        compiler_params=_cp("parallel"),
        name=name,
    )(*parts, g.reshape(1, d).astype(F32))


def _mm_kernel(x_ref, w_ref, o_ref):
    o_ref[...] = jnp.dot(x_ref[...].astype(BF16), w_ref[...],
                         preferred_element_type=F32).astype(o_ref.dtype)


def _wspec(w, layer, block, index_map):
    if w.ndim == 2:
        return pl.BlockSpec(block, index_map)
    return pl.BlockSpec((None,) + block, lambda *a: (layer,) + index_map(*a))


def matmul(x, w, out_dtype, *, layer=0, tm_prefs=(1024, 512, 256, 128, 64, 32, 16, 8),
           tn_prefs=(512, 256, 128), name="matmul"):
    m, k = x.shape
    n = w.shape[-1]
    tm, tn = _pick(m, tm_prefs), _pick(n, tn_prefs)
    return pl.pallas_call(
        _mm_kernel,
        grid=(m // tm, n // tn),
        in_specs=[pl.BlockSpec((tm, k), lambda i, j: (i, 0)),
                  _wspec(w, layer, (k, tn), lambda i, j: (0, j))],
        out_specs=pl.BlockSpec((tm, tn), lambda i, j: (i, j)),
        out_shape=jax.ShapeDtypeStruct((m, n), out_dtype),
        compiler_params=_cp("parallel", "parallel"),
        name=name,
    )(x, w)


def _mm_res_kernel(x_ref, w_ref, r_ref, o_ref):
    o_ref[...] = r_ref[...] + jnp.dot(x_ref[...], w_ref[...], preferred_element_type=F32)


def matmul_residual(x, w, res, *, layer=0, name="matmul_res"):
    m, k = x.shape
    n = w.shape[-1]
    tm, tn = _pick(m, (1024, 512, 256, 128)), _pick(n, (1024, 512, 256, 128) if k <= 1024 else (512, 256, 128))
    return pl.pallas_call(
        _mm_res_kernel,
        grid=(m // tm, n // tn),
        in_specs=[pl.BlockSpec((tm, k), lambda i, j: (i, 0)),
                  _wspec(w, layer, (k, tn), lambda i, j: (0, j)),
                  pl.BlockSpec((tm, tn), lambda i, j: (i, j))],
        out_specs=pl.BlockSpec((tm, tn), lambda i, j: (i, j)),
        out_shape=jax.ShapeDtypeStruct((m, n), F32),
        input_output_aliases={2: 0},
        compiler_params=_cp("parallel", "parallel"),
        name=name,
    )(x, w, res)


def _mm_res_ktiled_kernel(x_ref, w_ref, r_ref, o_ref, acc_ref):
    kk = pl.program_id(2)

    @pl.when(kk == 0)
    def _():
        acc_ref[...] = jnp.zeros_like(acc_ref)

    acc_ref[...] += jnp.dot(x_ref[...], w_ref[...], preferred_element_type=F32)

    @pl.when(kk == pl.num_programs(2) - 1)
    def _():
        o_ref[...] = r_ref[...] + acc_ref[...]


def matmul_residual_ktiled(x, w, res, *, tk, layer=0, name="matmul_res_k"):
    m, k = x.shape
    n = w.shape[-1]
    tm, tn = _pick(m, (1024, 512, 256, 128)), _pick(n, (512, 256, 128))
    return pl.pallas_call(
        _mm_res_ktiled_kernel,
        grid=(m // tm, n // tn, k // tk),
        in_specs=[pl.BlockSpec((tm, tk), lambda i, j, kk: (i, kk)),
                  _wspec(w, layer, (tk, tn), lambda i, j, kk: (kk, j)),
                  pl.BlockSpec((tm, tn), lambda i, j, kk: (i, j))],
        out_specs=pl.BlockSpec((tm, tn), lambda i, j, kk: (i, j)),
        out_shape=jax.ShapeDtypeStruct((m, n), F32),
        scratch_shapes=[pltpu.VMEM((tm, tn), F32)],
        input_output_aliases={2: 0},
        compiler_params=_cp("parallel", "parallel", "arbitrary"),
        name=name,
    )(x, w, res)


def _rope_slot(r, cos, sin):
    return r * cos + pltpu.roll(r, 64, 1) * sin


def _lat_kernel(h_ref, w_ref, gq_ref, gkv_ref, cos_ref, sin_ref,
                cq_ref, ckv_ref, ckvb_ref, kr_ref, krb_ref, *, ql, kvl):
    acc = jnp.dot(h_ref[...], w_ref[...], preferred_element_type=F32)
    cq_ref[...] = _rmsnorm_rows(acc[:, :ql], gq_ref[...]).astype(BF16)
    ckv = _rmsnorm_rows(acc[:, ql:ql + kvl], gkv_ref[...])
    ckv_ref[...] = ckv
    ckvb_ref[...] = ckv.astype(BF16)
    kr = _rope_slot(acc[:, ql + kvl:], cos_ref[...], sin_ref[...])
    kr_ref[...] = kr
    krb_ref[...] = kr.astype(BF16)


def latent_project(h, w_lat, g_cq, g_ckv, cos_t, sin_t):
    m, d = h.shape
    ql, kvl = g_cq.shape[0], g_ckv.shape[0]
    n = w_lat.shape[1]
    tm = _pick(m, (512, 256, 128))
    row = lambda i: (i, 0)
    fix = lambda i: (0, 0)
    return pl.pallas_call(
        functools.partial(_lat_kernel, ql=ql, kvl=kvl),
        grid=(m // tm,),
        in_specs=[pl.BlockSpec((tm, d), row), pl.BlockSpec((d, n), fix),
                  pl.BlockSpec((1, ql), fix), pl.BlockSpec((1, kvl), fix),
                  pl.BlockSpec((tm, LANES), row), pl.BlockSpec((tm, LANES), row)],
        out_specs=[pl.BlockSpec((tm, ql), row), pl.BlockSpec((tm, kvl), row),
                   pl.BlockSpec((tm, kvl), row), pl.BlockSpec((tm, LANES), row),
                   pl.BlockSpec((tm, LANES), row)],
        out_shape=[jax.ShapeDtypeStruct((m, ql), BF16), jax.ShapeDtypeStruct((m, kvl), F32),
                   jax.ShapeDtypeStruct((m, kvl), BF16), jax.ShapeDtypeStruct((m, LANES), F32),
                   jax.ShapeDtypeStruct((m, LANES), BF16)],
        compiler_params=_cp("parallel"),
        name="latent_project",
    )(h, w_lat, g_cq.reshape(1, ql), g_ckv.reshape(1, kvl), cos_t, sin_t)


def _q_kernel(c_ref, w_ref, cos_ref, sin_ref, o_ref, *, heads, scale):
    acc = jnp.dot(c_ref[...], w_ref[...], preferred_element_type=F32)
    cos, sin = cos_ref[...], sin_ref[...]
    for h in range(heads):
        lo = h * HEAD_SLOT
        o_ref[:, lo:lo + LANES] = (acc[:, lo:lo + LANES] * scale).astype(BF16)
        r = _rope_slot(acc[:, lo + LANES:lo + HEAD_SLOT], cos, sin)
        o_ref[:, lo + LANES:lo + HEAD_SLOT] = (r * scale).astype(BF16)


def q_project(cq, w_uq, cos_t, sin_t, scale, *, row0, rows):
    ql = cq.shape[1]
    n = w_uq.shape[1]
    tm = _pick(math.gcd(rows, row0), (512, 256, 128, 64))
    tn = _pick(n, (1024, 512, 256))
    r0 = row0 // tm
    return pl.pallas_call(
        functools.partial(_q_kernel, heads=tn // HEAD_SLOT, scale=scale),
        grid=(rows // tm, n // tn),
        in_specs=[pl.BlockSpec((tm, ql), lambda i, j: (r0 + i, 0)),
                  pl.BlockSpec((ql, tn), lambda i, j: (0, j)),
                  pl.BlockSpec((tm, LANES), lambda i, j: (r0 + i, 0)),
                  pl.BlockSpec((tm, LANES), lambda i, j: (r0 + i, 0))],
        out_specs=pl.BlockSpec((tm, tn), lambda i, j: (i, j)),
        out_shape=jax.ShapeDtypeStruct((rows, n), BF16),
        compiler_params=_cp("parallel", "parallel"),
        name="q_project",
    )(cq, w_uq, cos_t, sin_t)


def _qt_kernel(c_ref, wt_ref, cos_ref, sin_ref, o_ref, *, heads, scale):
    acc = lax.dot_general(wt_ref[...], c_ref[...], (((1,), (1,)), ((), ())),
                          preferred_element_type=F32)
    cos, sin = cos_ref[...], sin_ref[...]
    for h in range(heads):
        lo = h * HEAD_SLOT
        o_ref[0, lo:lo + LANES, :] = (acc[lo:lo + LANES] * scale).astype(BF16)
        r = acc[lo + LANES:lo + HEAD_SLOT]
        r = r * cos + pltpu.roll(r, 64, 0) * sin
        o_ref[0, lo + LANES:lo + HEAD_SLOT, :] = (r * scale).astype(BF16)


def q_project_t(cq, w_uq_t, cos_tt, sin_tt, scale, *, rows, tq):
    ql = cq.shape[1]
    n = w_uq_t.shape[0]
    tn = _pick(n, (1024, 512, 256))
    return pl.pallas_call(
        functools.partial(_qt_kernel, heads=tn // HEAD_SLOT, scale=scale),
        grid=(rows // tq, n // tn),
        in_specs=[pl.BlockSpec((tq, ql), lambda i, j: (i, 0)),
                  pl.BlockSpec((tn, ql), lambda i, j: (j, 0)),
                  pl.BlockSpec((LANES, tq), lambda i, j: (0, i)),
                  pl.BlockSpec((LANES, tq), lambda i, j: (0, i))],
        out_specs=pl.BlockSpec((1, tn, tq), lambda i, j: (i, j, 0)),
        out_shape=jax.ShapeDtypeStruct((rows // tq, n, tq), BF16),
        compiler_params=_cp("parallel", "parallel"),
        name="q_project_t",
    )(cq, w_uq_t, cos_tt, sin_tt)


def _kv_kernel(c_ref, kr_ref, wk_ref, wvt_ref, k_ref, vt_ref, *, heads):
    c = c_ref[...]
    kn = jnp.dot(c, wk_ref[...], preferred_element_type=F32)
    vt = lax.dot_general(wvt_ref[...], c, (((1,), (1,)), ((), ())),
                         preferred_element_type=F32).astype(BF16)
    ones = jnp.ones((VT_SLOT - LANES, vt.shape[1]), BF16)
    kr = kr_ref[...]
    for h in range(heads):
        lo = h * HEAD_SLOT
        k_ref[:, lo:lo + LANES] = kn[:, h * LANES:(h + 1) * LANES].astype(BF16)
        k_ref[:, lo + LANES:lo + HEAD_SLOT] = kr
        vt_ref[0, h * VT_SLOT:h * VT_SLOT + LANES, :] = vt[h * LANES:(h + 1) * LANES]
        vt_ref[0, h * VT_SLOT + LANES:(h + 1) * VT_SLOT, :] = ones


def kv_decompress(ckv, kr, w_uk, w_uv_t, *, rows, tm):
    kvl = ckv.shape[1]
    heads = w_uk.shape[1] // LANES
    row = lambda i: (i, 0)
    fix = lambda i: (0, 0)
    return pl.pallas_call(
        functools.partial(_kv_kernel, heads=heads),
        grid=(rows // tm,),
        in_specs=[pl.BlockSpec((tm, kvl), row), pl.BlockSpec((tm, LANES), row),
                  pl.BlockSpec(w_uk.shape, fix), pl.BlockSpec(w_uv_t.shape, fix)],
        out_specs=[pl.BlockSpec((tm, heads * HEAD_SLOT), row),
                   pl.BlockSpec((1, heads * VT_SLOT, tm), lambda i: (i, 0, 0))],
        out_shape=[jax.ShapeDtypeStruct((rows, heads * HEAD_SLOT), BF16),
                   jax.ShapeDtypeStruct((rows // tm, heads * VT_SLOT, tm), BF16)],
        compiler_params=_cp("parallel"),
        name="kv_decompress",
    )(ckv, kr, w_uk, w_uv_t)


def _qk(q, k):
    return lax.dot_general(q, k, (((1,), (1,)), ((), ())), preferred_element_type=F32)


def _mla_prompt_kernel(qt_ref, k_ref, vt_ref, o_ref, m_ref, acc_ref, s0_ref, s1_ref, *, t, nh):
    qi = pl.program_id(2)
    m_ref[...] = jnp.full_like(m_ref, NEG_INF)
    acc_ref[...] = jnp.zeros_like(acc_ref)

    def scores(ki, s_ref):
        start = pl.multiple_of(ki * t, t)
        for h in range(nh):
            s_ref[h] = jnp.dot(k_ref[pl.ds(start, t), h * HEAD_SLOT:(h + 1) * HEAD_SLOT],
                               qt_ref[0, h * HEAD_SLOT:(h + 1) * HEAD_SLOT, :],
                               preferred_element_type=F32)

    def consume(ki, s_ref, mask):
        for h in range(nh):
            s = s_ref[h]
            if mask is not None:
                s = jnp.where(mask, s, NEG_INF)
            m_old = m_ref[h]
            m_new = jnp.maximum(m_old, jnp.max(s, axis=0, keepdims=True))
            alpha = jnp.exp2(m_old - m_new)
            p = jnp.exp2(s - m_new)
            acc_ref[h] = alpha * acc_ref[h] + jnp.dot(
                vt_ref[ki, h * VT_SLOT:(h + 1) * VT_SLOT, :], p.astype(BF16),
                preferred_element_type=F32)
            m_ref[h] = m_new

    scores(0, s0_ref)

    def body(j, carry):
        scores(2 * j + 1, s1_ref)
        consume(2 * j, s0_ref, None)
        scores(2 * j + 2, s0_ref)
        consume(2 * j + 1, s1_ref, None)
        return carry

    lax.fori_loop(0, qi // 2, body, 0)
    kc = lax.broadcasted_iota(jnp.int32, (t, t), 0) // CHUNK
    qc = lax.broadcasted_iota(jnp.int32, (t, t), 1) // CHUNK
    mask = kc <= qc

    @pl.when(qi % 2 == 0)
    def _():
        consume(qi, s0_ref, mask)

    @pl.when(qi % 2 == 1)
    def _():
        scores(qi, s1_ref)
        consume(qi - 1, s0_ref, None)
        consume(qi, s1_ref, mask)

    for h in range(nh):
        o_ref[:, h * LANES:(h + 1) * LANES] = (
            acc_ref[h, :LANES, :] / acc_ref[h, LANES:LANES + 1, :]).T


def mla_prompt(qt, k, vt, rows, batch, seq, heads, t, nh):
    nq = seq // t
    return pl.pallas_call(
        functools.partial(_mla_prompt_kernel, t=t, nh=nh),
        grid=(batch, heads // nh, nq),
        in_specs=[pl.BlockSpec((1, nh * HEAD_SLOT, t), lambda b, g, i: (b * nq + i, g, 0)),
                  pl.BlockSpec((seq, nh * HEAD_SLOT), lambda b, g, i: (b, g),
                               pipeline_mode=pl.Buffered(1)),
                  pl.BlockSpec((nq, nh * VT_SLOT, t), lambda b, g, i: (b, g, 0),
                               pipeline_mode=pl.Buffered(1))],
        out_specs=pl.BlockSpec((t, nh * LANES), lambda b, g, i: (b * nq + i, g)),
        out_shape=jax.ShapeDtypeStruct((rows, heads * LANES), F32),
        scratch_shapes=[pltpu.VMEM((nh, 1, t), F32), pltpu.VMEM((nh, VT_SLOT, t), F32),
                        pltpu.VMEM((nh, t, t), F32), pltpu.VMEM((nh, t, t), F32)],
        compiler_params=_cp("parallel", "parallel", "arbitrary"),
        name="mla_prompt",
    )(qt, k, vt)


def _q_absorb_kernel(q_ref, wk_ref, o_ref):
    kvl = wk_ref.shape[0]
    o_ref[0, :, :kvl] = lax.dot_general(q_ref[:, :LANES], wk_ref[...], (((1,), (1,)), ((), ())),
                                        preferred_element_type=F32).astype(BF16)
    o_ref[0, :, kvl:] = q_ref[:, LANES:]


def _mla_latent_kernel(q_ref, cc_ref, kc_ref, perm_ref, cn_ref, kn_ref, o_ref):
    heads, dseq, width = q_ref.shape
    q = q_ref[...].reshape(heads * dseq, width)
    cc = cc_ref[...].astype(BF16)
    krc = jnp.dot(kc_ref[...].astype(BF16), perm_ref[...], preferred_element_type=F32).astype(BF16)
    cn = cn_ref[...]
    s1 = _qk(q, jnp.concatenate([cc, krc], axis=1))
    s2 = _qk(q, jnp.concatenate([cn, kn_ref[...]], axis=1))
    m = jnp.maximum(jnp.max(s1, axis=-1, keepdims=True), jnp.max(s2, axis=-1, keepdims=True))
    p1 = jnp.exp2(s1 - m)
    p2 = jnp.exp2(s2 - m)
    l = jnp.sum(p1, axis=-1, keepdims=True) + jnp.sum(p2, axis=-1, keepdims=True)
    o = (jnp.dot(p1.astype(BF16), cc, preferred_element_type=F32)
         + jnp.dot(p2.astype(BF16), cn, preferred_element_type=F32)) / l
    o_ref[...] = o.astype(BF16).reshape(heads, dseq, o.shape[1])


def _o_absorb_kernel(ol_ref, wv_ref, oin_ref, o_ref):
    del oin_ref
    o_ref[...] = jnp.dot(ol_ref[0], wv_ref[...], preferred_element_type=F32)


def mla_sample_latent(q_s, c_ckv, c_kr, ckv_b, kr_b, w_uk, w_uv, perm, o_buf, layer,
                      dbatch, dseq, past, heads, row0):
    ts, kvl = q_s.shape[0], w_uk.shape[0]
    width = kvl + LANES
    q_abs = pl.pallas_call(
        _q_absorb_kernel,
        grid=(heads,),
        in_specs=[pl.BlockSpec((ts, HEAD_SLOT), lambda h: (0, h)),
                  pl.BlockSpec((kvl, LANES), lambda h: (0, h))],
        out_specs=pl.BlockSpec((1, ts, width), lambda h: (h, 0, 0)),
        out_shape=jax.ShapeDtypeStruct((heads, ts, width), BF16),
        compiler_params=_cp("parallel"),
        name="mla_q_absorb",
    )(q_s, w_uk)
    r0 = row0 // dseq
    o_lat = pl.pallas_call(
        _mla_latent_kernel,
        grid=(dbatch,),
        in_specs=[pl.BlockSpec((heads, dseq, width), lambda b: (0, b, 0)),
                  pl.BlockSpec((None, past, kvl), lambda b: (layer, b, 0)),
                  pl.BlockSpec((None, past, c_kr.shape[2]), lambda b: (layer, b, 0)),
                  pl.BlockSpec(perm.shape, lambda b: (0, 0)),
                  pl.BlockSpec((dseq, kvl), lambda b: (r0 + b, 0)),
                  pl.BlockSpec((dseq, LANES), lambda b: (r0 + b, 0))],
        out_specs=pl.BlockSpec((heads, dseq, kvl), lambda b: (0, b, 0)),
        out_shape=jax.ShapeDtypeStruct((heads, ts, kvl), BF16),
        compiler_params=_cp("parallel"),
        name="mla_sample_latent",
    )(q_abs, c_ckv, c_kr, perm, ckv_b, kr_b)
    return pl.pallas_call(
        _o_absorb_kernel,
        grid=(heads,),
        in_specs=[pl.BlockSpec((1, ts, kvl), lambda h: (h, 0, 0)),
                  pl.BlockSpec((kvl, LANES), lambda h: (0, h)),
                  pl.BlockSpec(memory_space=pl.ANY)],
        out_specs=pl.BlockSpec((ts, LANES), lambda h: (row0 // ts, h)),
        out_shape=jax.ShapeDtypeStruct(o_buf.shape, F32),
        input_output_aliases={2: 0},
        compiler_params=_cp("parallel"),
        name="mla_o_absorb",
    )(o_lat, w_uv, o_buf)


def _band_prompt_kernel(q_ref, k0_ref, k1_ref, k2_ref, v0_ref, v1_ref, v2_ref, bias_ref, o_ref,
                        *, tq, nprev, scale, nh):
    qi = pl.program_id(2)
    ks = (k0_ref, k1_ref, k2_ref)[3 - nprev - 1:]
    vs = (v0_ref, v1_ref, v2_ref)[3 - nprev - 1:]
    raw = [[_qk(q_ref[:, h * LANES:(h + 1) * LANES], kr[:, h * LANES:(h + 1) * LANES]) for kr in ks]
           for h in range(nh)]
    for h in range(nh):
        hs = slice(h * LANES, (h + 1) * LANES)
        ss = []
        for d in range(len(ks)):
            s = raw[h][d] * scale + bias_ref[h, :, d * tq:(d + 1) * tq]
            if d < nprev:
                s = jnp.where(qi - (nprev - d) >= 0, s, NEG_INF)
            ss.append(s)
        m = functools.reduce(jnp.maximum, [jnp.max(s, axis=-1, keepdims=True) for s in ss])
        ps = [jnp.exp(s - m) for s in ss]
        l = functools.reduce(lambda a, b: a + b, [jnp.sum(p, axis=-1, keepdims=True) for p in ps])
        o = functools.reduce(lambda a, b: a + b,
                             [jnp.dot(p.astype(BF16), vr[:, hs], preferred_element_type=F32)
                              for p, vr in zip(ps, vs)])
        o_ref[:, hs] = o / l


def band_prompt(qkv, bias, batch, seq, heads, tq, nprev, scale):
    t = qkv.shape[0]
    nq = seq // tq
    nh = _pick(heads, (4, 2, 1))
    ng = heads // nh

    def kspec(back, col0):
        return pl.BlockSpec((tq, nh * LANES),
                            lambda b, g, i: (b * nq + jnp.maximum(i - back, 0), col0 + g))

    return pl.pallas_call(
        functools.partial(_band_prompt_kernel, tq=tq, nprev=nprev, scale=scale, nh=nh),
        grid=(batch, ng, nq),
        in_specs=[pl.BlockSpec((tq, nh * LANES), lambda b, g, i: (b * nq + i, g)),
                  kspec(2, ng), kspec(1, ng), kspec(0, ng),
                  kspec(2, 2 * ng), kspec(1, 2 * ng), kspec(0, 2 * ng),
                  pl.BlockSpec((nh, tq, (nprev + 1) * tq), lambda b, g, i: (g, 0, 0))],
        out_specs=pl.BlockSpec((tq, nh * LANES), lambda b, g, i: (b * nq + i, g)),
        out_shape=jax.ShapeDtypeStruct((t, heads * LANES), F32),
        compiler_params=_cp("parallel", "parallel", "arbitrary"),
        name="band_prompt",
    )(qkv, qkv, qkv, qkv, qkv, qkv, qkv, bias)


def _band_sample_kernel(q_ref, kc_ref, vc_ref, kn_ref, vn_ref, bias_ref, oin_ref, o_ref,
                        *, nb, scale, nh):
    del oin_ref
    for h in range(nh):
        hs = slice(h * LANES, (h + 1) * LANES)
        q = q_ref[:, hs]
        s1 = _qk(q, kc_ref[:, h, :].astype(BF16)) * scale + bias_ref[h, :, :nb]
        s2 = _qk(q, kn_ref[:, hs]) * scale + bias_ref[h, :, nb:]
        m = jnp.maximum(jnp.max(s1, axis=-1, keepdims=True), jnp.max(s2, axis=-1, keepdims=True))
        p1 = jnp.exp(s1 - m)
        p2 = jnp.exp(s2 - m)
        l = jnp.sum(p1, axis=-1, keepdims=True) + jnp.sum(p2, axis=-1, keepdims=True)
        o = (jnp.dot(p1.astype(BF16), vc_ref[:, h, :].astype(BF16), preferred_element_type=F32)
             + jnp.dot(p2.astype(BF16), vn_ref[:, hs], preferred_element_type=F32))
        o_ref[:, hs] = o / l


def band_sample(qkv, kc, vc, bias, o_buf, layer, dbatch, dseq, nb, heads, row0, scale):
    r0 = row0 // dseq
    nh, ng = heads, 1
    w = nh * LANES
    cspec = pl.BlockSpec((None, None, nb, heads, LANES), lambda b, g: (layer, b, 0, 0, 0))
    return pl.pallas_call(
        functools.partial(_band_sample_kernel, nb=nb, scale=scale, nh=nh),
        grid=(dbatch, ng),
        in_specs=[pl.BlockSpec((dseq, w), lambda b, g: (r0 + b, g)),
                  cspec, cspec,
                  pl.BlockSpec((dseq, w), lambda b, g: (r0 + b, ng + g)),
                  pl.BlockSpec((dseq, w), lambda b, g: (r0 + b, 2 * ng + g)),
                  pl.BlockSpec((nh, dseq, nb + dseq), lambda b, g: (g, 0, 0)),
                  pl.BlockSpec(memory_space=pl.ANY)],
        out_specs=pl.BlockSpec((dseq, w), lambda b, g: (r0 + b, g)),
        out_shape=jax.ShapeDtypeStruct(o_buf.shape, F32),
        input_output_aliases={6: 0},
        compiler_params=_cp("parallel", "parallel"),
        name="band_sample",
    )(qkv, kc, vc, qkv, qkv, bias, o_buf)


def _band_bias_table(rel_bias, nq, nk, q0):
    span = nq + nk - 1
    k = np.concatenate([np.arange(0, nk), np.arange(-(nq - 1), 0)])
    idx = np.clip(q0 - k, -MAX_REL, MAX_REL) + MAX_REL
    u = rel_bias.astype(F32)[:, idx]
    tab = jnp.tile(u, (1, nq))[:, :nq * (span - 1)].reshape(-1, nq, span - 1)[:, :, :nk]
    qc = (q0 + np.arange(nq))[:, None] // CHUNK
    kc = np.arange(nk)[None, :] // CHUNK
    mask = (kc <= qc) & (kc >= qc - BAND_PREV)
    return jnp.where(mask[None], tab, NEG_INF)


def _norm_mm_kernel(*refs, nparts, res_blocks):
    xs, gs = refs[:nparts], refs[nparts:2 * nparts]
    w_ref = refs[2 * nparts]
    r_refs = refs[2 * nparts + 1:-2]
    o_ref, hs_ref = refs[-2:]

    @pl.when(pl.program_id(1) == 0)
    def _():
        off = 0
        for x_ref, g_ref in zip(xs, gs):
            hp = _rmsnorm_rows(x_ref[...], g_ref[...]).astype(BF16)
            hs_ref[:, off:off + hp.shape[1]] = hp
            off += hp.shape[1]

    acc = jnp.dot(hs_ref[...], w_ref[...], preferred_element_type=F32)
    if r_refs:
        o_ref[...] = _part_rows(r_refs, res_blocks, pl.program_id(0)) + acc
    else:
        o_ref[...] = acc.astype(o_ref.dtype)


def norm_matmul(parts, gains, w, *, layer=0, res=(), out_dtype=F32, name):
    m = parts[0].shape[0]
    widths = [p.shape[1] for p in parts]
    k, n = sum(widths), w.shape[-1]
    tm, tn = _pick(math.gcd(m, *[r.shape[0] for r in res]), (512, 256, 128)), _pick(n, (1024, 512, 256, 128))
    in_specs = ([pl.BlockSpec((tm, dp), lambda i, j: (i, 0)) for dp in widths]
                + [pl.BlockSpec((1, dp), lambda i, j: (0, 0)) for dp in widths]
                + [_wspec(w, layer, (k, tn), lambda i, j: (0, j))]
                + _part_specs(res, (tm, tn), lambda j: j))
    args = list(parts) + [g.reshape(1, -1).astype(F32) for g in gains] + [w] + list(res)
    alias = {len(args) - 1: 0} if len(res) == 1 else {}
    return pl.pallas_call(
        functools.partial(_norm_mm_kernel, nparts=len(parts),
                          res_blocks=tuple(r.shape[0] // tm for r in res)),
        grid=(m // tm, n // tn),
        in_specs=in_specs,
        out_specs=pl.BlockSpec((tm, tn), lambda i, j: (i, j)),
        out_shape=jax.ShapeDtypeStruct((m, n), F32 if res else out_dtype),
        scratch_shapes=[pltpu.VMEM((tm, k), BF16)],
        input_output_aliases=alias,
        compiler_params=_cp("parallel", "arbitrary"),
        name=name,
    )(*args)


def _mem_kernel(q_ref, k_ref, v_ref, *rest, heads, dim, scale):
    o_ref = rest[-1]
    for h in range(heads):
        sl = slice(h * dim, (h + 1) * dim)
        k = k_ref[:, sl] if len(k_ref.shape) == 2 else k_ref[:, h, :]
        v = v_ref[:, sl] if len(v_ref.shape) == 2 else v_ref[:, h, :]
        s = _qk(q_ref[:, sl], k.astype(BF16)) * scale
        m = jnp.max(s, axis=-1, keepdims=True)
        p = jnp.exp(s - m)
        l = jnp.sum(p, axis=-1, keepdims=True)
        o = jnp.dot(p.astype(BF16), v.astype(BF16), preferred_element_type=F32)
        o_ref[:, sl] = (o / l).astype(BF16)


def mem_attend(q, mk, mv, kcol, vcol, o_buf, *, nbatch, rows_per_batch, row0, heads, dim, name):
    t, w = q.shape
    tq = _pick(rows_per_batch, (512, 256, 128, 64))
    nq = rows_per_batch // tq
    r0 = row0 // tq
    if mk.ndim == 2:
        mtok = mk.shape[0] // nbatch
        kspec = pl.BlockSpec((mtok, w), lambda b, i: (b, kcol))
        vspec = pl.BlockSpec((mtok, w), lambda b, i: (b, vcol))
    else:
        kspec = vspec = pl.BlockSpec((None, None) + mk.shape[2:], lambda b, i: (kcol, b, 0, 0, 0))
    in_specs = [pl.BlockSpec((tq, w), lambda b, i: (r0 + b * nq + i, 0)), kspec, vspec]
    args = [q, mk, mv]
    alias = {}
    if o_buf is not None:
        in_specs.append(pl.BlockSpec(memory_space=pl.ANY))
        args.append(o_buf)
        alias = {3: 0}
    return pl.pallas_call(
        functools.partial(_mem_kernel, heads=heads, dim=dim, scale=dim ** -0.5),
        grid=(nbatch, nq),
        in_specs=in_specs,
        out_specs=pl.BlockSpec((tq, w), lambda b, i: (r0 + b * nq + i, 0)),
        out_shape=jax.ShapeDtypeStruct((t, w), BF16),
        input_output_aliases=alias,
        compiler_params=_cp("parallel", "parallel"),
        name=name,
    )(*args)


def _ffn_up_kernel(x_ref, *refs, seg, nseg, nch, blocks_per_seq, use_state, has_buf):
    wa_refs, wg_refs = refs[:nch], refs[nch:2 * nch]
    wdw_ref, bdw_ref, st_ref = refs[2 * nch:2 * nch + 3]
    rest = refs[2 * nch + 3 + (1 if has_buf else 0):]
    if use_state:
        u_ref, tail_ref = rest
        carry_ref = None
    else:
        u_ref, tail_ref, carry_ref = rest
    i, j = pl.program_id(0), pl.program_id(1)
    tc = FFN_CHUNK
    row = lax.broadcasted_iota(jnp.int32, (8, tc), 0)
    if not use_state:
        @pl.when(i % blocks_per_seq == 0)
        def _():
            carry_ref[j] = jnp.zeros(carry_ref.shape[1:], F32)
    for c in range(nch):
        cs = slice(c * tc, (c + 1) * tc)
        a = jnp.dot(x_ref[...], wa_refs[c][...], preferred_element_type=F32)
        g = jnp.dot(x_ref[...], wg_refs[c][...], preferred_element_type=F32)
        w0, w1, w2 = wdw_ref[0:1, cs], wdw_ref[1:2, cs], wdw_ref[2:3, cs]
        for s in range(nseg):
            gs = g[s * seg:(s + 1) * seg]
            if use_state:
                p2, p1 = st_ref[s, 0:1, cs], st_ref[s, 1:2, cs]
            else:
                p2, p1 = carry_ref[j, 6:7, cs], carry_ref[j, 7:8, cs]
            prev8 = jnp.where(row == 6, p2, jnp.where(row == 7, p1, 0.0))
            g3 = gs.reshape(seg // 8, 8, tc)
            shifted = []
            for sh in (1, 2):
                cur = pltpu.roll(g3, sh, 1)
                before = jnp.concatenate([pltpu.roll(prev8, sh, 0)[None], cur[:-1]], axis=0)
                shifted.append(jnp.where(row[None] >= sh, cur, before).reshape(seg, tc))
            gm1, gm2 = shifted
            gc = ((bdw_ref[:, cs] + w0 * gm2) + w1 * gm1) + w2 * gs
            u_ref[s * seg:(s + 1) * seg, cs] = (
                a[s * seg:(s + 1) * seg] * (gc * jax.nn.sigmoid(gc))).astype(BF16)
            tail = gs[seg - 8:seg]
            tail_ref[s, :, cs] = tail
            if not use_state:
                carry_ref[j, :, cs] = tail


def ffn_up(h, w_up, w_dw, b_dw, state, u_buf, layer, *, row0, rows, seg, blocks_per_seq, use_state,
           name):
    t, d = h.shape
    ff = w_dw.shape[2]
    nblk = ff // FFN_CHUNK
    assert nblk * FFN_CHUNK == ff
    nch = FFN_NCHUNK if (u_buf is None and ff >= FFN_NCHUNK * FFN_CHUNK) else 1
    tn = nch * FFN_CHUNK
    if use_state:
        tm, nseg = rows, rows // seg
    else:
        tm, nseg = seg, 1
    r0 = row0 // tm
    ni, nj = rows // tm, pl.cdiv(ff, tn)
    scratch = [] if use_state else [pltpu.VMEM((nj, 8, tn), F32)]

    def wspec(half, c):
        return pl.BlockSpec(
            (None, d, FFN_CHUNK),
            lambda i, j: (layer, 0, jnp.minimum(half * nblk + j * nch + c, 2 * nblk - 1)))

    in_specs = ([pl.BlockSpec((tm, d), lambda i, j: (r0 + i, 0))]
                + [wspec(0, c) for c in range(nch)] + [wspec(1, c) for c in range(nch)]
                + [pl.BlockSpec((None, CONV_W, tn), lambda i, j: (layer, 0, j)),
                   pl.BlockSpec((None, 1, tn), lambda i, j: (layer, 0, j)),
                   pl.BlockSpec((None, state.shape[1], CONV_W - 1, tn),
                                lambda i, j: (layer, 0, 0, j))])
    args = [h] + [w_up] * (2 * nch) + [w_dw, b_dw.reshape(b_dw.shape[0], 1, ff), state]
    alias = {}
    if u_buf is not None:
        alias = {len(args): 0}
        in_specs.append(pl.BlockSpec(memory_space=pl.ANY))
        args.append(u_buf)
    return pl.pallas_call(
        functools.partial(_ffn_up_kernel, seg=seg, nseg=nseg, nch=nch,
                          blocks_per_seq=blocks_per_seq, use_state=use_state,
                          has_buf=u_buf is not None),
        grid=(ni, nj),
        in_specs=in_specs,
        out_specs=[pl.BlockSpec((tm, tn), lambda i, j: (r0 + i, j)),
                   pl.BlockSpec((nseg, 8, tn), lambda i, j: (i, 0, j))],
        out_shape=[jax.ShapeDtypeStruct((t, ff), BF16),
                   jax.ShapeDtypeStruct((ni * nseg, 8, ff), F32)],
        scratch_shapes=scratch,
        input_output_aliases=alias,
        compiler_params=_cp("arbitrary", "arbitrary"),
        name=name,
    )(*args)


def _rope_tables(pos):
    half = 32
    inv = ROPE_THETA ** (-jnp.arange(half, dtype=F32) / half)
    ang = pos.astype(F32)[:, None] * inv[None, :]
    c, s, z = jnp.cos(ang), jnp.sin(ang), jnp.zeros_like(ang)
    return jnp.concatenate([c, z, c, z], axis=1), jnp.concatenate([-s, z, s, z], axis=1)


def _slot_cols(w, half):
    z = jnp.zeros(w.shape[:-1] + (LANES // 2 - half,), w.dtype)
    return jnp.concatenate([w[..., :half], z, w[..., half:], z], axis=-1)


def kernel(x_prompt, x_sample, mem_prompt, cache_mla_ckv, cache_mla_krope, cache_band_k, cache_band_v, cache_mem_k, cache_mem_v, state_conv, norm_mix, w_in, norm_cq, norm_ckv, w_uq, w_uk, w_uv, rel_bias, g_out_a, g_out_b, w_o, norm_mem, norm_memtok, w_mq, w_mkv, w_mo, norm_ffn, w_up, w_dw, b_dw, w_down, norm_final):
    batch, seq, d = x_prompt.shape
    dbatch, dseq, _ = x_sample.shape
    depth = norm_mix.shape[0]
    past = cache_mla_ckv.shape[2]
    nband = cache_band_k.shape[2]
    ql, kvl = norm_cq.shape[1], norm_ckv.shape[1]
    rope = cache_mla_krope.shape[3]
    half = rope // 2
    a_heads, a_nope = w_uk.shape[2], w_uk.shape[3]
    a_vdim = w_uv.shape[3]
    b_heads, b_dim = cache_band_k.shape[3], cache_band_k.shape[4]
    mtok, m_heads, m_dim = cache_mem_k.shape[2], cache_mem_k.shape[3], cache_mem_k.shape[4]
    mem_w = m_heads * m_dim
    ff = b_dw.shape[1]
    assert a_nope == LANES and a_vdim == LANES and b_dim == LANES and rope == LANES // 2
    assert seq % CHUNK == 0 and dseq == CHUNK and past % CHUNK == 0

    tp, ts = batch * seq, dbatch * dseq
    t = tp + ts
    band_keep = min(BAND_PREV * CHUNK, seq)
    mla_scale = (a_nope + rope) ** -0.5
    b_scale = b_dim ** -0.5

    pos = jnp.concatenate([jnp.tile(jnp.arange(seq, dtype=jnp.int32), batch),
                           jnp.tile(past + jnp.arange(dseq, dtype=jnp.int32), dbatch)])
    cos_t, sin_t = _rope_tables(pos)
    band_tq = _pick(seq, (256, 128, 64))
    nprev = (BAND_PREV * CHUNK) // band_tq
    assert nprev * band_tq == BAND_PREV * CHUNK and nprev <= 2
    perm = _slot_cols(jnp.eye(rope, dtype=BF16), half)
    tk_down = _pick(ff, (5504, 2816, 2048, 1024, 512, 256))
    ffn_tm = _pick(math.gcd(seq, 1024), (1024, 512, 256, 128))
    mla_t = _pick(seq, (512, 256, 128))
    mla_nh = _pick(a_heads, (4, 2, 1))
    cos_tt, sin_tt = cos_t[:tp].T, sin_t[:tp].T

    w_up_b, w_down_b, w_o_b = w_up.astype(BF16), w_down.astype(BF16), w_o.astype(BF16)
    w_mq_b, w_mkv_b, w_mo_b = w_mq.astype(BF16), w_mkv.astype(BF16), w_mo.astype(BF16)
    c_ckv = cache_mla_ckv.reshape(depth, dbatch * past, kvl)
    c_kr = cache_mla_krope.reshape(depth, dbatch * past, rope)

    x = [x_prompt.reshape(tp, d), x_sample.reshape(ts, d)]
    outs = {k: [] for k in ("p_ckv", "p_kr", "p_bk", "p_bv", "p_mk", "p_mv", "p_conv",
                            "s_ckv", "s_kr", "s_bk", "s_bv", "s_conv")}

    def unslot(kr):
        return jnp.concatenate([kr[:, :half], kr[:, LANES // 2:LANES // 2 + half]], axis=1)

    for l in range(depth):
        wi = w_in[l]
        w_lat = jnp.concatenate([wi[:, :ql + kvl], _slot_cols(wi[:, ql + kvl:ql + kvl + rope], half)],
                                axis=1).astype(BF16)
        w_qkvb = wi[:, ql + kvl + rope:].astype(BF16)
        wq = w_uq[l].reshape(ql, a_heads, a_nope + rope)
        wq = jnp.concatenate([wq[..., :a_nope], _slot_cols(wq[..., a_nope:], half)], axis=-1)
        wq = wq.reshape(ql, a_heads * HEAD_SLOT).astype(BF16)
        wq_t = wq.T
        wuk = w_uk[l].reshape(kvl, a_heads * a_nope).astype(BF16)
        wuv = w_uv[l].reshape(kvl, a_heads * a_vdim).astype(BF16)
        wuv_t = wuv.T

        h = rmsnorm_parts(x, norm_mix[l], BF16, name="norm_mix")
        cq, ckv, ckv_b, kr, kr_b = latent_project(h, w_lat, norm_cq[l], norm_ckv[l], cos_t, sin_t)
        qkvb = matmul(h, w_qkvb, BF16, name="qkv_band")
        sel = jnp.concatenate([h[b * seq + seq - band_keep:(b + 1) * seq] for b in range(batch)]
                              + [h[tp:]], axis=0)
        kv_keep = matmul(sel, w_qkvb[:, b_heads * b_dim:], F32, name="kv_band_keep")
        q_t = q_project_t(cq, wq_t, cos_tt, sin_tt, mla_scale * LOG2E, rows=tp, tq=mla_t)
        q_s = q_project(cq, wq, cos_t, sin_t, mla_scale * LOG2E, row0=tp, rows=ts)
        k, v_t = kv_decompress(ckv_b, kr_b, wuk, wuv_t, rows=tp, tm=mla_t)
        oa = mla_prompt(q_t, k, v_t, t, batch, seq, a_heads, mla_t, mla_nh)
        oa = mla_sample_latent(q_s, c_ckv, c_kr, ckv_b, kr_b, wuk, wuv, perm, oa, l,
                               dbatch, dseq, past, a_heads, tp)
        bias_p = _band_bias_table(rel_bias[l], band_tq, (nprev + 1) * band_tq, nprev * band_tq)
        bias_s = _band_bias_table(rel_bias[l], dseq, nband + dseq, nband)
        ob = band_prompt(qkvb, bias_p, batch, seq, b_heads, band_tq, nprev, b_scale)
        ob = band_sample(qkvb, cache_band_k, cache_band_v, bias_s, ob, l, dbatch, dseq, nband,
                         b_heads, tp, b_scale)
        x = norm_matmul([oa, ob], [g_out_a[l], g_out_b[l]], w_o_b, layer=l, res=x, name="out_proj")

        memn = rmsnorm(mem_prompt.reshape(batch * mtok, d), norm_memtok[l], BF16, name="norm_memtok")
        mkv = matmul(memn, w_mkv_b, F32, layer=l, name="mem_kv")
        qm = norm_matmul([x], [norm_mem[l]], w_mq_b, layer=l, out_dtype=BF16, name="mem_q")
        om = mem_attend(qm, mkv, mkv, 0, 1, None, nbatch=batch, rows_per_batch=seq, row0=0,
                        heads=m_heads, dim=m_dim, name="mem_prompt")
        om = mem_attend(qm, cache_mem_k, cache_mem_v, l, l, om, nbatch=dbatch, rows_per_batch=dseq,
                        row0=tp, heads=m_heads, dim=m_dim, name="mem_sample")
        x = matmul_residual(om, w_mo_b, x, layer=l, name="mem_out")

        h = rmsnorm(x, norm_ffn[l], BF16, name="norm_ffn")
        u, tail_p = ffn_up(h, w_up_b, w_dw, b_dw, state_conv, None, l, row0=0, rows=tp, seg=ffn_tm,
                           blocks_per_seq=seq // ffn_tm, use_state=False, name="ffn_up_prompt")
        u, tail_s = ffn_up(h, w_up_b, w_dw, b_dw, state_conv, u, l, row0=tp, rows=ts, seg=dseq,
                           blocks_per_seq=1, use_state=True, name="ffn_up_sample")
        y = matmul_residual_ktiled(u, w_down_b, x, tk=tk_down, layer=l, name="ffn_down")
        x = [y]

        outs["p_ckv"].append(ckv[:tp].reshape(batch, seq, kvl))
        outs["s_ckv"].append(ckv[tp:].reshape(dbatch, dseq, kvl))
        kr64 = unslot(kr)
        outs["p_kr"].append(kr64[:tp].reshape(batch, seq, rope))
        outs["s_kr"].append(kr64[tp:].reshape(dbatch, dseq, rope))
        hw = b_heads * b_dim
        nkp = batch * band_keep
        outs["p_bk"].append(kv_keep[:nkp, :hw].reshape(batch, band_keep, b_heads, b_dim))
        outs["p_bv"].append(kv_keep[:nkp, hw:].reshape(batch, band_keep, b_heads, b_dim))
        outs["s_bk"].append(kv_keep[nkp:, :hw].reshape(dbatch, dseq, b_heads, b_dim))
        outs["s_bv"].append(kv_keep[nkp:, hw:].reshape(dbatch, dseq, b_heads, b_dim))
        outs["p_mk"].append(mkv[:, :mem_w].reshape(batch, mtok, m_heads, m_dim))
        outs["p_mv"].append(mkv[:, mem_w:].reshape(batch, mtok, m_heads, m_dim))
        nblk = seq // ffn_tm
        tail_p = tail_p.reshape(batch, nblk, 8, ff)[:, nblk - 1, 8 - (CONV_W - 1):]
        outs["p_conv"].append(tail_p)
        outs["s_conv"].append(tail_s[:, 8 - (CONV_W - 1):])

    y_prompt = rmsnorm(y, norm_final, F32, row0=0, rows=tp, name="norm_final_p").reshape(batch, seq, d)
    y_sample = rmsnorm(y, norm_final, F32, row0=tp, rows=ts, name="norm_final_s").reshape(dbatch, dseq, d)
    st = {k_: jnp.stack(v_) for k_, v_ in outs.items()}
    return (y_prompt, y_sample, st["p_ckv"], st["p_kr"], st["p_bk"], st["p_bv"], st["p_mk"],
            st["p_mv"], st["p_conv"], st["s_ckv"], st["s_kr"], st["s_bk"], st["s_bv"], st["s_conv"])
```

```python
import functools
import math

import jax
import jax.numpy as jnp
import numpy as np
from jax import lax
from jax.experimental import pallas as pl
from jax.experimental.pallas import tpu as pltpu

CHUNK = 64
BAND_PREV = 8
MAX_REL = 128
CONV_W = 3
ROPE_THETA = 10000.0
EPS = 1e-6
NEG_INF = -1e30
LOG2E = math.log2(math.e)

LANES = 128
HEAD_SLOT = 256
VT_SLOT = 144
VMEM_LIMIT = 56 * 1024 * 1024
FFN_CHUNK = 256
FFN_NCHUNK = 2

F32 = jnp.float32
BF16 = jnp.bfloat16


def _cp(*sem):
    return pltpu.CompilerParams(dimension_semantics=sem, vmem_limit_bytes=VMEM_LIMIT)


def _pick(n, prefs):
    for p in prefs:
        if n % p == 0:
            return p
    return n


def _rmsnorm_rows(x, g):
    ms = jnp.mean(x * x, axis=-1, keepdims=True)
    return (x * lax.rsqrt(ms + EPS)) * g


def _rmsnorm_kernel(x_ref, g_ref, o_ref):
    o_ref[...] = _rmsnorm_rows(x_ref[...].astype(F32), g_ref[...]).astype(o_ref.dtype)


def rmsnorm(x, g, out_dtype, *, row0=0, rows=None, name="rmsnorm"):
    m, d = x.shape
    rows = m if rows is None else rows
    tm = _pick(math.gcd(rows, row0) if row0 else rows, (256, 128, 64, 32, 16, 8))
    off = row0 // tm
    return pl.pallas_call(
        _rmsnorm_kernel,
        grid=(rows // tm,),
        in_specs=[pl.BlockSpec((tm, d), lambda i: (i + off, 0)),
                  pl.BlockSpec((1, d), lambda i: (0, 0))],
        out_specs=pl.BlockSpec((tm, d), lambda i: (i, 0)),
        out_shape=jax.ShapeDtypeStruct((rows, d), out_dtype),
        compiler_params=_cp("parallel"),
        name=name,
    )(x, g.reshape(1, d).astype(F32))


def _part_specs(parts, block, col):
    specs, off = [], 0
    for p in parts:
        nb = p.shape[0] // block[0]
        specs.append(pl.BlockSpec(
            block, lambda i, *r, off=off, nb=nb: (jnp.clip(i - off, 0, nb - 1), col(*r))))
        off += nb
    return specs


def _part_rows(refs, parts_blocks, i):
    x, bound = refs[0][...], 0
    for r, nb in zip(refs[1:], parts_blocks[:-1]):
        bound += nb
        x = jnp.where(i >= bound, r[...], x)
    return x


def _rmsnorm_parts_kernel(*refs, blocks):
    g_ref, o_ref = refs[-2:]
    x = _part_rows(refs[:-2], blocks, pl.program_id(0))
    o_ref[...] = _rmsnorm_rows(x, g_ref[...]).astype(o_ref.dtype)


def rmsnorm_parts(parts, g, out_dtype, *, name):
    d = parts[0].shape[1]
    tm = _pick(math.gcd(*[p.shape[0] for p in parts]), (256, 128, 64, 32, 16, 8))
    blocks = tuple(p.shape[0] // tm for p in parts)
    return pl.pallas_call(
        functools.partial(_rmsnorm_parts_kernel, blocks=blocks),
        grid=(sum(blocks),),
        in_specs=_part_specs(parts, (tm, d), lambda: 0) + [pl.BlockSpec((1, d), lambda i: (0, 0))],
        out_specs=pl.BlockSpec((tm, d), lambda i: (i, 0)),
        out_shape=jax.ShapeDtypeStruct((sum(blocks) * tm, d), out_dtype),
        compiler_params=_cp("parallel"),
        name=name,
    )(*parts, g.reshape(1, d).astype(F32))


def _mm_kernel(x_ref, w_ref, o_ref):
    o_ref[...] = jnp.dot(x_ref[...].astype(BF16), w_ref[...],
                         preferred_element_type=F32).astype(o_ref.dtype)


def _wspec(w, layer, block, index_map):
    if w.ndim == 2:
        return pl.BlockSpec(block, index_map)
    return pl.BlockSpec((None,) + block, lambda *a: (layer,) + index_map(*a))


def matmul(x, w, out_dtype, *, layer=0, tm_prefs=(1024, 512, 256, 128, 64, 32, 16, 8),
           tn_prefs=(1024, 512, 256, 128), name="matmul"):
    m, k = x.shape
    n = w.shape[-1]
    tm, tn = _pick(m, tm_prefs), _pick(n, tn_prefs)
    return pl.pallas_call(
        _mm_kernel,
        grid=(m // tm, n // tn),
        in_specs=[pl.BlockSpec((tm, k), lambda i, j: (i, 0)),
                  _wspec(w, layer, (k, tn), lambda i, j: (0, j))],
        out_specs=pl.BlockSpec((tm, tn), lambda i, j: (i, j)),
        out_shape=jax.ShapeDtypeStruct((m, n), out_dtype),
        compiler_params=_cp("parallel", "parallel"),
        name=name,
    )(x, w)


def _mm_res_kernel(x_ref, w_ref, r_ref, o_ref):
    o_ref[...] = r_ref[...] + jnp.dot(x_ref[...], w_ref[...], preferred_element_type=F32)


def matmul_residual(x, w, res, *, layer=0, name="matmul_res"):
    m, k = x.shape
    n = w.shape[-1]
    tm, tn = _pick(m, (1024, 512, 256, 128)), _pick(n, (1024, 512, 256, 128) if k <= 1024 else (512, 256, 128))
    return pl.pallas_call(
        _mm_res_kernel,
        grid=(m // tm, n // tn),
        in_specs=[pl.BlockSpec((tm, k), lambda i, j: (i, 0)),
                  _wspec(w, layer, (k, tn), lambda i, j: (0, j)),
                  pl.BlockSpec((tm, tn), lambda i, j: (i, j))],
        out_specs=pl.BlockSpec((tm, tn), lambda i, j: (i, j)),
        out_shape=jax.ShapeDtypeStruct((m, n), F32),
        input_output_aliases={2: 0},
        compiler_params=_cp("parallel", "parallel"),
        name=name,
    )(x, w, res)


def _mm_res_ktiled_kernel(x_ref, w_ref, r_ref, o_ref, acc_ref):
    kk = pl.program_id(2)

    @pl.when(kk == 0)
    def _():
        acc_ref[...] = jnp.zeros_like(acc_ref)

    acc_ref[...] += jnp.dot(x_ref[...], w_ref[...], preferred_element_type=F32)

    @pl.when(kk == pl.num_programs(2) - 1)
    def _():
        o_ref[...] = r_ref[...] + acc_ref[...]


def matmul_residual_ktiled(x, w, res, *, tk, layer=0, name="matmul_res_k"):
    m, k = x.shape
    n = w.shape[-1]
    tm, tn = _pick(m, (1024, 512, 256, 128)), _pick(n, (512, 256, 128))
    return pl.pallas_call(
        _mm_res_ktiled_kernel,
        grid=(m // tm, n // tn, k // tk),
        in_specs=[pl.BlockSpec((tm, tk), lambda i, j, kk: (i, kk)),
                  _wspec(w, layer, (tk, tn), lambda i, j, kk: (kk, j)),
                  pl.BlockSpec((tm, tn), lambda i, j, kk: (i, j))],
        out_specs=pl.BlockSpec((tm, tn), lambda i, j, kk: (i, j)),
        out_shape=jax.ShapeDtypeStruct((m, n), F32),
        scratch_shapes=[pltpu.VMEM((tm, tn), F32)],
        input_output_aliases={2: 0},
        compiler_params=_cp("parallel", "parallel", "arbitrary"),
        name=name,
    )(x, w, res)


def _rope_slot(r, cos, sin):
    return r * cos + pltpu.roll(r, 64, 1) * sin


def _lat_kernel(h_ref, w_ref, gq_ref, gkv_ref, cos_ref, sin_ref,
                cq_ref, ckv_ref, ckvb_ref, kr_ref, krb_ref, *, ql, kvl):
    acc = jnp.dot(h_ref[...], w_ref[...], preferred_element_type=F32)
    cq_ref[...] = _rmsnorm_rows(acc[:, :ql], gq_ref[...]).astype(BF16)
    ckv = _rmsnorm_rows(acc[:, ql:ql + kvl], gkv_ref[...])
    ckv_ref[...] = ckv
    ckvb_ref[...] = ckv.astype(BF16)
    kr = _rope_slot(acc[:, ql + kvl:], cos_ref[...], sin_ref[...])
    kr_ref[...] = kr
    krb_ref[...] = kr.astype(BF16)


def latent_project(h, w_lat, g_cq, g_ckv, cos_t, sin_t):
    m, d = h.shape
    ql, kvl = g_cq.shape[0], g_ckv.shape[0]
    n = w_lat.shape[1]
    tm = _pick(m, (512, 256, 128))
    row = lambda i: (i, 0)
    fix = lambda i: (0, 0)
    return pl.pallas_call(
        functools.partial(_lat_kernel, ql=ql, kvl=kvl),
        grid=(m // tm,),
        in_specs=[pl.BlockSpec((tm, d), row), pl.BlockSpec((d, n), fix),
                  pl.BlockSpec((1, ql), fix), pl.BlockSpec((1, kvl), fix),
                  pl.BlockSpec((tm, LANES), row), pl.BlockSpec((tm, LANES), row)],
        out_specs=[pl.BlockSpec((tm, ql), row), pl.BlockSpec((tm, kvl), row),
                   pl.BlockSpec((tm, kvl), row), pl.BlockSpec((tm, LANES), row),
                   pl.BlockSpec((tm, LANES), row)],
        out_shape=[jax.ShapeDtypeStruct((m, ql), BF16), jax.ShapeDtypeStruct((m, kvl), F32),
                   jax.ShapeDtypeStruct((m, kvl), BF16), jax.ShapeDtypeStruct((m, LANES), F32),
                   jax.ShapeDtypeStruct((m, LANES), BF16)],
        compiler_params=_cp("parallel"),
        name="latent_project",
    )(h, w_lat, g_cq.reshape(1, ql), g_ckv.reshape(1, kvl), cos_t, sin_t)


def _q_kernel(c_ref, w_ref, cos_ref, sin_ref, o_ref, *, heads, scale):
    acc = jnp.dot(c_ref[...], w_ref[...], preferred_element_type=F32)
    cos, sin = cos_ref[...], sin_ref[...]
    for h in range(heads):
        lo = h * HEAD_SLOT
        o_ref[:, lo:lo + LANES] = (acc[:, lo:lo + LANES] * scale).astype(BF16)
        r = _rope_slot(acc[:, lo + LANES:lo + HEAD_SLOT], cos, sin)
        o_ref[:, lo + LANES:lo + HEAD_SLOT] = (r * scale).astype(BF16)


def q_project(cq, w_uq, cos_t, sin_t, scale, *, row0, rows):
    ql = cq.shape[1]
    n = w_uq.shape[1]
    tm = _pick(math.gcd(rows, row0), (512, 256, 128, 64))
    tn = _pick(n, (1024, 512, 256))
    r0 = row0 // tm
    return pl.pallas_call(
        functools.partial(_q_kernel, heads=tn // HEAD_SLOT, scale=scale),
        grid=(rows // tm, n // tn),
        in_specs=[pl.BlockSpec((tm, ql), lambda i, j: (r0 + i, 0)),
                  pl.BlockSpec((ql, tn), lambda i, j: (0, j)),
                  pl.BlockSpec((tm, LANES), lambda i, j: (r0 + i, 0)),
                  pl.BlockSpec((tm, LANES), lambda i, j: (r0 + i, 0))],
        out_specs=pl.BlockSpec((tm, tn), lambda i, j: (i, j)),
        out_shape=jax.ShapeDtypeStruct((rows, n), BF16),
        compiler_params=_cp("parallel", "parallel"),
        name="q_project",
    )(cq, w_uq, cos_t, sin_t)


def _qt_kernel(c_ref, wt_ref, cos_ref, sin_ref, o_ref, *, heads, scale):
    acc = lax.dot_general(wt_ref[...], c_ref[...], (((1,), (1,)), ((), ())),
                          preferred_element_type=F32)
    cos, sin = cos_ref[...], sin_ref[...]
    for h in range(heads):
        lo = h * HEAD_SLOT
        o_ref[0, lo:lo + LANES, :] = (acc[lo:lo + LANES] * scale).astype(BF16)
        r = acc[lo + LANES:lo + HEAD_SLOT]
        r = r * cos + pltpu.roll(r, 64, 0) * sin
        o_ref[0, lo + LANES:lo + HEAD_SLOT, :] = (r * scale).astype(BF16)


def q_project_t(cq, w_uq_t, cos_tt, sin_tt, scale, *, rows, tq):
    ql = cq.shape[1]
    n = w_uq_t.shape[0]
    tn = _pick(n, (1024, 512, 256))
    return pl.pallas_call(
        functools.partial(_qt_kernel, heads=tn // HEAD_SLOT, scale=scale),
        grid=(rows // tq, n // tn),
        in_specs=[pl.BlockSpec((tq, ql), lambda i, j: (i, 0)),
                  pl.BlockSpec((tn, ql), lambda i, j: (j, 0)),
                  pl.BlockSpec((LANES, tq), lambda i, j: (0, i)),
                  pl.BlockSpec((LANES, tq), lambda i, j: (0, i))],
        out_specs=pl.BlockSpec((1, tn, tq), lambda i, j: (i, j, 0)),
        out_shape=jax.ShapeDtypeStruct((rows // tq, n, tq), BF16),
        compiler_params=_cp("parallel", "parallel"),
        name="q_project_t",
    )(cq, w_uq_t, cos_tt, sin_tt)


def _kv_kernel(c_ref, kr_ref, wk_ref, wvt_ref, k_ref, vt_ref, *, heads):
    c = c_ref[...]
    kn = jnp.dot(c, wk_ref[...], preferred_element_type=F32)
    vt = lax.dot_general(wvt_ref[...], c, (((1,), (1,)), ((), ())),
                         preferred_element_type=F32).astype(BF16)
    ones = jnp.ones((VT_SLOT - LANES, vt.shape[1]), BF16)
    kr = kr_ref[...]
    for h in range(heads):
        lo = h * HEAD_SLOT
        k_ref[:, lo:lo + LANES] = kn[:, h * LANES:(h + 1) * LANES].astype(BF16)
        k_ref[:, lo + LANES:lo + HEAD_SLOT] = kr
        vt_ref[0, h * VT_SLOT:h * VT_SLOT + LANES, :] = vt[h * LANES:(h + 1) * LANES]
        vt_ref[0, h * VT_SLOT + LANES:(h + 1) * VT_SLOT, :] = ones


def kv_decompress(ckv, kr, w_uk, w_uv_t, *, rows, tm):
    kvl = ckv.shape[1]
    heads = w_uk.shape[1] // LANES
    row = lambda i: (i, 0)
    fix = lambda i: (0, 0)
    return pl.pallas_call(
        functools.partial(_kv_kernel, heads=heads),
        grid=(rows // tm,),
        in_specs=[pl.BlockSpec((tm, kvl), row), pl.BlockSpec((tm, LANES), row),
                  pl.BlockSpec(w_uk.shape, fix), pl.BlockSpec(w_uv_t.shape, fix)],
        out_specs=[pl.BlockSpec((tm, heads * HEAD_SLOT), row),
                   pl.BlockSpec((1, heads * VT_SLOT, tm), lambda i: (i, 0, 0))],
        out_shape=[jax.ShapeDtypeStruct((rows, heads * HEAD_SLOT), BF16),
                   jax.ShapeDtypeStruct((rows // tm, heads * VT_SLOT, tm), BF16)],
        compiler_params=_cp("parallel"),
        name="kv_decompress",
    )(ckv, kr, w_uk, w_uv_t)


def _qk(q, k):
    return lax.dot_general(q, k, (((1,), (1,)), ((), ())), preferred_element_type=F32)


def _mla_prompt_kernel(qt_ref, k_ref, vt_ref, o_ref, m_ref, acc_ref, s0_ref, s1_ref, *, t, nh):
    qi = pl.program_id(2)
    m_ref[...] = jnp.full_like(m_ref, NEG_INF)
    acc_ref[...] = jnp.zeros_like(acc_ref)

    def scores(ki, s_ref):
        start = pl.multiple_of(ki * t, t)
        for h in range(nh):
            s_ref[h] = jnp.dot(k_ref[pl.ds(start, t), h * HEAD_SLOT:(h + 1) * HEAD_SLOT],
                               qt_ref[0, h * HEAD_SLOT:(h + 1) * HEAD_SLOT, :],
                               preferred_element_type=F32)

    def consume(ki, s_ref, mask):
        for h in range(nh):
            s = s_ref[h]
            if mask is not None:
                s = jnp.where(mask, s, NEG_INF)
            m_old = m_ref[h]
            m_new = jnp.maximum(m_old, jnp.max(s, axis=0, keepdims=True))
            alpha = jnp.exp2(m_old - m_new)
            p = jnp.exp2(s - m_new)
            acc_ref[h] = alpha * acc_ref[h] + jnp.dot(
                vt_ref[ki, h * VT_SLOT:(h + 1) * VT_SLOT, :], p.astype(BF16),
                preferred_element_type=F32)
            m_ref[h] = m_new

    scores(0, s0_ref)

    def body(j, carry):
        scores(2 * j + 1, s1_ref)
        consume(2 * j, s0_ref, None)
        scores(2 * j + 2, s0_ref)
        consume(2 * j + 1, s1_ref, None)
        return carry

    lax.fori_loop(0, qi // 2, body, 0)
    kc = lax.broadcasted_iota(jnp.int32, (t, t), 0) // CHUNK
    qc = lax.broadcasted_iota(jnp.int32, (t, t), 1) // CHUNK
    mask = kc <= qc

    @pl.when(qi % 2 == 0)
    def _():
        consume(qi, s0_ref, mask)

    @pl.when(qi % 2 == 1)
    def _():
        scores(qi, s1_ref)
        consume(qi - 1, s0_ref, None)
        consume(qi, s1_ref, mask)

    for h in range(nh):
        o_ref[:, h * LANES:(h + 1) * LANES] = (
            acc_ref[h, :LANES, :] / acc_ref[h, LANES:LANES + 1, :]).T


def mla_prompt(qt, k, vt, rows, batch, seq, heads, t, nh):
    nq = seq // t
    return pl.pallas_call(
        functools.partial(_mla_prompt_kernel, t=t, nh=nh),
        grid=(batch, heads // nh, nq),
        in_specs=[pl.BlockSpec((1, nh * HEAD_SLOT, t), lambda b, g, i: (b * nq + i, g, 0)),
                  pl.BlockSpec((seq, nh * HEAD_SLOT), lambda b, g, i: (b, g),
                               pipeline_mode=pl.Buffered(1)),
                  pl.BlockSpec((nq, nh * VT_SLOT, t), lambda b, g, i: (b, g, 0),
                               pipeline_mode=pl.Buffered(1))],
        out_specs=pl.BlockSpec((t, nh * LANES), lambda b, g, i: (b * nq + i, g)),
        out_shape=jax.ShapeDtypeStruct((rows, heads * LANES), F32),
        scratch_shapes=[pltpu.VMEM((nh, 1, t), F32), pltpu.VMEM((nh, VT_SLOT, t), F32),
                        pltpu.VMEM((nh, t, t), F32), pltpu.VMEM((nh, t, t), F32)],
        compiler_params=_cp("parallel", "parallel", "arbitrary"),
        name="mla_prompt",
    )(qt, k, vt)


def _q_absorb_kernel(q_ref, wk_ref, o_ref):
    kvl = wk_ref.shape[0]
    o_ref[0, :, :kvl] = lax.dot_general(q_ref[:, :LANES], wk_ref[...], (((1,), (1,)), ((), ())),
                                        preferred_element_type=F32).astype(BF16)
    o_ref[0, :, kvl:] = q_ref[:, LANES:]


def _mla_latent_kernel(q_ref, cc_ref, kc_ref, perm_ref, cn_ref, kn_ref, o_ref):
    heads, dseq, width = q_ref.shape
    q = q_ref[...].reshape(heads * dseq, width)
    cc = cc_ref[...].astype(BF16)
    krc = jnp.dot(kc_ref[...].astype(BF16), perm_ref[...], preferred_element_type=F32).astype(BF16)
    cn = cn_ref[...]
    s1 = _qk(q, jnp.concatenate([cc, krc], axis=1))
    s2 = _qk(q, jnp.concatenate([cn, kn_ref[...]], axis=1))
    m = jnp.maximum(jnp.max(s1, axis=-1, keepdims=True), jnp.max(s2, axis=-1, keepdims=True))
    p1 = jnp.exp2(s1 - m)
    p2 = jnp.exp2(s2 - m)
    l = jnp.sum(p1, axis=-1, keepdims=True) + jnp.sum(p2, axis=-1, keepdims=True)
    o = (jnp.dot(p1.astype(BF16), cc, preferred_element_type=F32)
         + jnp.dot(p2.astype(BF16), cn, preferred_element_type=F32)) / l
    o_ref[...] = o.astype(BF16).reshape(heads, dseq, o.shape[1])


def _o_absorb_kernel(ol_ref, wv_ref, oin_ref, o_ref):
    del oin_ref
    o_ref[...] = jnp.dot(ol_ref[0], wv_ref[...], preferred_element_type=F32)


def mla_sample_latent(q_s, c_ckv, c_kr, ckv_b, kr_b, w_uk, w_uv, perm, o_buf, layer,
                      dbatch, dseq, past, heads, row0):
    ts, kvl = q_s.shape[0], w_uk.shape[0]
    width = kvl + LANES
    q_abs = pl.pallas_call(
        _q_absorb_kernel,
        grid=(heads,),
        in_specs=[pl.BlockSpec((ts, HEAD_SLOT), lambda h: (0, h)),
                  pl.BlockSpec((kvl, LANES), lambda h: (0, h))],
        out_specs=pl.BlockSpec((1, ts, width), lambda h: (h, 0, 0)),
        out_shape=jax.ShapeDtypeStruct((heads, ts, width), BF16),
        compiler_params=_cp("parallel"),
        name="mla_q_absorb",
    )(q_s, w_uk)
    r0 = row0 // dseq
    o_lat = pl.pallas_call(
        _mla_latent_kernel,
        grid=(dbatch,),
        in_specs=[pl.BlockSpec((heads, dseq, width), lambda b: (0, b, 0)),
                  pl.BlockSpec((None, past, kvl), lambda b: (layer, b, 0)),
                  pl.BlockSpec((None, past, c_kr.shape[2]), lambda b: (layer, b, 0)),
                  pl.BlockSpec(perm.shape, lambda b: (0, 0)),
                  pl.BlockSpec((dseq, kvl), lambda b: (r0 + b, 0)),
                  pl.BlockSpec((dseq, LANES), lambda b: (r0 + b, 0))],
        out_specs=pl.BlockSpec((heads, dseq, kvl), lambda b: (0, b, 0)),
        out_shape=jax.ShapeDtypeStruct((heads, ts, kvl), BF16),
        compiler_params=_cp("parallel"),
        name="mla_sample_latent",
    )(q_abs, c_ckv, c_kr, perm, ckv_b, kr_b)
    return pl.pallas_call(
        _o_absorb_kernel,
        grid=(heads,),
        in_specs=[pl.BlockSpec((1, ts, kvl), lambda h: (h, 0, 0)),
                  pl.BlockSpec((kvl, LANES), lambda h: (0, h)),
                  pl.BlockSpec(memory_space=pl.ANY)],
        out_specs=pl.BlockSpec((ts, LANES), lambda h: (row0 // ts, h)),
        out_shape=jax.ShapeDtypeStruct(o_buf.shape, F32),
        input_output_aliases={2: 0},
        compiler_params=_cp("parallel"),
        name="mla_o_absorb",
    )(o_lat, w_uv, o_buf)


def _band_prompt_kernel(q_ref, k0_ref, k1_ref, k2_ref, v0_ref, v1_ref, v2_ref, bias_ref, o_ref,
                        *, tq, nprev, scale, nh):
    qi = pl.program_id(2)
    ks = (k0_ref, k1_ref, k2_ref)[3 - nprev - 1:]
    vs = (v0_ref, v1_ref, v2_ref)[3 - nprev - 1:]
    raw = [[_qk(q_ref[:, h * LANES:(h + 1) * LANES], kr[:, h * LANES:(h + 1) * LANES]) for kr in ks]
           for h in range(nh)]
    for h in range(nh):
        hs = slice(h * LANES, (h + 1) * LANES)
        ss = []
        for d in range(len(ks)):
            s = raw[h][d] * scale + bias_ref[h, :, d * tq:(d + 1) * tq]
            if d < nprev:
                s = jnp.where(qi - (nprev - d) >= 0, s, NEG_INF)
            ss.append(s)
        m = functools.reduce(jnp.maximum, [jnp.max(s, axis=-1, keepdims=True) for s in ss])
        ps = [jnp.exp(s - m) for s in ss]
        l = functools.reduce(lambda a, b: a + b, [jnp.sum(p, axis=-1, keepdims=True) for p in ps])
        o = functools.reduce(lambda a, b: a + b,
                             [jnp.dot(p.astype(BF16), vr[:, hs], preferred_element_type=F32)
                              for p, vr in zip(ps, vs)])
        o_ref[:, hs] = o / l


def band_prompt(qkv, bias, batch, seq, heads, tq, nprev, scale):
    t = qkv.shape[0]
    nq = seq // tq
    nh = _pick(heads, (4, 2, 1))
    ng = heads // nh

    def kspec(back, col0):
        return pl.BlockSpec((tq, nh * LANES),
                            lambda b, g, i: (b * nq + jnp.maximum(i - back, 0), col0 + g))

    return pl.pallas_call(
        functools.partial(_band_prompt_kernel, tq=tq, nprev=nprev, scale=scale, nh=nh),
        grid=(batch, ng, nq),
        in_specs=[pl.BlockSpec((tq, nh * LANES), lambda b, g, i: (b * nq + i, g)),
                  kspec(2, ng), kspec(1, ng), kspec(0, ng),
                  kspec(2, 2 * ng), kspec(1, 2 * ng), kspec(0, 2 * ng),
                  pl.BlockSpec((nh, tq, (nprev + 1) * tq), lambda b, g, i: (g, 0, 0))],
        out_specs=pl.BlockSpec((tq, nh * LANES), lambda b, g, i: (b * nq + i, g)),
        out_shape=jax.ShapeDtypeStruct((t, heads * LANES), F32),
        compiler_params=_cp("parallel", "parallel", "arbitrary"),
        name="band_prompt",
    )(qkv, qkv, qkv, qkv, qkv, qkv, qkv, bias)


def _band_sample_kernel(q_ref, kc_ref, vc_ref, kn_ref, vn_ref, bias_ref, oin_ref, o_ref,
                        *, nb, scale, nh):
    del oin_ref
    for h in range(nh):
        hs = slice(h * LANES, (h + 1) * LANES)
        q = q_ref[:, hs]
        s1 = _qk(q, kc_ref[:, h, :].astype(BF16)) * scale + bias_ref[h, :, :nb]
        s2 = _qk(q, kn_ref[:, hs]) * scale + bias_ref[h, :, nb:]
        m = jnp.maximum(jnp.max(s1, axis=-1, keepdims=True), jnp.max(s2, axis=-1, keepdims=True))
        p1 = jnp.exp(s1 - m)
        p2 = jnp.exp(s2 - m)
        l = jnp.sum(p1, axis=-1, keepdims=True) + jnp.sum(p2, axis=-1, keepdims=True)
        o = (jnp.dot(p1.astype(BF16), vc_ref[:, h, :].astype(BF16), preferred_element_type=F32)
             + jnp.dot(p2.astype(BF16), vn_ref[:, hs], preferred_element_type=F32))
        o_ref[:, hs] = o / l


def band_sample(qkv, kc, vc, bias, o_buf, layer, dbatch, dseq, nb, heads, row0, scale):
    r0 = row0 // dseq
    nh, ng = heads, 1
    w = nh * LANES
    cspec = pl.BlockSpec((None, None, nb, heads, LANES), lambda b, g: (layer, b, 0, 0, 0))
    return pl.pallas_call(
        functools.partial(_band_sample_kernel, nb=nb, scale=scale, nh=nh),
        grid=(dbatch, ng),
        in_specs=[pl.BlockSpec((dseq, w), lambda b, g: (r0 + b, g)),
                  cspec, cspec,
                  pl.BlockSpec((dseq, w), lambda b, g: (r0 + b, ng + g)),
                  pl.BlockSpec((dseq, w), lambda b, g: (r0 + b, 2 * ng + g)),
                  pl.BlockSpec((nh, dseq, nb + dseq), lambda b, g: (g, 0, 0)),
                  pl.BlockSpec(memory_space=pl.ANY)],
        out_specs=pl.BlockSpec((dseq, w), lambda b, g: (r0 + b, g)),
        out_shape=jax.ShapeDtypeStruct(o_buf.shape, F32),
        input_output_aliases={6: 0},
        compiler_params=_cp("parallel", "parallel"),
        name="band_sample",
    )(qkv, kc, vc, qkv, qkv, bias, o_buf)


def _band_bias_table(rel_bias, nq, nk, q0):
    span = nq + nk - 1
    k = np.concatenate([np.arange(0, nk), np.arange(-(nq - 1), 0)])
    idx = np.clip(q0 - k, -MAX_REL, MAX_REL) + MAX_REL
    u = rel_bias.astype(F32)[:, idx]
    tab = jnp.tile(u, (1, nq))[:, :nq * (span - 1)].reshape(-1, nq, span - 1)[:, :, :nk]
    qc = (q0 + np.arange(nq))[:, None] // CHUNK
    kc = np.arange(nk)[None, :] // CHUNK
    mask = (kc <= qc) & (kc >= qc - BAND_PREV)
    return jnp.where(mask[None], tab, NEG_INF)


def _norm_mm_kernel(*refs, nparts, res_blocks):
    xs, gs = refs[:nparts], refs[nparts:2 * nparts]
    w_ref = refs[2 * nparts]
    r_refs = refs[2 * nparts + 1:-2]
    o_ref, hs_ref = refs[-2:]

    @pl.when(pl.program_id(1) == 0)
    def _():
        off = 0
        for x_ref, g_ref in zip(xs, gs):
            hp = _rmsnorm_rows(x_ref[...], g_ref[...]).astype(BF16)
            hs_ref[:, off:off + hp.shape[1]] = hp
            off += hp.shape[1]

    acc = jnp.dot(hs_ref[...], w_ref[...], preferred_element_type=F32)
    if r_refs:
        o_ref[...] = _part_rows(r_refs, res_blocks, pl.program_id(0)) + acc
    else:
        o_ref[...] = acc.astype(o_ref.dtype)


def norm_matmul(parts, gains, w, *, layer=0, res=(), out_dtype=F32, name):
    m = parts[0].shape[0]
    widths = [p.shape[1] for p in parts]
    k, n = sum(widths), w.shape[-1]
    tm, tn = _pick(math.gcd(m, *[r.shape[0] for r in res]), (512, 256, 128)), _pick(n, (1024, 512, 256, 128))
    in_specs = ([pl.BlockSpec((tm, dp), lambda i, j: (i, 0)) for dp in widths]
                + [pl.BlockSpec((1, dp), lambda i, j: (0, 0)) for dp in widths]
                + [_wspec(w, layer, (k, tn), lambda i, j: (0, j))]
                + _part_specs(res, (tm, tn), lambda j: j))
    args = list(parts) + [g.reshape(1, -1).astype(F32) for g in gains] + [w] + list(res)
    alias = {len(args) - 1: 0} if len(res) == 1 else {}
    return pl.pallas_call(
        functools.partial(_norm_mm_kernel, nparts=len(parts),
                          res_blocks=tuple(r.shape[0] // tm for r in res)),
        grid=(m // tm, n // tn),
        in_specs=in_specs,
        out_specs=pl.BlockSpec((tm, tn), lambda i, j: (i, j)),
        out_shape=jax.ShapeDtypeStruct((m, n), F32 if res else out_dtype),
        scratch_shapes=[pltpu.VMEM((tm, k), BF16)],
        input_output_aliases=alias,
        compiler_params=_cp("parallel", "arbitrary"),
        name=name,
    )(*args)


def _mem_kernel(q_ref, k_ref, v_ref, *rest, heads, dim, scale):
    o_ref = rest[-1]
    for h in range(heads):
        sl = slice(h * dim, (h + 1) * dim)
        k = k_ref[:, sl] if len(k_ref.shape) == 2 else k_ref[:, h, :]
        v = v_ref[:, sl] if len(v_ref.shape) == 2 else v_ref[:, h, :]
        s = _qk(q_ref[:, sl], k.astype(BF16)) * scale
        m = jnp.max(s, axis=-1, keepdims=True)
        p = jnp.exp(s - m)
        l = jnp.sum(p, axis=-1, keepdims=True)
        o = jnp.dot(p.astype(BF16), v.astype(BF16), preferred_element_type=F32)
        o_ref[:, sl] = (o / l).astype(BF16)


def mem_attend(q, mk, mv, kcol, vcol, o_buf, *, nbatch, rows_per_batch, row0, heads, dim, name):
    t, w = q.shape
    tq = _pick(rows_per_batch, (512, 256, 128, 64))
    nq = rows_per_batch // tq
    r0 = row0 // tq
    if mk.ndim == 2:
        mtok = mk.shape[0] // nbatch
        kspec = pl.BlockSpec((mtok, w), lambda b, i: (b, kcol))
        vspec = pl.BlockSpec((mtok, w), lambda b, i: (b, vcol))
    else:
        kspec = vspec = pl.BlockSpec((None, None) + mk.shape[2:], lambda b, i: (kcol, b, 0, 0, 0))
    in_specs = [pl.BlockSpec((tq, w), lambda b, i: (r0 + b * nq + i, 0)), kspec, vspec]
    args = [q, mk, mv]
    alias = {}
    if o_buf is not None:
        in_specs.append(pl.BlockSpec(memory_space=pl.ANY))
        args.append(o_buf)
        alias = {3: 0}
    return pl.pallas_call(
        functools.partial(_mem_kernel, heads=heads, dim=dim, scale=dim ** -0.5),
        grid=(nbatch, nq),
        in_specs=in_specs,
        out_specs=pl.BlockSpec((tq, w), lambda b, i: (r0 + b * nq + i, 0)),
        out_shape=jax.ShapeDtypeStruct((t, w), BF16),
        input_output_aliases=alias,
        compiler_params=_cp("parallel", "parallel"),
        name=name,
    )(*args)


def _ffn_up_kernel(x_ref, *refs, seg, nseg, nch, blocks_per_seq, use_state, has_buf):
    wa_refs, wg_refs = refs[:nch], refs[nch:2 * nch]
    wdw_ref, bdw_ref, st_ref = refs[2 * nch:2 * nch + 3]
    rest = refs[2 * nch + 3 + (1 if has_buf else 0):]
    if use_state:
        u_ref, tail_ref = rest
        carry_ref = None
    else:
        u_ref, tail_ref, carry_ref = rest
    i, j = pl.program_id(0), pl.program_id(1)
    tc = FFN_CHUNK
    row = lax.broadcasted_iota(jnp.int32, (8, tc), 0)
    if not use_state:
        @pl.when(i % blocks_per_seq == 0)
        def _():
            carry_ref[j] = jnp.zeros(carry_ref.shape[1:], F32)
    for c in range(nch):
        cs = slice(c * tc, (c + 1) * tc)
        a = jnp.dot(x_ref[...], wa_refs[c][...], preferred_element_type=F32)
        g = jnp.dot(x_ref[...], wg_refs[c][...], preferred_element_type=F32)
        w0, w1, w2 = wdw_ref[0:1, cs], wdw_ref[1:2, cs], wdw_ref[2:3, cs]
        for s in range(nseg):
            gs = g[s * seg:(s + 1) * seg]
            if use_state:
                p2, p1 = st_ref[s, 0:1, cs], st_ref[s, 1:2, cs]
            else:
                p2, p1 = carry_ref[j, 6:7, cs], carry_ref[j, 7:8, cs]
            prev8 = jnp.where(row == 6, p2, jnp.where(row == 7, p1, 0.0))
            g3 = gs.reshape(seg // 8, 8, tc)
            shifted = []
            for sh in (1, 2):
                cur = pltpu.roll(g3, sh, 1)
                before = jnp.concatenate([pltpu.roll(prev8, sh, 0)[None], cur[:-1]], axis=0)
                shifted.append(jnp.where(row[None] >= sh, cur, before).reshape(seg, tc))
            gm1, gm2 = shifted
            gc = ((bdw_ref[:, cs] + w0 * gm2) + w1 * gm1) + w2 * gs
            u_ref[s * seg:(s + 1) * seg, cs] = (
                a[s * seg:(s + 1) * seg] * (gc * jax.nn.sigmoid(gc))).astype(BF16)
            tail = gs[seg - 8:seg]
            tail_ref[s, :, cs] = tail
            if not use_state:
                carry_ref[j, :, cs] = tail


def ffn_up(h, w_up, w_dw, b_dw, state, u_buf, layer, *, row0, rows, seg, blocks_per_seq, use_state,
           name):
    t, d = h.shape
    ff = w_dw.shape[2]
    nblk = ff // FFN_CHUNK
    assert nblk * FFN_CHUNK == ff
    nch = FFN_NCHUNK if (u_buf is None and ff >= FFN_NCHUNK * FFN_CHUNK) else 1
    tn = nch * FFN_CHUNK
    if use_state:
        tm, nseg = rows, rows // seg
    else:
        tm, nseg = seg, 1
    r0 = row0 // tm
    ni, nj = rows // tm, pl.cdiv(ff, tn)
    scratch = [] if use_state else [pltpu.VMEM((nj, 8, tn), F32)]

    def wspec(half, c):
        return pl.BlockSpec(
            (None, d, FFN_CHUNK),
            lambda i, j: (layer, 0, jnp.minimum(half * nblk + j * nch + c, 2 * nblk - 1)))

    in_specs = ([pl.BlockSpec((tm, d), lambda i, j: (r0 + i, 0))]
                + [wspec(0, c) for c in range(nch)] + [wspec(1, c) for c in range(nch)]
                + [pl.BlockSpec((None, CONV_W, tn), lambda i, j: (layer, 0, j)),
                   pl.BlockSpec((None, 1, tn), lambda i, j: (layer, 0, j)),
                   pl.BlockSpec((None, state.shape[1], CONV_W - 1, tn),
                                lambda i, j: (layer, 0, 0, j))])
    args = [h] + [w_up] * (2 * nch) + [w_dw, b_dw.reshape(b_dw.shape[0], 1, ff), state]
    alias = {}
    if u_buf is not None:
        alias = {len(args): 0}
        in_specs.append(pl.BlockSpec(memory_space=pl.ANY))
        args.append(u_buf)
    return pl.pallas_call(
        functools.partial(_ffn_up_kernel, seg=seg, nseg=nseg, nch=nch,
                          blocks_per_seq=blocks_per_seq, use_state=use_state,
                          has_buf=u_buf is not None),
        grid=(ni, nj),
        in_specs=in_specs,
        out_specs=[pl.BlockSpec((tm, tn), lambda i, j: (r0 + i, j)),
                   pl.BlockSpec((nseg, 8, tn), lambda i, j: (i, 0, j))],
        out_shape=[jax.ShapeDtypeStruct((t, ff), BF16),
                   jax.ShapeDtypeStruct((ni * nseg, 8, ff), F32)],
        scratch_shapes=scratch,
        input_output_aliases=alias,
        compiler_params=_cp("arbitrary", "arbitrary"),
        name=name,
    )(*args)


def _rope_tables(pos):
    half = 32
    inv = ROPE_THETA ** (-jnp.arange(half, dtype=F32) / half)
    ang = pos.astype(F32)[:, None] * inv[None, :]
    c, s, z = jnp.cos(ang), jnp.sin(ang), jnp.zeros_like(ang)
    return jnp.concatenate([c, z, c, z], axis=1), jnp.concatenate([-s, z, s, z], axis=1)


def _slot_cols(w, half):
    z = jnp.zeros(w.shape[:-1] + (LANES // 2 - half,), w.dtype)
    return jnp.concatenate([w[..., :half], z, w[..., half:], z], axis=-1)


def kernel(x_prompt, x_sample, mem_prompt, cache_mla_ckv, cache_mla_krope, cache_band_k, cache_band_v, cache_mem_k, cache_mem_v, state_conv, norm_mix, w_in, norm_cq, norm_ckv, w_uq, w_uk, w_uv, rel_bias, g_out_a, g_out_b, w_o, norm_mem, norm_memtok, w_mq, w_mkv, w_mo, norm_ffn, w_up, w_dw, b_dw, w_down, norm_final):
    batch, seq, d = x_prompt.shape
    dbatch, dseq, _ = x_sample.shape
    depth = norm_mix.shape[0]
    past = cache_mla_ckv.shape[2]
    nband = cache_band_k.shape[2]
    ql, kvl = norm_cq.shape[1], norm_ckv.shape[1]
    rope = cache_mla_krope.shape[3]
    half = rope // 2
    a_heads, a_nope = w_uk.shape[2], w_uk.shape[3]
    a_vdim = w_uv.shape[3]
    b_heads, b_dim = cache_band_k.shape[3], cache_band_k.shape[4]
    mtok, m_heads, m_dim = cache_mem_k.shape[2], cache_mem_k.shape[3], cache_mem_k.shape[4]
    mem_w = m_heads * m_dim
    ff = b_dw.shape[1]
    assert a_nope == LANES and a_vdim == LANES and b_dim == LANES and rope == LANES // 2
    assert seq % CHUNK == 0 and dseq == CHUNK and past % CHUNK == 0

    tp, ts = batch * seq, dbatch * dseq
    t = tp + ts
    band_keep = min(BAND_PREV * CHUNK, seq)
    mla_scale = (a_nope + rope) ** -0.5
    b_scale = b_dim ** -0.5

    pos = jnp.concatenate([jnp.tile(jnp.arange(seq, dtype=jnp.int32), batch),
                           jnp.tile(past + jnp.arange(dseq, dtype=jnp.int32), dbatch)])
    cos_t, sin_t = _rope_tables(pos)
    band_tq = _pick(seq, (256, 128, 64))
    nprev = (BAND_PREV * CHUNK) // band_tq
    assert nprev * band_tq == BAND_PREV * CHUNK and nprev <= 2
    perm = _slot_cols(jnp.eye(rope, dtype=BF16), half)
    tk_down = _pick(ff, (5504, 2816, 2048, 1024, 512, 256))
    ffn_tm = _pick(math.gcd(seq, 1024), (1024, 512, 256, 128))
    mla_t = _pick(seq, (512, 256, 128))
    mla_nh = _pick(a_heads, (4, 2, 1))
    cos_tt, sin_tt = cos_t[:tp].T, sin_t[:tp].T

    w_up_b, w_down_b, w_o_b = w_up.astype(BF16), w_down.astype(BF16), w_o.astype(BF16)
    w_mq_b, w_mkv_b, w_mo_b = w_mq.astype(BF16), w_mkv.astype(BF16), w_mo.astype(BF16)
    c_ckv = cache_mla_ckv.reshape(depth, dbatch * past, kvl)
    c_kr = cache_mla_krope.reshape(depth, dbatch * past, rope)

    x = [x_prompt.reshape(tp, d), x_sample.reshape(ts, d)]
    outs = {k: [] for k in ("p_ckv", "p_kr", "p_bk", "p_bv", "p_mk", "p_mv", "p_conv",
                            "s_ckv", "s_kr", "s_bk", "s_bv", "s_conv")}

    def unslot(kr):
        return jnp.concatenate([kr[:, :half], kr[:, LANES // 2:LANES // 2 + half]], axis=1)

    for l in range(depth):
        wi = w_in[l]
        w_lat = jnp.concatenate([wi[:, :ql + kvl], _slot_cols(wi[:, ql + kvl:ql + kvl + rope], half)],
                                axis=1).astype(BF16)
        w_qkvb = wi[:, ql + kvl + rope:].astype(BF16)
        wq = w_uq[l].reshape(ql, a_heads, a_nope + rope)
        wq = jnp.concatenate([wq[..., :a_nope], _slot_cols(wq[..., a_nope:], half)], axis=-1)
        wq = wq.reshape(ql, a_heads * HEAD_SLOT).astype(BF16)
        wq_t = wq.T
        wuk = w_uk[l].reshape(kvl, a_heads * a_nope).astype(BF16)
        wuv = w_uv[l].reshape(kvl, a_heads * a_vdim).astype(BF16)
        wuv_t = wuv.T

        h = rmsnorm_parts(x, norm_mix[l], BF16, name="norm_mix")
        cq, ckv, ckv_b, kr, kr_b = latent_project(h, w_lat, norm_cq[l], norm_ckv[l], cos_t, sin_t)
        qkvb = matmul(h, w_qkvb, BF16, name="qkv_band")
        sel = jnp.concatenate([h[b * seq + seq - band_keep:(b + 1) * seq] for b in range(batch)]
                              + [h[tp:]], axis=0)
        kv_keep = matmul(sel, w_qkvb[:, b_heads * b_dim:], F32, name="kv_band_keep")
        q_t = q_project_t(cq, wq_t, cos_tt, sin_tt, mla_scale * LOG2E, rows=tp, tq=mla_t)
        q_s = q_project(cq, wq, cos_t, sin_t, mla_scale * LOG2E, row0=tp, rows=ts)
        k, v_t = kv_decompress(ckv_b, kr_b, wuk, wuv_t, rows=tp, tm=mla_t)
        oa = mla_prompt(q_t, k, v_t, t, batch, seq, a_heads, mla_t, mla_nh)
        oa = mla_sample_latent(q_s, c_ckv, c_kr, ckv_b, kr_b, wuk, wuv, perm, oa, l,
                               dbatch, dseq, past, a_heads, tp)
        bias_p = _band_bias_table(rel_bias[l], band_tq, (nprev + 1) * band_tq, nprev * band_tq)
        bias_s = _band_bias_table(rel_bias[l], dseq, nband + dseq, nband)
        ob = band_prompt(qkvb, bias_p, batch, seq, b_heads, band_tq, nprev, b_scale)
        ob = band_sample(qkvb, cache_band_k, cache_band_v, bias_s, ob, l, dbatch, dseq, nband,
                         b_heads, tp, b_scale)
        x = norm_matmul([oa, ob], [g_out_a[l], g_out_b[l]], w_o_b, layer=l, res=x, name="out_proj")

        memn = rmsnorm(mem_prompt.reshape(batch * mtok, d), norm_memtok[l], BF16, name="norm_memtok")
        mkv = matmul(memn, w_mkv_b, F32, layer=l, name="mem_kv")
        qm = norm_matmul([x], [norm_mem[l]], w_mq_b, layer=l, out_dtype=BF16, name="mem_q")
        om = mem_attend(qm, mkv, mkv, 0, 1, None, nbatch=batch, rows_per_batch=seq, row0=0,
                        heads=m_heads, dim=m_dim, name="mem_prompt")
        om = mem_attend(qm, cache_mem_k, cache_mem_v, l, l, om, nbatch=dbatch, rows_per_batch=dseq,
                        row0=tp, heads=m_heads, dim=m_dim, name="mem_sample")
        x = matmul_residual(om, w_mo_b, x, layer=l, name="mem_out")

        h = rmsnorm(x, norm_ffn[l], BF16, name="norm_ffn")
        u, tail_p = ffn_up(h, w_up_b, w_dw, b_dw, state_conv, None, l, row0=0, rows=tp, seg=ffn_tm,
                           blocks_per_seq=seq // ffn_tm, use_state=False, name="ffn_up_prompt")
        u, tail_s = ffn_up(h, w_up_b, w_dw, b_dw, state_conv, u, l, row0=tp, rows=ts, seg=dseq,
                           blocks_per_seq=1, use_state=True, name="ffn_up_sample")
        y = matmul_residual_ktiled(u, w_down_b, x, tk=tk_down, layer=l, name="ffn_down")
        x = [y]

        outs["p_ckv"].append(ckv[:tp].reshape(batch, seq, kvl))
        outs["s_ckv"].append(ckv[tp:].reshape(dbatch, dseq, kvl))
        kr64 = unslot(kr)
        outs["p_kr"].append(kr64[:tp].reshape(batch, seq, rope))
        outs["s_kr"].append(kr64[tp:].reshape(dbatch, dseq, rope))
        hw = b_heads * b_dim
        nkp = batch * band_keep
        outs["p_bk"].append(kv_keep[:nkp, :hw].reshape(batch, band_keep, b_heads, b_dim))
        outs["p_bv"].append(kv_keep[:nkp, hw:].reshape(batch, band_keep, b_heads, b_dim))
        outs["s_bk"].append(kv_keep[nkp:, :hw].reshape(dbatch, dseq, b_heads, b_dim))
        outs["s_bv"].append(kv_keep[nkp:, hw:].reshape(dbatch, dseq, b_heads, b_dim))
        outs["p_mk"].append(mkv[:, :mem_w].reshape(batch, mtok, m_heads, m_dim))
        outs["p_mv"].append(mkv[:, mem_w:].reshape(batch, mtok, m_heads, m_dim))
        nblk = seq // ffn_tm
        tail_p = tail_p.reshape(batch, nblk, 8, ff)[:, nblk - 1, 8 - (CONV_W - 1):]
        outs["p_conv"].append(tail_p)
        outs["s_conv"].append(tail_s[:, 8 - (CONV_W - 1):])

    y_prompt = rmsnorm(y, norm_final, F32, row0=0, rows=tp, name="norm_final_p").reshape(batch, seq, d)
    y_sample = rmsnorm(y, norm_final, F32, row0=tp, rows=ts, name="norm_final_s").reshape(dbatch, dseq, d)
    st = {k_: jnp.stack(v_) for k_, v_ in outs.items()}
    return (y_prompt, y_sample, st["p_ckv"], st["p_kr"], st["p_bk"], st["p_bv"], st["p_mk"],
            st["p_mv"], st["p_conv"], st["s_ckv"], st["s_kr"], st["s_bk"], st["s_bv"], st["s_conv"])
```

```python
import functools
import math

import jax
import jax.numpy as jnp
import numpy as np
from jax import lax
from jax.experimental import pallas as pl
from jax.experimental.pallas import tpu as pltpu

CHUNK = 64
BAND_PREV = 8
MAX_REL = 128
CONV_W = 3
ROPE_THETA = 10000.0
EPS = 1e-6
NEG_INF = -1e30
LOG2E = math.log2(math.e)

LANES = 128
HEAD_SLOT = 256
VT_SLOT = 144
VMEM_LIMIT = 56 * 1024 * 1024
FFN_CHUNK = 256
FFN_NCHUNK = 2

F32 = jnp.float32
BF16 = jnp.bfloat16


def _cp(*sem):
    return pltpu.CompilerParams(dimension_semantics=sem, vmem_limit_bytes=VMEM_LIMIT)


def _pick(n, prefs):
    for p in prefs:
        if n % p == 0:
            return p
    return n


def _rmsnorm_rows(x, g):
    ms = jnp.mean(x * x, axis=-1, keepdims=True)
    return (x * lax.rsqrt(ms + EPS)) * g


def _rmsnorm_kernel(x_ref, g_ref, o_ref):
    o_ref[...] = _rmsnorm_rows(x_ref[...].astype(F32), g_ref[...]).astype(o_ref.dtype)


def rmsnorm(x, g, out_dtype, *, row0=0, rows=None, name="rmsnorm"):
    m, d = x.shape
    rows = m if rows is None else rows
    tm = _pick(math.gcd(rows, row0) if row0 else rows, (256, 128, 64, 32, 16, 8))
    off = row0 // tm
    return pl.pallas_call(
        _rmsnorm_kernel,
        grid=(rows // tm,),
        in_specs=[pl.BlockSpec((tm, d), lambda i: (i + off, 0)),
                  pl.BlockSpec((1, d), lambda i: (0, 0))],
        out_specs=pl.BlockSpec((tm, d), lambda i: (i, 0)),
        out_shape=jax.ShapeDtypeStruct((rows, d), out_dtype),
        compiler_params=_cp("parallel"),
        name=name,
    )(x, g.reshape(1, d).astype(F32))


def _part_specs(parts, block, col):
    specs, off = [], 0
    for p in parts:
        nb = p.shape[0] // block[0]
        specs.append(pl.BlockSpec(
            block, lambda i, *r, off=off, nb=nb: (jnp.clip(i - off, 0, nb - 1), col(*r))))
        off += nb
    return specs


def _part_rows(refs, parts_blocks, i):
    x, bound = refs[0][...], 0
    for r, nb in zip(refs[1:], parts_blocks[:-1]):
        bound += nb
        x = jnp.where(i >= bound, r[...], x)
    return x


def _rmsnorm_parts_kernel(*refs, blocks):
    g_ref, o_ref = refs[-2:]
    x = _part_rows(refs[:-2], blocks, pl.program_id(0))
    o_ref[...] = _rmsnorm_rows(x, g_ref[...]).astype(o_ref.dtype)


def rmsnorm_parts(parts, g, out_dtype, *, name):
    d = parts[0].shape[1]
    tm = _pick(math.gcd(*[p.shape[0] for p in parts]), (256, 128, 64, 32, 16, 8))
    blocks = tuple(p.shape[0] // tm for p in parts)
    return pl.pallas_call(
        functools.partial(_rmsnorm_parts_kernel, blocks=blocks),
        grid=(sum(blocks),),
        in_specs=_part_specs(parts, (tm, d), lambda: 0) + [pl.BlockSpec((1, d), lambda i: (0, 0))],
        out_specs=pl.BlockSpec((tm, d), lambda i: (i, 0)),
        out_shape=jax.ShapeDtypeStruct((sum(blocks) * tm, d), out_dtype),
        compiler_params=_cp("parallel"),
        name=name,
    )(*parts, g.reshape(1, d).astype(F32))


def _mm_kernel(x_ref, w_ref, o_ref):
    o_ref[...] = jnp.dot(x_ref[...].astype(BF16), w_ref[...],
                         preferred_element_type=F32).astype(o_ref.dtype)


def _wspec(w, layer, block, index_map):
    if w.ndim == 2:
        return pl.BlockSpec(block, index_map)
    return pl.BlockSpec((None,) + block, lambda *a: (layer,) + index_map(*a))


def matmul(x, w, out_dtype, *, layer=0, tm_prefs=(1024, 512, 256, 128, 64, 32, 16, 8),
           tn_prefs=(1024, 512, 256, 128), name="matmul"):
    m, k = x.shape
    n = w.shape[-1]
    tm, tn = _pick(m, tm_prefs), _pick(n, tn_prefs)
    return pl.pallas_call(
        _mm_kernel,
        grid=(m // tm, n // tn),
        in_specs=[pl.BlockSpec((tm, k), lambda i, j: (i, 0)),
                  _wspec(w, layer, (k, tn), lambda i, j: (0, j))],
        out_specs=pl.BlockSpec((tm, tn), lambda i, j: (i, j)),
        out_shape=jax.ShapeDtypeStruct((m, n), out_dtype),
        compiler_params=_cp("parallel", "parallel"),
        name=name,
    )(x, w)


def _mm_res_kernel(x_ref, w_ref, r_ref, o_ref):
    o_ref[...] = r_ref[...] + jnp.dot(x_ref[...], w_ref[...], preferred_element_type=F32)


def matmul_residual(x, w, res, *, layer=0, name="matmul_res"):
    m, k = x.shape
    n = w.shape[-1]
    tm, tn = _pick(m, (1024, 512, 256, 128)), _pick(n, (1024, 512, 256, 128) if k <= 1024 else (512, 256, 128))
    return pl.pallas_call(
        _mm_res_kernel,
        grid=(m // tm, n // tn),
        in_specs=[pl.BlockSpec((tm, k), lambda i, j: (i, 0)),
                  _wspec(w, layer, (k, tn), lambda i, j: (0, j)),
                  pl.BlockSpec((tm, tn), lambda i, j: (i, j))],
        out_specs=pl.BlockSpec((tm, tn), lambda i, j: (i, j)),
        out_shape=jax.ShapeDtypeStruct((m, n), F32),
        input_output_aliases={2: 0},
        compiler_params=_cp("parallel", "parallel"),
        name=name,
    )(x, w, res)


def _mm_res_ktiled_kernel(x_ref, w_ref, r_ref, o_ref, acc_ref):
    kk = pl.program_id(2)

    @pl.when(kk == 0)
    def _():
        acc_ref[...] = jnp.zeros_like(acc_ref)

    acc_ref[...] += jnp.dot(x_ref[...], w_ref[...], preferred_element_type=F32)

    @pl.when(kk == pl.num_programs(2) - 1)
    def _():
        o_ref[...] = r_ref[...] + acc_ref[...]


def matmul_residual_ktiled(x, w, res, *, tk, layer=0, name="matmul_res_k"):
    m, k = x.shape
    n = w.shape[-1]
    tm, tn = _pick(m, (1024, 512, 256, 128)), _pick(n, (512, 256, 128))
    return pl.pallas_call(
        _mm_res_ktiled_kernel,
        grid=(m // tm, n // tn, k // tk),
        in_specs=[pl.BlockSpec((tm, tk), lambda i, j, kk: (i, kk)),
                  _wspec(w, layer, (tk, tn), lambda i, j, kk: (kk, j)),
                  pl.BlockSpec((tm, tn), lambda i, j, kk: (i, j))],
        out_specs=pl.BlockSpec((tm, tn), lambda i, j, kk: (i, j)),
        out_shape=jax.ShapeDtypeStruct((m, n), F32),
        scratch_shapes=[pltpu.VMEM((tm, tn), F32)],
        input_output_aliases={2: 0},
        compiler_params=_cp("parallel", "parallel", "arbitrary"),
        name=name,
    )(x, w, res)


def _rope_slot(r, cos, sin):
    return r * cos + pltpu.roll(r, 64, 1) * sin


def _lat_kernel(h_ref, w_ref, gq_ref, gkv_ref, cos_ref, sin_ref,
                cq_ref, ckv_ref, ckvb_ref, kr_ref, krb_ref, *, ql, kvl):
    acc = jnp.dot(h_ref[...], w_ref[...], preferred_element_type=F32)
    cq_ref[...] = _rmsnorm_rows(acc[:, :ql], gq_ref[...]).astype(BF16)
    ckv = _rmsnorm_rows(acc[:, ql:ql + kvl], gkv_ref[...])
    ckv_ref[...] = ckv
    ckvb_ref[...] = ckv.astype(BF16)
    kr = _rope_slot(acc[:, ql + kvl:], cos_ref[...], sin_ref[...])
    kr_ref[...] = kr
    krb_ref[...] = kr.astype(BF16)


def latent_project(h, w_lat, g_cq, g_ckv, cos_t, sin_t):
    m, d = h.shape
    ql, kvl = g_cq.shape[0], g_ckv.shape[0]
    n = w_lat.shape[1]
    tm = _pick(m, (512, 256, 128))
    row = lambda i: (i, 0)
    fix = lambda i: (0, 0)
    return pl.pallas_call(
        functools.partial(_lat_kernel, ql=ql, kvl=kvl),
        grid=(m // tm,),
        in_specs=[pl.BlockSpec((tm, d), row), pl.BlockSpec((d, n), fix),
                  pl.BlockSpec((1, ql), fix), pl.BlockSpec((1, kvl), fix),
                  pl.BlockSpec((tm, LANES), row), pl.BlockSpec((tm, LANES), row)],
        out_specs=[pl.BlockSpec((tm, ql), row), pl.BlockSpec((tm, kvl), row),
                   pl.BlockSpec((tm, kvl), row), pl.BlockSpec((tm, LANES), row),
                   pl.BlockSpec((tm, LANES), row)],
        out_shape=[jax.ShapeDtypeStruct((m, ql), BF16), jax.ShapeDtypeStruct((m, kvl), F32),
                   jax.ShapeDtypeStruct((m, kvl), BF16), jax.ShapeDtypeStruct((m, LANES), F32),
                   jax.ShapeDtypeStruct((m, LANES), BF16)],
        compiler_params=_cp("parallel"),
        name="latent_project",
    )(h, w_lat, g_cq.reshape(1, ql), g_ckv.reshape(1, kvl), cos_t, sin_t)


def _q_kernel(c_ref, w_ref, cos_ref, sin_ref, o_ref, *, heads, scale):
    acc = jnp.dot(c_ref[...], w_ref[...], preferred_element_type=F32)
    cos, sin = cos_ref[...], sin_ref[...]
    for h in range(heads):
        lo = h * HEAD_SLOT
        o_ref[:, lo:lo + LANES] = (acc[:, lo:lo + LANES] * scale).astype(BF16)
        r = _rope_slot(acc[:, lo + LANES:lo + HEAD_SLOT], cos, sin)
        o_ref[:, lo + LANES:lo + HEAD_SLOT] = (r * scale).astype(BF16)


def q_project(cq, w_uq, cos_t, sin_t, scale, *, row0, rows):
    ql = cq.shape[1]
    n = w_uq.shape[1]
    tm = _pick(math.gcd(rows, row0), (512, 256, 128, 64))
    tn = _pick(n, (1024, 512, 256))
    r0 = row0 // tm
    return pl.pallas_call(
        functools.partial(_q_kernel, heads=tn // HEAD_SLOT, scale=scale),
        grid=(rows // tm, n // tn),
        in_specs=[pl.BlockSpec((tm, ql), lambda i, j: (r0 + i, 0)),
                  pl.BlockSpec((ql, tn), lambda i, j: (0, j)),
                  pl.BlockSpec((tm, LANES), lambda i, j: (r0 + i, 0)),
                  pl.BlockSpec((tm, LANES), lambda i, j: (r0 + i, 0))],
        out_specs=pl.BlockSpec((tm, tn), lambda i, j: (i, j)),
        out_shape=jax.ShapeDtypeStruct((rows, n), BF16),
        compiler_params=_cp("parallel", "parallel"),
        name="q_project",
    )(cq, w_uq, cos_t, sin_t)


def _qt_kernel(c_ref, wt_ref, cos_ref, sin_ref, o_ref, *, heads, scale):
    acc = lax.dot_general(wt_ref[...], c_ref[...], (((1,), (1,)), ((), ())),
                          preferred_element_type=F32)
    cos, sin = cos_ref[...], sin_ref[...]
    for h in range(heads):
        lo = h * HEAD_SLOT
        o_ref[0, lo:lo + LANES, :] = (acc[lo:lo + LANES] * scale).astype(BF16)
        r = acc[lo + LANES:lo + HEAD_SLOT]
        r = r * cos + pltpu.roll(r, 64, 0) * sin
        o_ref[0, lo + LANES:lo + HEAD_SLOT, :] = (r * scale).astype(BF16)


def q_project_t(cq, w_uq_t, cos_tt, sin_tt, scale, *, rows, tq):
    ql = cq.shape[1]
    n = w_uq_t.shape[0]
    tn = _pick(n, (2048, 1024, 512, 256))
    return pl.pallas_call(
        functools.partial(_qt_kernel, heads=tn // HEAD_SLOT, scale=scale),
        grid=(rows // tq, n // tn),
        in_specs=[pl.BlockSpec((tq, ql), lambda i, j: (i, 0)),
                  pl.BlockSpec((tn, ql), lambda i, j: (j, 0)),
                  pl.BlockSpec((LANES, tq), lambda i, j: (0, i)),
                  pl.BlockSpec((LANES, tq), lambda i, j: (0, i))],
        out_specs=pl.BlockSpec((1, tn, tq), lambda i, j: (i, j, 0)),
        out_shape=jax.ShapeDtypeStruct((rows // tq, n, tq), BF16),
        compiler_params=_cp("parallel", "parallel"),
        name="q_project_t",
    )(cq, w_uq_t, cos_tt, sin_tt)


def _kv_kernel(c_ref, kr_ref, wk_ref, wvt_ref, k_ref, vt_ref, *, heads):
    c = c_ref[...]
    kn = jnp.dot(c, wk_ref[...], preferred_element_type=F32)
    vt = lax.dot_general(wvt_ref[...], c, (((1,), (1,)), ((), ())),
                         preferred_element_type=F32).astype(BF16)
    ones = jnp.ones((VT_SLOT - LANES, vt.shape[1]), BF16)
    kr = kr_ref[...]
    for h in range(heads):
        lo = h * HEAD_SLOT
        k_ref[:, lo:lo + LANES] = kn[:, h * LANES:(h + 1) * LANES].astype(BF16)
        k_ref[:, lo + LANES:lo + HEAD_SLOT] = kr
        vt_ref[0, h * VT_SLOT:h * VT_SLOT + LANES, :] = vt[h * LANES:(h + 1) * LANES]
        vt_ref[0, h * VT_SLOT + LANES:(h + 1) * VT_SLOT, :] = ones


def kv_decompress(ckv, kr, w_uk, w_uv_t, *, rows, tm):
    kvl = ckv.shape[1]
    heads = w_uk.shape[1] // LANES
    row = lambda i: (i, 0)
    fix = lambda i: (0, 0)
    return pl.pallas_call(
        functools.partial(_kv_kernel, heads=heads),
        grid=(rows // tm,),
        in_specs=[pl.BlockSpec((tm, kvl), row), pl.BlockSpec((tm, LANES), row),
                  pl.BlockSpec(w_uk.shape, fix), pl.BlockSpec(w_uv_t.shape, fix)],
        out_specs=[pl.BlockSpec((tm, heads * HEAD_SLOT), row),
                   pl.BlockSpec((1, heads * VT_SLOT, tm), lambda i: (i, 0, 0))],
        out_shape=[jax.ShapeDtypeStruct((rows, heads * HEAD_SLOT), BF16),
                   jax.ShapeDtypeStruct((rows // tm, heads * VT_SLOT, tm), BF16)],
        compiler_params=_cp("parallel"),
        name="kv_decompress",
    )(ckv, kr, w_uk, w_uv_t)


def _qk(q, k):
    return lax.dot_general(q, k, (((1,), (1,)), ((), ())), preferred_element_type=F32)


def _mla_prompt_kernel(qt_ref, k_ref, vt_ref, o_ref, m_ref, acc_ref, s0_ref, s1_ref, *, t, nh):
    qi = pl.program_id(2)
    m_ref[...] = jnp.full_like(m_ref, NEG_INF)
    acc_ref[...] = jnp.zeros_like(acc_ref)

    def scores(ki, s_ref):
        start = pl.multiple_of(ki * t, t)
        for h in range(nh):
            s_ref[h] = jnp.dot(k_ref[pl.ds(start, t), h * HEAD_SLOT:(h + 1) * HEAD_SLOT],
                               qt_ref[0, h * HEAD_SLOT:(h + 1) * HEAD_SLOT, :],
                               preferred_element_type=F32)

    def consume(ki, s_ref, mask):
        for h in range(nh):
            s = s_ref[h]
            if mask is not None:
                s = jnp.where(mask, s, NEG_INF)
            m_old = m_ref[h]
            m_new = jnp.maximum(m_old, jnp.max(s, axis=0, keepdims=True))
            alpha = jnp.exp2(m_old - m_new)
            p = jnp.exp2(s - m_new)
            acc_ref[h] = alpha * acc_ref[h] + jnp.dot(
                vt_ref[ki, h * VT_SLOT:(h + 1) * VT_SLOT, :], p.astype(BF16),
                preferred_element_type=F32)
            m_ref[h] = m_new

    scores(0, s0_ref)

    def body(j, carry):
        scores(2 * j + 1, s1_ref)
        consume(2 * j, s0_ref, None)
        scores(2 * j + 2, s0_ref)
        consume(2 * j + 1, s1_ref, None)
        return carry

    lax.fori_loop(0, qi // 2, body, 0)
    kc = lax.broadcasted_iota(jnp.int32, (t, t), 0) // CHUNK
    qc = lax.broadcasted_iota(jnp.int32, (t, t), 1) // CHUNK
    mask = kc <= qc

    @pl.when(qi % 2 == 0)
    def _():
        consume(qi, s0_ref, mask)

    @pl.when(qi % 2 == 1)
    def _():
        scores(qi, s1_ref)
        consume(qi - 1, s0_ref, None)
        consume(qi, s1_ref, mask)

    for h in range(nh):
        o_ref[:, h * LANES:(h + 1) * LANES] = (
            acc_ref[h, :LANES, :] / acc_ref[h, LANES:LANES + 1, :]).T


def mla_prompt(qt, k, vt, rows, batch, seq, heads, t, nh):
    nq = seq // t
    return pl.pallas_call(
        functools.partial(_mla_prompt_kernel, t=t, nh=nh),
        grid=(batch, heads // nh, nq),
        in_specs=[pl.BlockSpec((1, nh * HEAD_SLOT, t), lambda b, g, i: (b * nq + i, g, 0)),
                  pl.BlockSpec((seq, nh * HEAD_SLOT), lambda b, g, i: (b, g),
                               pipeline_mode=pl.Buffered(1)),
                  pl.BlockSpec((nq, nh * VT_SLOT, t), lambda b, g, i: (b, g, 0),
                               pipeline_mode=pl.Buffered(1))],
        out_specs=pl.BlockSpec((t, nh * LANES), lambda b, g, i: (b * nq + i, g)),
        out_shape=jax.ShapeDtypeStruct((rows, heads * LANES), F32),
        scratch_shapes=[pltpu.VMEM((nh, 1, t), F32), pltpu.VMEM((nh, VT_SLOT, t), F32),
                        pltpu.VMEM((nh, t, t), F32), pltpu.VMEM((nh, t, t), F32)],
        compiler_params=_cp("parallel", "parallel", "arbitrary"),
        name="mla_prompt",
    )(qt, k, vt)


def _q_absorb_kernel(q_ref, wk_ref, o_ref):
    kvl = wk_ref.shape[0]
    o_ref[0, :, :kvl] = lax.dot_general(q_ref[:, :LANES], wk_ref[...], (((1,), (1,)), ((), ())),
                                        preferred_element_type=F32).astype(BF16)
    o_ref[0, :, kvl:] = q_ref[:, LANES:]


def _mla_latent_kernel(q_ref, cc_ref, kc_ref, perm_ref, cn_ref, kn_ref, o_ref):
    heads, dseq, width = q_ref.shape
    q = q_ref[...].reshape(heads * dseq, width)
    cc = cc_ref[...].astype(BF16)
    krc = jnp.dot(kc_ref[...].astype(BF16), perm_ref[...], preferred_element_type=F32).astype(BF16)
    cn = cn_ref[...]
    s1 = _qk(q, jnp.concatenate([cc, krc], axis=1))
    s2 = _qk(q, jnp.concatenate([cn, kn_ref[...]], axis=1))
    m = jnp.maximum(jnp.max(s1, axis=-1, keepdims=True), jnp.max(s2, axis=-1, keepdims=True))
    p1 = jnp.exp2(s1 - m)
    p2 = jnp.exp2(s2 - m)
    l = jnp.sum(p1, axis=-1, keepdims=True) + jnp.sum(p2, axis=-1, keepdims=True)
    o = (jnp.dot(p1.astype(BF16), cc, preferred_element_type=F32)
         + jnp.dot(p2.astype(BF16), cn, preferred_element_type=F32)) / l
    o_ref[...] = o.astype(BF16).reshape(heads, dseq, o.shape[1])


def _o_absorb_kernel(ol_ref, wv_ref, oin_ref, o_ref):
    del oin_ref
    o_ref[...] = jnp.dot(ol_ref[0], wv_ref[...], preferred_element_type=F32)


def mla_sample_latent(q_s, c_ckv, c_kr, ckv_b, kr_b, w_uk, w_uv, perm, o_buf, layer,
                      dbatch, dseq, past, heads, row0):
    ts, kvl = q_s.shape[0], w_uk.shape[0]
    width = kvl + LANES
    q_abs = pl.pallas_call(
        _q_absorb_kernel,
        grid=(heads,),
        in_specs=[pl.BlockSpec((ts, HEAD_SLOT), lambda h: (0, h)),
                  pl.BlockSpec((kvl, LANES), lambda h: (0, h))],
        out_specs=pl.BlockSpec((1, ts, width), lambda h: (h, 0, 0)),
        out_shape=jax.ShapeDtypeStruct((heads, ts, width), BF16),
        compiler_params=_cp("parallel"),
        name="mla_q_absorb",
    )(q_s, w_uk)
    r0 = row0 // dseq
    o_lat = pl.pallas_call(
        _mla_latent_kernel,
        grid=(dbatch,),
        in_specs=[pl.BlockSpec((heads, dseq, width), lambda b: (0, b, 0)),
                  pl.BlockSpec((None, past, kvl), lambda b: (layer, b, 0)),
                  pl.BlockSpec((None, past, c_kr.shape[2]), lambda b: (layer, b, 0)),
                  pl.BlockSpec(perm.shape, lambda b: (0, 0)),
                  pl.BlockSpec((dseq, kvl), lambda b: (r0 + b, 0)),
                  pl.BlockSpec((dseq, LANES), lambda b: (r0 + b, 0))],
        out_specs=pl.BlockSpec((heads, dseq, kvl), lambda b: (0, b, 0)),
        out_shape=jax.ShapeDtypeStruct((heads, ts, kvl), BF16),
        compiler_params=_cp("parallel"),
        name="mla_sample_latent",
    )(q_abs, c_ckv, c_kr, perm, ckv_b, kr_b)
    return pl.pallas_call(
        _o_absorb_kernel,
        grid=(heads,),
        in_specs=[pl.BlockSpec((1, ts, kvl), lambda h: (h, 0, 0)),
                  pl.BlockSpec((kvl, LANES), lambda h: (0, h)),
                  pl.BlockSpec(memory_space=pl.ANY)],
        out_specs=pl.BlockSpec((ts, LANES), lambda h: (row0 // ts, h)),
        out_shape=jax.ShapeDtypeStruct(o_buf.shape, F32),
        input_output_aliases={2: 0},
        compiler_params=_cp("parallel"),
        name="mla_o_absorb",
    )(o_lat, w_uv, o_buf)


def _band_prompt_kernel(q_ref, k0_ref, k1_ref, k2_ref, v0_ref, v1_ref, v2_ref, bias_ref, o_ref,
                        *, tq, nprev, scale, nh):
    qi = pl.program_id(2)
    ks = (k0_ref, k1_ref, k2_ref)[3 - nprev - 1:]
    vs = (v0_ref, v1_ref, v2_ref)[3 - nprev - 1:]
    raw = [[_qk(q_ref[:, h * LANES:(h + 1) * LANES], kr[:, h * LANES:(h + 1) * LANES]) for kr in ks]
           for h in range(nh)]
    for h in range(nh):
        hs = slice(h * LANES, (h + 1) * LANES)
        ss = []
        for d in range(len(ks)):
            s = raw[h][d] * scale + bias_ref[h, :, d * tq:(d + 1) * tq]
            if d < nprev:
                s = jnp.where(qi - (nprev - d) >= 0, s, NEG_INF)
            ss.append(s)
        m = functools.reduce(jnp.maximum, [jnp.max(s, axis=-1, keepdims=True) for s in ss])
        ps = [jnp.exp(s - m) for s in ss]
        l = functools.reduce(lambda a, b: a + b, [jnp.sum(p, axis=-1, keepdims=True) for p in ps])
        o = functools.reduce(lambda a, b: a + b,
                             [jnp.dot(p.astype(BF16), vr[:, hs], preferred_element_type=F32)
                              for p, vr in zip(ps, vs)])
        o_ref[:, hs] = o / l


def band_prompt(qkv, bias, batch, seq, heads, tq, nprev, scale):
    t = qkv.shape[0]
    nq = seq // tq
    nh = _pick(heads, (8, 4, 2, 1))
    ng = heads // nh

    def kspec(back, col0):
        return pl.BlockSpec((tq, nh * LANES),
                            lambda b, g, i: (b * nq + jnp.maximum(i - back, 0), col0 + g))

    return pl.pallas_call(
        functools.partial(_band_prompt_kernel, tq=tq, nprev=nprev, scale=scale, nh=nh),
        grid=(batch, ng, nq),
        in_specs=[pl.BlockSpec((tq, nh * LANES), lambda b, g, i: (b * nq + i, g)),
                  kspec(2, ng), kspec(1, ng), kspec(0, ng),
                  kspec(2, 2 * ng), kspec(1, 2 * ng), kspec(0, 2 * ng),
                  pl.BlockSpec((nh, tq, (nprev + 1) * tq), lambda b, g, i: (g, 0, 0))],
        out_specs=pl.BlockSpec((tq, nh * LANES), lambda b, g, i: (b * nq + i, g)),
        out_shape=jax.ShapeDtypeStruct((t, heads * LANES), F32),
        compiler_params=_cp("parallel", "parallel", "arbitrary"),
        name="band_prompt",
    )(qkv, qkv, qkv, qkv, qkv, qkv, qkv, bias)


def _band_sample_kernel(q_ref, kc_ref, vc_ref, kn_ref, vn_ref, bias_ref, oin_ref, o_ref,
                        *, nb, scale, nh):
    del oin_ref
    for h in range(nh):
        hs = slice(h * LANES, (h + 1) * LANES)
        q = q_ref[:, hs]
        s1 = _qk(q, kc_ref[:, h, :].astype(BF16)) * scale + bias_ref[h, :, :nb]
        s2 = _qk(q, kn_ref[:, hs]) * scale + bias_ref[h, :, nb:]
        m = jnp.maximum(jnp.max(s1, axis=-1, keepdims=True), jnp.max(s2, axis=-1, keepdims=True))
        p1 = jnp.exp(s1 - m)
        p2 = jnp.exp(s2 - m)
        l = jnp.sum(p1, axis=-1, keepdims=True) + jnp.sum(p2, axis=-1, keepdims=True)
        o = (jnp.dot(p1.astype(BF16), vc_ref[:, h, :].astype(BF16), preferred_element_type=F32)
             + jnp.dot(p2.astype(BF16), vn_ref[:, hs], preferred_element_type=F32))
        o_ref[:, hs] = o / l


def band_sample(qkv, kc, vc, bias, o_buf, layer, dbatch, dseq, nb, heads, row0, scale):
    r0 = row0 // dseq
    nh, ng = heads, 1
    w = nh * LANES
    cspec = pl.BlockSpec((None, None, nb, heads, LANES), lambda b, g: (layer, b, 0, 0, 0))
    return pl.pallas_call(
        functools.partial(_band_sample_kernel, nb=nb, scale=scale, nh=nh),
        grid=(dbatch, ng),
        in_specs=[pl.BlockSpec((dseq, w), lambda b, g: (r0 + b, g)),
                  cspec, cspec,
                  pl.BlockSpec((dseq, w), lambda b, g: (r0 + b, ng + g)),
                  pl.BlockSpec((dseq, w), lambda b, g: (r0 + b, 2 * ng + g)),
                  pl.BlockSpec((nh, dseq, nb + dseq), lambda b, g: (g, 0, 0)),
                  pl.BlockSpec(memory_space=pl.ANY)],
        out_specs=pl.BlockSpec((dseq, w), lambda b, g: (r0 + b, g)),
        out_shape=jax.ShapeDtypeStruct(o_buf.shape, F32),
        input_output_aliases={6: 0},
        compiler_params=_cp("parallel", "parallel"),
        name="band_sample",
    )(qkv, kc, vc, qkv, qkv, bias, o_buf)


def _band_bias_table(rel_bias, nq, nk, q0):
    span = nq + nk - 1
    k = np.concatenate([np.arange(0, nk), np.arange(-(nq - 1), 0)])
    idx = np.clip(q0 - k, -MAX_REL, MAX_REL) + MAX_REL
    u = rel_bias.astype(F32)[:, idx]
    tab = jnp.tile(u, (1, nq))[:, :nq * (span - 1)].reshape(-1, nq, span - 1)[:, :, :nk]
    qc = (q0 + np.arange(nq))[:, None] // CHUNK
    kc = np.arange(nk)[None, :] // CHUNK
    mask = (kc <= qc) & (kc >= qc - BAND_PREV)
    return jnp.where(mask[None], tab, NEG_INF)


def _norm_mm_kernel(*refs, nparts, res_blocks):
    xs, gs = refs[:nparts], refs[nparts:2 * nparts]
    w_ref = refs[2 * nparts]
    r_refs = refs[2 * nparts + 1:-2]
    o_ref, hs_ref = refs[-2:]

    @pl.when(pl.program_id(1) == 0)
    def _():
        off = 0
        for x_ref, g_ref in zip(xs, gs):
            hp = _rmsnorm_rows(x_ref[...], g_ref[...]).astype(BF16)
            hs_ref[:, off:off + hp.shape[1]] = hp
            off += hp.shape[1]

    acc = jnp.dot(hs_ref[...], w_ref[...], preferred_element_type=F32)
    if r_refs:
        o_ref[...] = _part_rows(r_refs, res_blocks, pl.program_id(0)) + acc
    else:
        o_ref[...] = acc.astype(o_ref.dtype)


def norm_matmul(parts, gains, w, *, layer=0, res=(), out_dtype=F32, name):
    m = parts[0].shape[0]
    widths = [p.shape[1] for p in parts]
    k, n = sum(widths), w.shape[-1]
    tm, tn = _pick(math.gcd(m, *[r.shape[0] for r in res]), (512, 256, 128)), _pick(n, (1024, 512, 256, 128))
    in_specs = ([pl.BlockSpec((tm, dp), lambda i, j: (i, 0)) for dp in widths]
                + [pl.BlockSpec((1, dp), lambda i, j: (0, 0)) for dp in widths]
                + [_wspec(w, layer, (k, tn), lambda i, j: (0, j))]
                + _part_specs(res, (tm, tn), lambda j: j))
    args = list(parts) + [g.reshape(1, -1).astype(F32) for g in gains] + [w] + list(res)
    alias = {len(args) - 1: 0} if len(res) == 1 else {}
    return pl.pallas_call(
        functools.partial(_norm_mm_kernel, nparts=len(parts),
                          res_blocks=tuple(r.shape[0] // tm for r in res)),
        grid=(m // tm, n // tn),
        in_specs=in_specs,
        out_specs=pl.BlockSpec((tm, tn), lambda i, j: (i, j)),
        out_shape=jax.ShapeDtypeStruct((m, n), F32 if res else out_dtype),
        scratch_shapes=[pltpu.VMEM((tm, k), BF16)],
        input_output_aliases=alias,
        compiler_params=_cp("parallel", "arbitrary"),
        name=name,
    )(*args)


def _mem_kernel(q_ref, k_ref, v_ref, *rest, heads, dim, scale):
    o_ref = rest[-1]
    for h in range(heads):
        sl = slice(h * dim, (h + 1) * dim)
        k = k_ref[:, sl] if len(k_ref.shape) == 2 else k_ref[:, h, :]
        v = v_ref[:, sl] if len(v_ref.shape) == 2 else v_ref[:, h, :]
        s = _qk(q_ref[:, sl], k.astype(BF16)) * scale
        m = jnp.max(s, axis=-1, keepdims=True)
        p = jnp.exp(s - m)
        l = jnp.sum(p, axis=-1, keepdims=True)
        o = jnp.dot(p.astype(BF16), v.astype(BF16), preferred_element_type=F32)
        o_ref[:, sl] = (o / l).astype(BF16)


def mem_attend(q, mk, mv, kcol, vcol, o_buf, *, nbatch, rows_per_batch, row0, heads, dim, name):
    t, w = q.shape
    tq = _pick(rows_per_batch, (512, 256, 128, 64))
    nq = rows_per_batch // tq
    r0 = row0 // tq
    if mk.ndim == 2:
        mtok = mk.shape[0] // nbatch
        kspec = pl.BlockSpec((mtok, w), lambda b, i: (b, kcol))
        vspec = pl.BlockSpec((mtok, w), lambda b, i: (b, vcol))
    else:
        kspec = vspec = pl.BlockSpec((None, None) + mk.shape[2:], lambda b, i: (kcol, b, 0, 0, 0))
    in_specs = [pl.BlockSpec((tq, w), lambda b, i: (r0 + b * nq + i, 0)), kspec, vspec]
    args = [q, mk, mv]
    alias = {}
    if o_buf is not None:
        in_specs.append(pl.BlockSpec(memory_space=pl.ANY))
        args.append(o_buf)
        alias = {3: 0}
    return pl.pallas_call(
        functools.partial(_mem_kernel, heads=heads, dim=dim, scale=dim ** -0.5),
        grid=(nbatch, nq),
        in_specs=in_specs,
        out_specs=pl.BlockSpec((tq, w), lambda b, i: (r0 + b * nq + i, 0)),
        out_shape=jax.ShapeDtypeStruct((t, w), BF16),
        input_output_aliases=alias,
        compiler_params=_cp("parallel", "parallel"),
        name=name,
    )(*args)


def _ffn_up_kernel(x_ref, *refs, seg, nseg, nch, blocks_per_seq, use_state, has_buf):
    wa_refs, wg_refs = refs[:nch], refs[nch:2 * nch]
    wdw_ref, bdw_ref, st_ref = refs[2 * nch:2 * nch + 3]
    rest = refs[2 * nch + 3 + (1 if has_buf else 0):]
    if use_state:
        u_ref, tail_ref = rest
        carry_ref = None
    else:
        u_ref, tail_ref, carry_ref = rest
    i, j = pl.program_id(0), pl.program_id(1)
    tc = FFN_CHUNK
    row = lax.broadcasted_iota(jnp.int32, (8, tc), 0)
    if not use_state:
        @pl.when(i % blocks_per_seq == 0)
        def _():
            carry_ref[j] = jnp.zeros(carry_ref.shape[1:], F32)
    for c in range(nch):
        cs = slice(c * tc, (c + 1) * tc)
        a = jnp.dot(x_ref[...], wa_refs[c][...], preferred_element_type=F32)
        g = jnp.dot(x_ref[...], wg_refs[c][...], preferred_element_type=F32)
        w0, w1, w2 = wdw_ref[0:1, cs], wdw_ref[1:2, cs], wdw_ref[2:3, cs]
        for s in range(nseg):
            gs = g[s * seg:(s + 1) * seg]
            if use_state:
                p2, p1 = st_ref[s, 0:1, cs], st_ref[s, 1:2, cs]
            else:
                p2, p1 = carry_ref[j, 6:7, cs], carry_ref[j, 7:8, cs]
            prev8 = jnp.where(row == 6, p2, jnp.where(row == 7, p1, 0.0))
            g3 = gs.reshape(seg // 8, 8, tc)
            shifted = []
            for sh in (1, 2):
                cur = pltpu.roll(g3, sh, 1)
                before = jnp.concatenate([pltpu.roll(prev8, sh, 0)[None], cur[:-1]], axis=0)
                shifted.append(jnp.where(row[None] >= sh, cur, before).reshape(seg, tc))
            gm1, gm2 = shifted
            gc = ((bdw_ref[:, cs] + w0 * gm2) + w1 * gm1) + w2 * gs
            u_ref[s * seg:(s + 1) * seg, cs] = (
                a[s * seg:(s + 1) * seg] * (gc * jax.nn.sigmoid(gc))).astype(BF16)
            tail = gs[seg - 8:seg]
            tail_ref[s, :, cs] = tail
            if not use_state:
                carry_ref[j, :, cs] = tail


def ffn_up(h, w_up, w_dw, b_dw, state, u_buf, layer, *, row0, rows, seg, blocks_per_seq, use_state,
           name):
    t, d = h.shape
    ff = w_dw.shape[2]
    nblk = ff // FFN_CHUNK
    assert nblk * FFN_CHUNK == ff
    nch = FFN_NCHUNK if (u_buf is None and ff >= FFN_NCHUNK * FFN_CHUNK) else 1
    tn = nch * FFN_CHUNK
    if use_state:
        tm, nseg = rows, rows // seg
    else:
        tm, nseg = seg, 1
    r0 = row0 // tm
    ni, nj = rows // tm, pl.cdiv(ff, tn)
    scratch = [] if use_state else [pltpu.VMEM((nj, 8, tn), F32)]

    def wspec(half, c):
        return pl.BlockSpec(
            (None, d, FFN_CHUNK),
            lambda i, j: (layer, 0, jnp.minimum(half * nblk + j * nch + c, 2 * nblk - 1)))

    in_specs = ([pl.BlockSpec((tm, d), lambda i, j: (r0 + i, 0))]
                + [wspec(0, c) for c in range(nch)] + [wspec(1, c) for c in range(nch)]
                + [pl.BlockSpec((None, CONV_W, tn), lambda i, j: (layer, 0, j)),
                   pl.BlockSpec((None, 1, tn), lambda i, j: (layer, 0, j)),
                   pl.BlockSpec((None, state.shape[1], CONV_W - 1, tn),
                                lambda i, j: (layer, 0, 0, j))])
    args = [h] + [w_up] * (2 * nch) + [w_dw, b_dw.reshape(b_dw.shape[0], 1, ff), state]
    alias = {}
    if u_buf is not None:
        alias = {len(args): 0}
        in_specs.append(pl.BlockSpec(memory_space=pl.ANY))
        args.append(u_buf)
    return pl.pallas_call(
        functools.partial(_ffn_up_kernel, seg=seg, nseg=nseg, nch=nch,
                          blocks_per_seq=blocks_per_seq, use_state=use_state,
                          has_buf=u_buf is not None),
        grid=(ni, nj),
        in_specs=in_specs,
        out_specs=[pl.BlockSpec((tm, tn), lambda i, j: (r0 + i, j)),
                   pl.BlockSpec((nseg, 8, tn), lambda i, j: (i, 0, j))],
        out_shape=[jax.ShapeDtypeStruct((t, ff), BF16),
                   jax.ShapeDtypeStruct((ni * nseg, 8, ff), F32)],
        scratch_shapes=scratch,
        input_output_aliases=alias,
        compiler_params=_cp("arbitrary", "arbitrary"),
        name=name,
    )(*args)


def _rope_tables(pos):
    half = 32
    inv = ROPE_THETA ** (-jnp.arange(half, dtype=F32) / half)
    ang = pos.astype(F32)[:, None] * inv[None, :]
    c, s, z = jnp.cos(ang), jnp.sin(ang), jnp.zeros_like(ang)
    return jnp.concatenate([c, z, c, z], axis=1), jnp.concatenate([-s, z, s, z], axis=1)


def _slot_cols(w, half):
    z = jnp.zeros(w.shape[:-1] + (LANES // 2 - half,), w.dtype)
    return jnp.concatenate([w[..., :half], z, w[..., half:], z], axis=-1)


def kernel(x_prompt, x_sample, mem_prompt, cache_mla_ckv, cache_mla_krope, cache_band_k, cache_band_v, cache_mem_k, cache_mem_v, state_conv, norm_mix, w_in, norm_cq, norm_ckv, w_uq, w_uk, w_uv, rel_bias, g_out_a, g_out_b, w_o, norm_mem, norm_memtok, w_mq, w_mkv, w_mo, norm_ffn, w_up, w_dw, b_dw, w_down, norm_final):
    batch, seq, d = x_prompt.shape
    dbatch, dseq, _ = x_sample.shape
    depth = norm_mix.shape[0]
    past = cache_mla_ckv.shape[2]
    nband = cache_band_k.shape[2]
    ql, kvl = norm_cq.shape[1], norm_ckv.shape[1]
    rope = cache_mla_krope.shape[3]
    half = rope // 2
    a_heads, a_nope = w_uk.shape[2], w_uk.shape[3]
    a_vdim = w_uv.shape[3]
    b_heads, b_dim = cache_band_k.shape[3], cache_band_k.shape[4]
    mtok, m_heads, m_dim = cache_mem_k.shape[2], cache_mem_k.shape[3], cache_mem_k.shape[4]
    mem_w = m_heads * m_dim
    ff = b_dw.shape[1]
    assert a_nope == LANES and a_vdim == LANES and b_dim == LANES and rope == LANES // 2
    assert seq % CHUNK == 0 and dseq == CHUNK and past % CHUNK == 0

    tp, ts = batch * seq, dbatch * dseq
    t = tp + ts
    band_keep = min(BAND_PREV * CHUNK, seq)
    mla_scale = (a_nope + rope) ** -0.5
    b_scale = b_dim ** -0.5

    pos = jnp.concatenate([jnp.tile(jnp.arange(seq, dtype=jnp.int32), batch),
                           jnp.tile(past + jnp.arange(dseq, dtype=jnp.int32), dbatch)])
    cos_t, sin_t = _rope_tables(pos)
    band_tq = _pick(seq, (256, 128, 64))
    nprev = (BAND_PREV * CHUNK) // band_tq
    assert nprev * band_tq == BAND_PREV * CHUNK and nprev <= 2
    perm = _slot_cols(jnp.eye(rope, dtype=BF16), half)
    tk_down = _pick(ff, (5504, 2816, 2048, 1024, 512, 256))
    ffn_tm = _pick(math.gcd(seq, 1024), (1024, 512, 256, 128))
    mla_t = _pick(seq, (512, 256, 128))
    mla_nh = _pick(a_heads, (4, 2, 1))
    cos_tt, sin_tt = cos_t[:tp].T, sin_t[:tp].T

    w_up_b, w_down_b, w_o_b = w_up.astype(BF16), w_down.astype(BF16), w_o.astype(BF16)
    w_mq_b, w_mkv_b, w_mo_b = w_mq.astype(BF16), w_mkv.astype(BF16), w_mo.astype(BF16)
    c_ckv = cache_mla_ckv.reshape(depth, dbatch * past, kvl)
    c_kr = cache_mla_krope.reshape(depth, dbatch * past, rope)

    x = [x_prompt.reshape(tp, d), x_sample.reshape(ts, d)]
    outs = {k: [] for k in ("p_ckv", "p_kr", "p_bk", "p_bv", "p_mk", "p_mv", "p_conv",
                            "s_ckv", "s_kr", "s_bk", "s_bv", "s_conv")}

    def unslot(kr):
        return jnp.concatenate([kr[:, :half], kr[:, LANES // 2:LANES // 2 + half]], axis=1)

    for l in range(depth):
        wi = w_in[l]
        w_lat = jnp.concatenate([wi[:, :ql + kvl], _slot_cols(wi[:, ql + kvl:ql + kvl + rope], half)],
                                axis=1).astype(BF16)
        w_qkvb = wi[:, ql + kvl + rope:].astype(BF16)
        wq = w_uq[l].reshape(ql, a_heads, a_nope + rope)
        wq = jnp.concatenate([wq[..., :a_nope], _slot_cols(wq[..., a_nope:], half)], axis=-1)
        wq = wq.reshape(ql, a_heads * HEAD_SLOT).astype(BF16)
        wq_t = wq.T
        wuk = w_uk[l].reshape(kvl, a_heads * a_nope).astype(BF16)
        wuv = w_uv[l].reshape(kvl, a_heads * a_vdim).astype(BF16)
        wuv_t = wuv.T

        h = rmsnorm_parts(x, norm_mix[l], BF16, name="norm_mix")
        cq, ckv, ckv_b, kr, kr_b = latent_project(h, w_lat, norm_cq[l], norm_ckv[l], cos_t, sin_t)
        qkvb = matmul(h, w_qkvb, BF16, name="qkv_band")
        sel = jnp.concatenate([h[b * seq + seq - band_keep:(b + 1) * seq] for b in range(batch)]
                              + [h[tp:]], axis=0)
        kv_keep = matmul(sel, w_qkvb[:, b_heads * b_dim:], F32, name="kv_band_keep")
        q_t = q_project_t(cq, wq_t, cos_tt, sin_tt, mla_scale * LOG2E, rows=tp, tq=mla_t)
        q_s = q_project(cq, wq, cos_t, sin_t, mla_scale * LOG2E, row0=tp, rows=ts)
        k, v_t = kv_decompress(ckv_b, kr_b, wuk, wuv_t, rows=tp, tm=mla_t)
        oa = mla_prompt(q_t, k, v_t, t, batch, seq, a_heads, mla_t, mla_nh)
        oa = mla_sample_latent(q_s, c_ckv, c_kr, ckv_b, kr_b, wuk, wuv, perm, oa, l,
                               dbatch, dseq, past, a_heads, tp)
        bias_p = _band_bias_table(rel_bias[l], band_tq, (nprev + 1) * band_tq, nprev * band_tq)
        bias_s = _band_bias_table(rel_bias[l], dseq, nband + dseq, nband)
        ob = band_prompt(qkvb, bias_p, batch, seq, b_heads, band_tq, nprev, b_scale)
        ob = band_sample(qkvb, cache_band_k, cache_band_v, bias_s, ob, l, dbatch, dseq, nband,
                         b_heads, tp, b_scale)
        x = norm_matmul([oa, ob], [g_out_a[l], g_out_b[l]], w_o_b, layer=l, res=x, name="out_proj")

        memn = rmsnorm(mem_prompt.reshape(batch * mtok, d), norm_memtok[l], BF16, name="norm_memtok")
        mkv = matmul(memn, w_mkv_b, F32, layer=l, name="mem_kv")
        qm = norm_matmul([x], [norm_mem[l]], w_mq_b, layer=l, out_dtype=BF16, name="mem_q")
        om = mem_attend(qm, mkv, mkv, 0, 1, None, nbatch=batch, rows_per_batch=seq, row0=0,
                        heads=m_heads, dim=m_dim, name="mem_prompt")
        om = mem_attend(qm, cache_mem_k, cache_mem_v, l, l, om, nbatch=dbatch, rows_per_batch=dseq,
                        row0=tp, heads=m_heads, dim=m_dim, name="mem_sample")
        x = matmul_residual(om, w_mo_b, x, layer=l, name="mem_out")

        h = rmsnorm(x, norm_ffn[l], BF16, name="norm_ffn")
        u, tail_p = ffn_up(h, w_up_b, w_dw, b_dw, state_conv, None, l, row0=0, rows=tp, seg=ffn_tm,
                           blocks_per_seq=seq // ffn_tm, use_state=False, name="ffn_up_prompt")
        u, tail_s = ffn_up(h, w_up_b, w_dw, b_dw, state_conv, u, l, row0=tp, rows=ts, seg=dseq,
                           blocks_per_seq=1, use_state=True, name="ffn_up_sample")
        y = matmul_residual_ktiled(u, w_down_b, x, tk=tk_down, layer=l, name="ffn_down")
        x = [y]

        outs["p_ckv"].append(ckv[:tp].reshape(batch, seq, kvl))
        outs["s_ckv"].append(ckv[tp:].reshape(dbatch, dseq, kvl))
        kr64 = unslot(kr)
        outs["p_kr"].append(kr64[:tp].reshape(batch, seq, rope))
        outs["s_kr"].append(kr64[tp:].reshape(dbatch, dseq, rope))
        hw = b_heads * b_dim
        nkp = batch * band_keep
        outs["p_bk"].append(kv_keep[:nkp, :hw].reshape(batch, band_keep, b_heads, b_dim))
        outs["p_bv"].append(kv_keep[:nkp, hw:].reshape(batch, band_keep, b_heads, b_dim))
        outs["s_bk"].append(kv_keep[nkp:, :hw].reshape(dbatch, dseq, b_heads, b_dim))
        outs["s_bv"].append(kv_keep[nkp:, hw:].reshape(dbatch, dseq, b_heads, b_dim))
        outs["p_mk"].append(mkv[:, :mem_w].reshape(batch, mtok, m_heads, m_dim))
        outs["p_mv"].append(mkv[:, mem_w:].reshape(batch, mtok, m_heads, m_dim))
        nblk = seq // ffn_tm
        tail_p = tail_p.reshape(batch, nblk, 8, ff)[:, nblk - 1, 8 - (CONV_W - 1):]
        outs["p_conv"].append(tail_p)
        outs["s_conv"].append(tail_s[:, 8 - (CONV_W - 1):])

    y_prompt = rmsnorm(y, norm_final, F32, row0=0, rows=tp, name="norm_final_p").reshape(batch, seq, d)
    y_sample = rmsnorm(y, norm_final, F32, row0=tp, rows=ts, name="norm_final_s").reshape(dbatch, dseq, d)
    st = {k_: jnp.stack(v_) for k_, v_ in outs.items()}
    return (y_prompt, y_sample, st["p_ckv"], st["p_kr"], st["p_bk"], st["p_bv"], st["p_mk"],
            st["p_mv"], st["p_conv"], st["s_ckv"], st["s_kr"], st["s_bk"], st["s_bv"], st["s_conv"])
```
